```python
import math
import jax, jax.numpy as jnp
from jax import lax
import numpy as np

D_MODEL = 1024
BATCH = 16
SEQ = 256
DEPTH = 2
DEC_BATCH = 8
DEC_SEQ = 2048
PAST_LEN = 256

GRID_W = 64
POOL_WIDTH = D_MODEL // 4
POOL_WINDOWS = (2, 4, 8, 16)
POOL_GROUPS = len(POOL_WINDOWS)
POOL_GC = POOL_WIDTH // POOL_GROUPS
ATTN_HEADS = 4
HEAD_DIM = 64
V_DIM = 2 * HEAD_DIM
ATTN_WIDTH = ATTN_HEADS * V_DIM
QK_WIDTH = ATTN_HEADS * 2 * HEAD_DIM
CONV_WIDTH = D_MODEL - POOL_WIDTH - ATTN_WIDTH
CONV_K = 31
MIX_WIDTH = POOL_WIDTH + ATTN_WIDTH + CONV_WIDTH
IN_WIDTH = POOL_WIDTH + 2 * QK_WIDTH + ATTN_WIDTH + 2 * CONV_WIDTH
AXIS_DIM = HEAD_DIM // 2
ROPE_THETA = 10000.0
N_GROUPS = 4
EXPERTS_PER_GROUP = 8
N_EXPERTS = N_GROUPS * EXPERTS_PER_GROUP
TOP_K = 2
EXPERT_FF = D_MODEL // 4
ALPHA = (2 * DEPTH) ** 0.25
BETA = (8 * DEPTH) ** -0.25
LN_EPS = 1e-5
Q_BLOCK = 128

kernel_name = "hybrid_pool_diffattn_conformer_hmoe_step"


def layer_norm(x, g, b):
    x32 = x.astype(jnp.float32)
    mu = jnp.mean(x32, axis=-1, keepdims=True)
    var = jnp.mean(jnp.square(x32 - mu), axis=-1, keepdims=True)
    return ((x32 - mu) * lax.rsqrt(var + LN_EPS) * g + b).astype(x.dtype)


def rope_tables(L):
    rows = L // GRID_W
    row_id = jnp.repeat(jnp.arange(rows), GRID_W).astype(jnp.float32)
    col_id = jnp.tile(jnp.arange(GRID_W), rows).astype(jnp.float32)
    inv = ROPE_THETA ** (-jnp.arange(0, AXIS_DIM, 2, dtype=jnp.float32) / AXIS_DIM)
    ang = jnp.stack([row_id[:, None] * inv, col_id[:, None] * inv], axis=1)
    ang = ang[None, :, None, None, :, None, :]
    return jnp.cos(ang), jnp.sin(ang)


def apply_rope(x, cos, sin):
    xs = x.reshape(x.shape[:-1] + (2, 2, AXIS_DIM // 2))
    rot = jnp.stack([-xs[..., 1, :], xs[..., 0, :]], axis=-2)
    return (xs * cos + rot * sin).astype(x.dtype).reshape(x.shape)


def multiscale_pool(a, w_pool, pool_scale):
    B, L, _ = a.shape
    a32 = a.astype(jnp.float32)
    cs = jnp.concatenate([jnp.zeros((B, 1, POOL_WIDTH), jnp.float32), jnp.cumsum(a32, axis=1)], axis=1)
    t = jnp.arange(L)
    means = []
    for g, w in enumerate(POOL_WINDOWS):
        lo = jnp.clip(t - w // 2, 0, L)
        hi = jnp.clip(t + w // 2, 0, L)
        sl = slice(g * POOL_GC, (g + 1) * POOL_GC)
        s = cs[:, hi, sl] - cs[:, lo, sl]
        means.append(s / (hi - lo).astype(jnp.float32)[None, :, None])
    pooled = jnp.stack(means, axis=2) - a32.reshape(B, L, POOL_GROUPS, POOL_GC)
    mixed = jnp.einsum('blgc,gcd->blgd', pooled.astype(a.dtype), w_pool)
    return mixed.reshape(B, L, POOL_WIDTH) * pool_scale


def diff_attention(q, k, v, lam):
    B, Lq = q.shape[:2]
    nb = Lq // Q_BLOCK
    qb = q.reshape(B, nb, Q_BLOCK, ATTN_HEADS, 2, HEAD_DIM).transpose(1, 0, 2, 3, 4, 5)
    scale = 1.0 / math.sqrt(HEAD_DIM)

    def one_block(qblk):
        s = jnp.einsum('bqhmd,bkhmd->bhmqk', qblk, k).astype(jnp.float32) * scale
        p = jax.nn.softmax(s, axis=-1)
        a = p[:, :, 0] - lam * p[:, :, 1]
        return jnp.einsum('bhqk,bkhe->bqhe', a.astype(v.dtype), v)

    out = lax.map(one_block, qb)
    return out.transpose(1, 0, 2, 3, 4).reshape(B, Lq, ATTN_HEADS, V_DIM)


def conformer_conv(c_in, dw, ln_g, ln_b, pw):
    a, b = jnp.split(c_in, 2, axis=-1)
    h = a * jax.nn.sigmoid(b)
    h = lax.conv_general_dilated(h, dw[:, None, :], window_strides=(1,),
                                 padding=[(CONV_K // 2, CONV_K // 2)],
                                 dimension_numbers=('NWC', 'WIO', 'NWC'),
                                 feature_group_count=CONV_WIDTH)
    h = jax.nn.silu(layer_norm(h, ln_g, ln_b))
    return h @ pw


def token_mixer(u, p, lam_init, rope, ctx_kv):
    B, L, _ = u.shape
    proj = u @ p['w_in']
    o1 = POOL_WIDTH
    o2 = o1 + QK_WIDTH
    o3 = o2 + QK_WIDTH
    o4 = o3 + ATTN_WIDTH
    a_in, q, k, v, c_in = jnp.split(proj, [o1, o2, o3, o4], axis=-1)
    y_pool = multiscale_pool(a_in, p['pool_w'], p['pool_scale'])
    q = q.reshape(B, L, ATTN_HEADS, 2, HEAD_DIM)
    k = k.reshape(B, L, ATTN_HEADS, 2, HEAD_DIM)
    v = v.reshape(B, L, ATTN_HEADS, V_DIM)
    if rope is not None:
        cos, sin = rope
        q = apply_rope(q, cos, sin)
        k_att = apply_rope(k, cos, sin)
    else:
        k_att = k
    if ctx_kv is not None:
        k_att = jnp.concatenate([k_att, ctx_kv[0]], axis=1)
        v_att = jnp.concatenate([v, ctx_kv[1]], axis=1)
    else:
        v_att = v
    lp = p['diff_lambda'].astype(jnp.float32)
    lam = jnp.exp(jnp.sum(lp[0] * lp[1])) - jnp.exp(jnp.sum(lp[2] * lp[3])) + lam_init
    o = diff_attention(q, k_att, v_att, lam).astype(jnp.float32)
    o = o * lax.rsqrt(jnp.mean(jnp.square(o), axis=-1, keepdims=True) + LN_EPS) * p['subln_g'] * (1.0 - lam_init)
    y_attn = o.astype(u.dtype).reshape(B, L, ATTN_WIDTH)
    y_conv = conformer_conv(c_in, p['conv_dw'], p['conv_ln_g'], p['conv_ln_b'], p['conv_pw'])
    y = jnp.concatenate([y_pool.astype(u.dtype), y_attn, y_conv.astype(u.dtype)], axis=-1) @ p['w_out']
    return y, k, v


def hier_moe(u, p):
    B, L, D = u.shape
    t = u.reshape(-1, D)
    g_logits = (t @ p['router_grp']).astype(jnp.float32) + p['router_grp_b']
    g_prob = jax.nn.softmax(g_logits, axis=-1)
    g_w, g_idx = lax.top_k(g_prob, 1)
    e_logits = ((t @ p['router_exp']).astype(jnp.float32) + p['router_exp_b']).reshape(-1, N_GROUPS, EXPERTS_PER_GROUP)
    e_sel = jnp.take_along_axis(e_logits, g_idx[:, :, None], axis=1)[:, 0]
    top_v, top_i = lax.top_k(e_sel, TOP_K)
    w = g_w * jax.nn.softmax(top_v, axis=-1)
    ids = g_idx * EXPERTS_PER_GROUP + top_i
    comb = jnp.einsum('nk,nke->ne', w, jax.nn.one_hot(ids, N_EXPERTS, dtype=jnp.float32))
    h = jax.nn.silu(jnp.einsum('nd,edf->nef', t, p['w_gate'])) * jnp.einsum('nd,edf->nef', t, p['w_up'])
    y = jnp.einsum('nef,efd->nd', h * comb[:, :, None].astype(h.dtype), p['w_down'])
    return y.reshape(B, L, D)


def trunk_layer(x, mod, p, lam_init, rope, ctx_kv):
    shift1, scale1, gate1, shift2, scale2, gate2 = jnp.split(mod[:, None, :].astype(x.dtype), 6, axis=-1)
    u = x * (1 + scale1) + shift1
    y, k, v = token_mixer(u, p, lam_init, rope, ctx_kv)
    x = layer_norm(ALPHA * x + gate1 * y, p['ln1_g'], p['ln1_b'])
    u = x * (1 + scale2) + shift2
    x = layer_norm(ALPHA * x + gate2 * hier_moe(u, p), p['ln2_g'], p['ln2_b'])
    return x, k, v


def setup_inputs(seed: int = 0) -> dict:
    key = jax.random.key(seed)
    ks = jax.random.split(key, 32)
    f = jnp.float32
    n = lambda i, shape, s: jax.random.normal(ks[i], shape, f) * s
    return {
        'x_prompt': n(0, (BATCH, SEQ, D_MODEL), 1.0),
        'x_sample': n(1, (DEC_BATCH, DEC_SEQ, D_MODEL), 1.0),
        'cache_k': n(2, (DEC_BATCH, DEPTH, PAST_LEN, ATTN_HEADS, 2, HEAD_DIM), 1.0),
        'cache_v': n(3, (DEC_BATCH, DEPTH, PAST_LEN, ATTN_HEADS, V_DIM), 1.0),
        'c': n(4, (DEC_BATCH, D_MODEL), 1.0),
        'c_ctx': n(5, (D_MODEL,), 1.0),
        'w_mod': n(6, (DEPTH, D_MODEL, 6 * D_MODEL), D_MODEL ** -0.5),
        'b_mod': n(7, (DEPTH, 6 * D_MODEL), 0.02),
        'w_in': n(8, (DEPTH, D_MODEL, IN_WIDTH), D_MODEL ** -0.5),
        'w_out': n(9, (DEPTH, MIX_WIDTH, D_MODEL), BETA * MIX_WIDTH ** -0.5),
        'pool_w': n(10, (DEPTH, POOL_GROUPS, POOL_GC, POOL_GC), POOL_GC ** -0.5),
        'pool_scale': 1.0 + n(11, (DEPTH, POOL_WIDTH), 0.1),
        'diff_lambda': n(12, (DEPTH, 4, HEAD_DIM), 0.1),
        'subln_g': 1.0 + n(13, (DEPTH, V_DIM), 0.1),
        'conv_dw': n(14, (DEPTH, CONV_K, CONV_WIDTH), CONV_K ** -0.5),
        'conv_ln_g': 1.0 + n(15, (DEPTH, CONV_WIDTH), 0.1),
        'conv_ln_b': n(16, (DEPTH, CONV_WIDTH), 0.02),
        'conv_pw': n(17, (DEPTH, CONV_WIDTH, CONV_WIDTH), CONV_WIDTH ** -0.5),
        'ln1_g': 1.0 + n(18, (DEPTH, D_MODEL), 0.1),
        'ln1_b': n(19, (DEPTH, D_MODEL), 0.02),
        'ln2_g': 1.0 + n(20, (DEPTH, D_MODEL), 0.1),
        'ln2_b': n(21, (DEPTH, D_MODEL), 0.02),
        'router_grp': n(22, (DEPTH, D_MODEL, N_GROUPS), D_MODEL ** -0.5),
        'router_grp_b': n(23, (DEPTH, N_GROUPS), 0.01),
        'router_exp': n(24, (DEPTH, D_MODEL, N_EXPERTS), D_MODEL ** -0.5),
        'router_exp_b': n(25, (DEPTH, N_EXPERTS), 0.01),
        'w_gate': n(26, (DEPTH, N_EXPERTS, D_MODEL, EXPERT_FF), D_MODEL ** -0.5),
        'w_up': n(27, (DEPTH, N_EXPERTS, D_MODEL, EXPERT_FF), D_MODEL ** -0.5),
        'w_down': n(28, (DEPTH, N_EXPERTS, EXPERT_FF, D_MODEL), BETA * EXPERT_FF ** -0.5),
    }


def reference(x_prompt, x_sample, cache_k, cache_v, c, c_ctx, w_mod, b_mod, w_in, w_out,
              pool_w, pool_scale, diff_lambda, subln_g, conv_dw, conv_ln_g, conv_ln_b, conv_pw,
              ln1_g, ln1_b, ln2_g, ln2_b, router_grp, router_grp_b, router_exp, router_exp_b,
              w_gate, w_up, w_down):
    rope = rope_tables(x_sample.shape[1])
    silu_ctx = jax.nn.silu(c_ctx)[None]
    silu_c = jax.nn.silu(c)
    xp, xs = x_prompt, x_sample
    new_k, new_v = [], []
    for l in range(DEPTH):
        p = {
            'w_in': w_in[l], 'w_out': w_out[l], 'pool_w': pool_w[l], 'pool_scale': pool_scale[l],
            'diff_lambda': diff_lambda[l], 'subln_g': subln_g[l], 'conv_dw': conv_dw[l],
            'conv_ln_g': conv_ln_g[l], 'conv_ln_b': conv_ln_b[l], 'conv_pw': conv_pw[l],
            'ln1_g': ln1_g[l], 'ln1_b': ln1_b[l], 'ln2_g': ln2_g[l], 'ln2_b': ln2_b[l],
            'router_grp': router_grp[l], 'router_grp_b': router_grp_b[l],
            'router_exp': router_exp[l], 'router_exp_b': router_exp_b[l],
            'w_gate': w_gate[l], 'w_up': w_up[l], 'w_down': w_down[l],
        }
        lam_init = 0.8 - 0.6 * math.exp(-0.3 * l)
        mod_ctx = silu_ctx @ w_mod[l] + b_mod[l]
        xp, k_l, v_l = trunk_layer(xp, mod_ctx, p, lam_init, None, None)
        new_k.append(k_l)
        new_v.append(v_l)
        mod_lat = silu_c @ w_mod[l] + b_mod[l]
        xs, _, _ = trunk_layer(xs, mod_lat, p, lam_init, rope, (cache_k[:, l], cache_v[:, l]))
    new_cache_k = jnp.stack(new_k, axis=1)
    new_cache_v = jnp.stack(new_v, axis=1)
    return (xp, xs, new_cache_k, new_cache_v)
```

```python
import functools
import math

import jax
import jax.numpy as jnp
from jax import lax
from jax.experimental import pallas as pl
from jax.experimental.pallas import tpu as pltpu

F32 = jnp.float32
BF16 = jnp.bfloat16

D_MODEL = 1024
GRID_W = 64
POOL_WIDTH = 256
POOL_WINDOWS = (2, 4, 8, 16)
POOL_GC = 64
ATTN_HEADS = 4
HEAD_DIM = 64
V_DIM = 128
ATTN_WIDTH = 512
QK_WIDTH = 512
CONV_WIDTH = 256
CONV_K = 31
IN_WIDTH = 2304
AXIS_DIM = 32
ROPE_THETA = 10000.0
N_GROUPS = 4
EXPERTS_PER_GROUP = 8
N_EXPERTS = 32
EXPERT_FF = 256
LN_EPS = 1e-5

O_A = 0
O_Q = POOL_WIDTH
O_K = O_Q + QK_WIDTH
O_V = O_K + QK_WIDTH
O_C = O_V + ATTN_WIDTH

HALO = 16
LANES = 128
ROUTER_LANES = 128
NEG_BIG = -1e30
VMEM_LIMIT = 52 * 1024 * 1024


def _silu(x):
    return x * jax.nn.sigmoid(x)


def _layer_norm(x, g, b):
    mu = jnp.mean(x, axis=-1, keepdims=True)
    xc = x - mu
    var = jnp.mean(xc * xc, axis=-1, keepdims=True)
    return xc * lax.rsqrt(var + LN_EPS) * g + b


def _dot(a, b):
    return jnp.dot(a, b, preferred_element_type=F32)


def _params(sem):
    return pltpu.CompilerParams(dimension_semantics=sem, vmem_limit_bytes=VMEM_LIMIT)


def _mod_kernel(c_ref, w_ref, b_ref, o_ref):
    s = _silu(c_ref[...])
    o_ref[0] = _dot(s.astype(BF16), w_ref[0].astype(BF16)) + b_ref[0]


def _mod_call(cvec, w_mod, b_mod):
    depth, d, n6 = w_mod.shape
    rows = cvec.shape[0]
    nb = 1536
    return pl.pallas_call(
        _mod_kernel,
        grid=(depth, n6 // nb),
        in_specs=[
            pl.BlockSpec((rows, d), lambda l, j: (0, 0)),
            pl.BlockSpec((1, d, nb), lambda l, j: (l, 0, j)),
            pl.BlockSpec((1, 1, nb), lambda l, j: (l, 0, j)),
        ],
        out_specs=pl.BlockSpec((1, rows, nb), lambda l, j: (l, 0, j)),
        out_shape=jax.ShapeDtypeStruct((depth, rows, n6), F32),
        compiler_params=_params(("parallel", "parallel")),
        name="mod",
    )(cvec, w_mod, b_mod.reshape(depth, 1, n6))


def _front_kernel(*refs, seq_len, tile, has_rope, want_f32_kv):
    it = iter(refs)
    x_ref, xp_ref, xn_ref, mod_ref, win_ref = next(it), next(it), next(it), next(it), next(it)
    poolw_ref, pscale_ref, dw_ref, cg_ref, cb_ref, pw_ref = (next(it) for _ in range(6))
    if has_rope:
        cos_ref, sin_ref = next(it), next(it)
    q_ref, k_ref, v_ref = next(it), next(it), next(it)
    if want_f32_kv:
        kf_ref, vf_ref = next(it), next(it)
    yp_ref, yc_ref = next(it), next(it)

    i = pl.program_id(1)
    n_ext = tile + 2 * HALO
    shift = mod_ref[0, 0:1, :]
    scale = mod_ref[0, 1:2, :]
    xe = jnp.concatenate([xp_ref[0], x_ref[0], xn_ref[0]], axis=0)
    u = (xe * (1.0 + scale) + shift).astype(BF16)
    proj = _dot(u, win_ref[0])

    row = lax.broadcasted_iota(jnp.int32, (n_ext, 1), 0)
    gpos = i * tile - HALO + row
    valid = (gpos >= 0) & (gpos < seq_len)

    a_ext = jnp.where(valid, proj[:, O_A:O_A + POOL_WIDTH], 0.0)

    def sh(z, s):
        return pltpu.roll(z, s % n_ext, axis=0)

    p2 = a_ext + sh(a_ext, 1)
    p4 = sh(p2, 1) + sh(p2, -1)
    p8 = sh(p4, 2) + sh(p4, -2)
    p16 = sh(p8, 4) + sh(p8, -4)
    t = gpos[HALO:HALO + tile]
    lane = lax.broadcasted_iota(jnp.int32, (1, POOL_WIDTH), 1)
    mean = None
    for g, (w, p) in enumerate(zip(POOL_WINDOWS, (p2, p4, p8, p16))):
        cnt = jnp.minimum(t + w // 2, seq_len) - jnp.maximum(t - w // 2, 0)
        m = p[HALO:HALO + tile] * (1.0 / cnt.astype(F32))
        mean = m if mean is None else jnp.where(lane >= g * POOL_GC, m, mean)
    pooled = mean - a_ext[HALO:HALO + tile]
    y_pool = _dot(pooled.astype(BF16), poolw_ref[0]) * pscale_ref[0]
    yp_ref[0] = y_pool.astype(BF16)

    ca = proj[:, O_C:O_C + CONV_WIDTH]
    cb = proj[:, O_C + CONV_WIDTH:O_C + 2 * CONV_WIDTH]
    h = jnp.where(valid, ca * jax.nn.sigmoid(cb), 0.0)
    acc = None
    for j in range(CONV_K):
        off = HALO - CONV_K // 2 + j
        term = h[off:off + tile] * dw_ref[0, j:j + 1, :]
        acc = term if acc is None else acc + term
    hn = _silu(_layer_norm(acc, cg_ref[0], cb_ref[0]))
    yc_ref[0] = _dot(hn.astype(BF16), pw_ref[0]).astype(BF16)

    qf = proj[HALO:HALO + tile, O_Q:O_Q + QK_WIDTH]
    kf = proj[HALO:HALO + tile, O_K:O_K + QK_WIDTH]
    vf = proj[HALO:HALO + tile, O_V:O_V + ATTN_WIDTH]
    if want_f32_kv:
        kf_ref[0] = kf
        vf_ref[0] = vf
    if has_rope:
        cos = cos_ref[...]
        sin = sin_ref[...]
        l128 = lax.broadcasted_iota(jnp.int32, (1, LANES), 1)
        first_half = (l128 & (AXIS_DIM - 1)) < (AXIS_DIM // 2)

        def rope(z):
            outs = []
            for c in range(QK_WIDTH // LANES):
                zc = z[:, c * LANES:(c + 1) * LANES]
                partner = jnp.where(first_half,
                                    pltpu.roll(zc, LANES - AXIS_DIM // 2, axis=1),
                                    pltpu.roll(zc, AXIS_DIM // 2, axis=1))
                outs.append(zc * cos + partner * sin)
            return jnp.concatenate(outs, axis=1)

        qf = rope(qf)
        kf = rope(kf)
    q_ref[0] = qf.astype(BF16)
    k_ref[0] = kf.astype(BF16)
    v_ref[0] = vf.astype(BF16)


def _front_call(x, mod, layer, w, rope, want_f32_kv, tile):
    b, seq_len, d = x.shape
    nt = seq_len // tile
    hb = tile // HALO
    n_hblk = seq_len // HALO
    mod_b = (lambda bi, i: (bi, 0, 0)) if mod.shape[0] == b else (lambda bi, i: (0, 0, 0))
    lw = lambda bi, i: (layer, 0, 0)
    in_specs = [
        pl.BlockSpec((1, tile, d), lambda bi, i: (bi, i, 0)),
        pl.BlockSpec((1, HALO, d), lambda bi, i: (bi, jnp.maximum(i * hb - 1, 0), 0)),
        pl.BlockSpec((1, HALO, d), lambda bi, i: (bi, jnp.minimum((i + 1) * hb, n_hblk - 1), 0)),
        pl.BlockSpec((1, 6, d), mod_b),
        pl.BlockSpec((1, d, IN_WIDTH), lw),
        pl.BlockSpec((1, POOL_WIDTH, POOL_WIDTH), lw),
        pl.BlockSpec((1, 1, POOL_WIDTH), lw),
        pl.BlockSpec((1, CONV_K, CONV_WIDTH), lw),
        pl.BlockSpec((1, 1, CONV_WIDTH), lw),
        pl.BlockSpec((1, 1, CONV_WIDTH), lw),
        pl.BlockSpec((1, CONV_WIDTH, CONV_WIDTH), lw),
    ]
    args = [x, x, x, mod, w["w_in"], w["pool_bd"], w["pool_scale"], w["conv_dw"], w["conv_ln_g"],
            w["conv_ln_b"], w["conv_pw"]]
    if rope is not None:
        in_specs += [pl.BlockSpec((tile, LANES), lambda bi, i: (i, 0))] * 2
        args += list(rope)
    tok = lambda width: pl.BlockSpec((1, tile, width), lambda bi, i: (bi, i, 0))
    out_specs = [tok(QK_WIDTH), tok(QK_WIDTH), tok(ATTN_WIDTH)]
    out_shape = [jax.ShapeDtypeStruct((b, seq_len, QK_WIDTH), BF16)] * 3
    if want_f32_kv:
        out_specs += [tok(QK_WIDTH), tok(ATTN_WIDTH)]
        out_shape += [jax.ShapeDtypeStruct((b, seq_len, QK_WIDTH), F32)] * 2
    out_specs += [tok(POOL_WIDTH), tok(CONV_WIDTH)]
    out_shape += [jax.ShapeDtypeStruct((b, seq_len, POOL_WIDTH), BF16)] * 2
    kern = functools.partial(_front_kernel, seq_len=seq_len, tile=tile, has_rope=rope is not None,
                             want_f32_kv=want_f32_kv)
    return pl.pallas_call(
        kern, grid=(b, nt), in_specs=in_specs, out_specs=out_specs, out_shape=out_shape,
        compiler_params=_params(("parallel", "parallel")), name="front",
    )(*args)


def _attn_kernel(*refs, lam_init, has_cache):
    if has_cache:
        lam_ref, g_ref, q_ref, k_ref, v_ref, ck_ref, cv_ref, o_ref = refs
    else:
        lam_ref, g_ref, q_ref, k_ref, v_ref, o_ref = refs
    lp = lam_ref[0]
    lam = (jnp.exp(jnp.sum(lp[0:1] * lp[1:2], axis=-1, keepdims=True))
           - jnp.exp(jnp.sum(lp[2:3] * lp[3:4], axis=-1, keepdims=True)) + lam_init)

    q = q_ref[0] * (1.0 / math.sqrt(HEAD_DIM))
    lane = lax.broadcasted_iota(jnp.int32, (1, LANES), 1)
    zero = jnp.zeros_like(q)
    q_maps = (jnp.where(lane < HEAD_DIM, q, zero), jnp.where(lane >= HEAD_DIM, q, zero))
    nt = (((1,), (1,)), ((), ()))
    keys = [k_ref[0]]
    vals = [v_ref[0]]
    if has_cache:
        keys.append(ck_ref[0, 0].astype(BF16))
        vals.append(cv_ref[0, 0].astype(BF16))

    probs = []
    for qm in q_maps:
        s = [lax.dot_general(qm, kk, nt, preferred_element_type=F32) for kk in keys]
        m = functools.reduce(jnp.maximum, [jnp.max(z, axis=-1, keepdims=True) for z in s])
        e = [jnp.exp(z - m) for z in s]
        den = functools.reduce(jnp.add, [jnp.sum(z, axis=-1, keepdims=True) for z in e])
        probs.append((e, 1.0 / den))
    (e1, r1), (e2, r2) = probs
    r2 = lam * r2
    o = None
    for z1, z2, vv in zip(e1, e2, vals):
        a = z1 * r1 - z2 * r2
        part = _dot(a.astype(BF16), vv)
        o = part if o is None else o + part
    o = o * lax.rsqrt(jnp.mean(o * o, axis=-1, keepdims=True) + LN_EPS) * g_ref[0] * (1.0 - lam_init)
    o_ref[0] = o.astype(BF16)


def _attn_call(q, k, v, cache, layer, w, lam_init, tq):
    b, seq_len, _ = q.shape
    lw = lambda bi, h, i: (layer, 0, 0)
    in_specs = [
        pl.BlockSpec((1, 4, HEAD_DIM), lw),
        pl.BlockSpec((1, 1, V_DIM), lw),
        pl.BlockSpec((1, tq, LANES), lambda bi, h, i: (bi, i, h)),
        pl.BlockSpec((1, seq_len, LANES), lambda bi, h, i: (bi, 0, h)),
        pl.BlockSpec((1, seq_len, V_DIM), lambda bi, h, i: (bi, 0, h)),
    ]
    args = [w["diff_lambda"], w["subln_g"], q, k, v]
    if cache is not None:
        past = cache[0].shape[2]
        cspec = pl.BlockSpec((1, 1, past, LANES), lambda bi, h, i: (bi, layer, 0, h))
        in_specs += [cspec, cspec]
        args += list(cache)
    kern = functools.partial(_attn_kernel, lam_init=lam_init, has_cache=cache is not None)
    return pl.pallas_call(
        kern, grid=(b, ATTN_HEADS, seq_len // tq), in_specs=in_specs,
        out_specs=pl.BlockSpec((1, tq, V_DIM), lambda bi, h, i: (bi, i, h)),
        out_shape=jax.ShapeDtypeStruct((b, seq_len, ATTN_WIDTH), BF16),
        compiler_params=_params(("parallel", "parallel", "parallel")), name="attn",
    )(*args)


def _post_kernel(yp_ref, ya_ref, yc_ref, x_ref, mod_ref, wout_ref, g_ref, b_ref, wr_ref, br_ref,
                 x1_ref, u2_ref, comb_ref, *, alpha):
    o1 = POOL_WIDTH
    o2 = o1 + ATTN_WIDTH
    y = (_dot(yp_ref[0], wout_ref[0, 0:o1, :]) + _dot(ya_ref[0], wout_ref[0, o1:o2, :])
         + _dot(yc_ref[0], wout_ref[0, o2:, :]))
    gate1 = mod_ref[0, 2:3, :]
    shift2 = mod_ref[0, 3:4, :]
    scale2 = mod_ref[0, 4:5, :]
    x1 = _layer_norm(alpha * x_ref[0] + gate1 * y, g_ref[0], b_ref[0])
    x1_ref[0] = x1
    u2 = x1 * (1.0 + scale2) + shift2
    u2_ref[0] = u2.astype(BF16)

    logits = jnp.dot(u2, wr_ref[0], preferred_element_type=F32,
                     precision=lax.Precision.HIGHEST) + br_ref[0]
    lane = lax.broadcasted_iota(jnp.int32, (1, ROUTER_LANES), 1)
    lane_f = lane.astype(F32)
    is_g = (lane >= N_EXPERTS) & (lane < N_EXPERTS + N_GROUPS)
    rmax = lambda z: jnp.max(z, axis=-1, keepdims=True)
    rmin = lambda z: jnp.min(z, axis=-1, keepdims=True)
    gl = jnp.where(is_g, logits, NEG_BIG)
    ge = jnp.where(is_g, jnp.exp(gl - rmax(gl)), 0.0)
    gp = ge / jnp.sum(ge, axis=-1, keepdims=True)
    g_w = rmax(gp)
    g_idx = rmin(jnp.where(is_g & (gp == g_w), lane_f - N_EXPERTS, 1e9))
    in_grp = (lane < N_EXPERTS) & (jnp.right_shift(lane, 3).astype(F32) == g_idx)
    el = jnp.where(in_grp, logits, NEG_BIG)
    v1 = rmax(el)
    i1 = rmin(jnp.where(in_grp & (el == v1), lane_f, 1e9))
    rest = in_grp & (lane_f != i1)
    el2 = jnp.where(rest, logits, NEG_BIG)
    v2 = rmax(el2)
    i2 = rmin(jnp.where(rest & (el2 == v2), lane_f, 1e9))
    tt = jnp.exp(v2 - v1)
    w1 = 1.0 / (1.0 + tt)
    w2 = tt / (1.0 + tt)
    comb_ref[0] = (jnp.where(lane_f == i1, g_w * w1, 0.0) + jnp.where(lane_f == i2, g_w * w2, 0.0))


def _post_call(y_pool, y_attn, y_conv, x, mod, layer, w, alpha, tile):
    b, seq_len, d = x.shape
    mod_b = (lambda bi, i: (bi, 0, 0)) if mod.shape[0] == b else (lambda bi, i: (0, 0, 0))
    lw = lambda bi, i: (layer, 0, 0)
    tok = lambda width: pl.BlockSpec((1, tile, width), lambda bi, i: (bi, i, 0))
    in_specs = [tok(POOL_WIDTH), tok(ATTN_WIDTH), tok(CONV_WIDTH), tok(d),
                pl.BlockSpec((1, 6, d), mod_b),
                pl.BlockSpec((1, d, d), lw),
                pl.BlockSpec((1, 1, d), lw), pl.BlockSpec((1, 1, d), lw),
                pl.BlockSpec((1, d, ROUTER_LANES), lw), pl.BlockSpec((1, 1, ROUTER_LANES), lw)]
    return pl.pallas_call(
        functools.partial(_post_kernel, alpha=alpha),
        grid=(b, seq_len // tile), in_specs=in_specs,
        out_specs=[tok(d), tok(d), tok(ROUTER_LANES)],
        out_shape=[jax.ShapeDtypeStruct((b, seq_len, d), F32),
                   jax.ShapeDtypeStruct((b, seq_len, d), BF16),
                   jax.ShapeDtypeStruct((b, seq_len, ROUTER_LANES), F32)],
        compiler_params=_params(("parallel", "parallel")), name="post",
    )(y_pool, y_attn, y_conv, x, mod, w["w_out"], w["ln1_g"], w["ln1_b"], w["router_w"], w["router_b"])


def _moe_kernel(u2_ref, comb_ref, x1_ref, mod_ref, wg_ref, wu_ref, wd_ref, g_ref, b_ref, o_ref, acc_ref,
                *, alpha):
    e = pl.program_id(2)

    @pl.when(e == 0)
    def _():
        acc_ref[...] = jnp.zeros_like(acc_ref)

    u = u2_ref[0]
    hg = _dot(u, wg_ref[0, 0].astype(BF16))
    hu = _dot(u, wu_ref[0, 0].astype(BF16))
    lane = lax.broadcasted_iota(jnp.int32, (1, ROUTER_LANES), 1)
    ce = jnp.sum(jnp.where(lane == e, comb_ref[0], 0.0), axis=-1, keepdims=True)
    h = _silu(hg) * hu * ce
    acc_ref[...] += _dot(h.astype(BF16), wd_ref[0, 0].astype(BF16))

    @pl.when(e == N_EXPERTS - 1)
    def _():
        gate2 = mod_ref[0, 5:6, :]
        o_ref[0] = _layer_norm(alpha * x1_ref[0] + gate2 * acc_ref[...], g_ref[0], b_ref[0])


def _moe_call(u2, comb, x1, mod, layer, w, alpha, tile):
    b, seq_len, d = x1.shape
    mod_b = (lambda bi, i, e: (bi, 0, 0)) if mod.shape[0] == b else (lambda bi, i, e: (0, 0, 0))
    lw = lambda bi, i, e: (layer, 0, 0)
    tok = lambda width: pl.BlockSpec((1, tile, width), lambda bi, i, e: (bi, i, 0))
    in_specs = [tok(d), tok(ROUTER_LANES), tok(d), pl.BlockSpec((1, 6, d), mod_b),
                pl.BlockSpec((1, 1, d, EXPERT_FF), lambda bi, i, e: (layer, e, 0, 0)),
                pl.BlockSpec((1, 1, d, EXPERT_FF), lambda bi, i, e: (layer, e, 0, 0)),
                pl.BlockSpec((1, 1, EXPERT_FF, d), lambda bi, i, e: (layer, e, 0, 0)),
                pl.BlockSpec((1, 1, d), lw), pl.BlockSpec((1, 1, d), lw)]
    return pl.pallas_call(
        functools.partial(_moe_kernel, alpha=alpha),
        grid=(b, seq_len // tile, N_EXPERTS), in_specs=in_specs, out_specs=tok(d),
        out_shape=jax.ShapeDtypeStruct((b, seq_len, d), F32),
        scratch_shapes=[pltpu.VMEM((tile, d), F32)],
        compiler_params=_params(("parallel", "parallel", "arbitrary")), name="moe",
    )(u2, comb, x1, mod, w["w_gate"], w["w_up"], w["w_down"], w["ln2_g"], w["ln2_b"])


def _rope_tables(seq_len):
    rows = seq_len // GRID_W
    row_id = jnp.repeat(jnp.arange(rows), GRID_W).astype(F32)
    col_id = jnp.tile(jnp.arange(GRID_W), rows).astype(F32)
    inv = ROPE_THETA ** (-jnp.arange(0, AXIS_DIM, 2, dtype=F32) / AXIS_DIM)
    ang = jnp.stack([row_id[:, None] * inv, col_id[:, None] * inv], axis=1)
    cos = jnp.cos(ang)[:, :, None, :]
    sin = jnp.sin(ang)[:, :, None, :]
    shape = (seq_len, 2, 2, AXIS_DIM // 2)
    cos = jnp.broadcast_to(cos, shape).reshape(seq_len, HEAD_DIM)
    sin = (jnp.broadcast_to(sin, shape) * jnp.array([-1.0, 1.0], F32)[None, None, :, None]).reshape(seq_len, HEAD_DIM)
    return jnp.tile(cos, (1, LANES // HEAD_DIM)), jnp.tile(sin, (1, LANES // HEAD_DIM))


def _trunk_layer(x, mod, layer, w, lam_init, alpha, rope, cache, want_f32_kv, tiles):
    outs = _front_call(x, mod, layer, w, rope, want_f32_kv, tiles["front"])
    if want_f32_kv:
        q, k, v, kf, vf, y_pool, y_conv = outs
    else:
        q, k, v, y_pool, y_conv = outs
        kf = vf = None
    y_attn = _attn_call(q, k, v, cache, layer, w, lam_init, tiles["attn"])
    x1, u2, comb = _post_call(y_pool, y_attn, y_conv, x, mod, layer, w, alpha, tiles["post"])
    x2 = _moe_call(u2, comb, x1, mod, layer, w, alpha, tiles["moe"])
    return x2, kf, vf


def kernel(x_prompt, x_sample, cache_k, cache_v, c, c_ctx, w_mod, b_mod, w_in, w_out, pool_w, pool_scale, diff_lambda, subln_g, conv_dw, conv_ln_g, conv_ln_b, conv_pw, ln1_g, ln1_b, ln2_g, ln2_b, router_grp, router_grp_b, router_exp, router_exp_b, w_gate, w_up, w_down):
    depth = w_in.shape[0]
    batch, seq, d = x_prompt.shape
    dec_batch, dec_seq, _ = x_sample.shape
    past = cache_k.shape[2]
    alpha = (2 * depth) ** 0.25

    mod_rows = 16
    cvec = jnp.concatenate([c, c_ctx[None], jnp.zeros((mod_rows - dec_batch - 1, d), F32)], axis=0)
    mod_all = _mod_call(cvec, w_mod, b_mod)

    eye = jnp.eye(len(POOL_WINDOWS), dtype=F32)
    pool_bd = (pool_w[:, :, :, None, :] * eye[None, :, None, :, None]).reshape(depth, POOL_WIDTH, POOL_WIDTH)
    router_w = jnp.concatenate(
        [router_exp, router_grp, jnp.zeros((depth, d, ROUTER_LANES - N_EXPERTS - N_GROUPS), F32)], axis=-1)
    router_b = jnp.concatenate(
        [router_exp_b, router_grp_b, jnp.zeros((depth, ROUTER_LANES - N_EXPERTS - N_GROUPS), F32)], axis=-1)
    r3 = lambda z: z.reshape(depth, 1, z.shape[-1])
    w = {
        "w_in": w_in.astype(BF16), "w_out": w_out.astype(BF16), "pool_bd": pool_bd.astype(BF16),
        "pool_scale": r3(pool_scale), "diff_lambda": diff_lambda, "subln_g": r3(subln_g),
        "conv_dw": conv_dw, "conv_ln_g": r3(conv_ln_g), "conv_ln_b": r3(conv_ln_b),
        "conv_pw": conv_pw.astype(BF16), "ln1_g": r3(ln1_g), "ln1_b": r3(ln1_b), "ln2_g": r3(ln2_g),
        "ln2_b": r3(ln2_b), "router_w": router_w, "router_b": r3(router_b),
        "w_gate": w_gate, "w_up": w_up, "w_down": w_down,
    }
    rope = _rope_tables(dec_seq)
    cache = (cache_k.reshape(dec_batch, depth, past, QK_WIDTH), cache_v.reshape(dec_batch, depth, past, ATTN_WIDTH))
    ctx_tiles = {"front": seq, "attn": seq, "post": seq, "moe": seq}
    lat_tiles = {"front": 512, "attn": 256, "post": 512, "moe": 1024}

    xp, xs = x_prompt, x_sample
    new_k, new_v = [], []
    for l in range(depth):
        lam_init = 0.8 - 0.6 * math.exp(-0.3 * l)
        mod_ctx = mod_all[l, dec_batch:dec_batch + 1].reshape(1, 6, d)
        mod_lat = mod_all[l, :dec_batch].reshape(dec_batch, 6, d)
        xp, kf, vf = _trunk_layer(xp, mod_ctx, l, w, lam_init, alpha, None, None, True, ctx_tiles)
        new_k.append(kf.reshape(batch, seq, ATTN_HEADS, 2, HEAD_DIM))
        new_v.append(vf.reshape(batch, seq, ATTN_HEADS, V_DIM))
        xs, _, _ = _trunk_layer(xs, mod_lat, l, w, lam_init, alpha, rope, cache, False, lat_tiles)
    return (xp, xs, jnp.stack(new_k, axis=1), jnp.stack(new_v, axis=1))
```

```python
import functools
import math

import jax
import jax.numpy as jnp
from jax import lax
from jax.experimental import pallas as pl
from jax.experimental.pallas import tpu as pltpu

F32 = jnp.float32
BF16 = jnp.bfloat16

D_MODEL = 1024
GRID_W = 64
POOL_WIDTH = 256
POOL_WINDOWS = (2, 4, 8, 16)
POOL_GC = 64
ATTN_HEADS = 4
HEAD_DIM = 64
V_DIM = 128
ATTN_WIDTH = 512
QK_WIDTH = 512
CONV_WIDTH = 256
CONV_K = 31
IN_WIDTH = 2304
AXIS_DIM = 32
ROPE_THETA = 10000.0
N_GROUPS = 4
EXPERTS_PER_GROUP = 8
N_EXPERTS = 32
EXPERT_FF = 256
LN_EPS = 1e-5

O_A = 0
O_Q = POOL_WIDTH
O_K = O_Q + QK_WIDTH
O_V = O_K + QK_WIDTH
O_C = O_V + ATTN_WIDTH

TOP_K = 2
HALO = 16
LANES = 128
SUBLANES = 8
ROW_CHUNKS = D_MODEL // LANES
ROUTER_LANES = 128
RANK_SPAN = 4096
EXPERT_BLOCK = 128
NEG_BIG = -1e30
VMEM_LIMIT = 56 * 1024 * 1024


def _silu(x):
    return x * jax.nn.sigmoid(x)


def _layer_norm(x, g, b):
    mu = jnp.mean(x, axis=-1, keepdims=True)
    xc = x - mu
    var = jnp.mean(xc * xc, axis=-1, keepdims=True)
    return xc * lax.rsqrt(var + LN_EPS) * g + b


def _dot(a, b):
    return jnp.dot(a, b, preferred_element_type=F32)


def _params(sem):
    return pltpu.CompilerParams(dimension_semantics=sem, vmem_limit_bytes=VMEM_LIMIT)


def _mod_kernel(c_ref, w_ref, b_ref, o_ref):
    s = _silu(c_ref[...])
    o_ref[0] = _dot(s.astype(BF16), w_ref[0].astype(BF16)) + b_ref[0]


def _mod_call(cvec, w_mod, b_mod):
    depth, d, n6 = w_mod.shape
    rows = cvec.shape[0]
    nb = 1536
    return pl.pallas_call(
        _mod_kernel,
        grid=(depth, n6 // nb),
        in_specs=[
            pl.BlockSpec((rows, d), lambda l, j: (0, 0)),
            pl.BlockSpec((1, d, nb), lambda l, j: (l, 0, j)),
            pl.BlockSpec((1, 1, nb), lambda l, j: (l, 0, j)),
        ],
        out_specs=pl.BlockSpec((1, rows, nb), lambda l, j: (l, 0, j)),
        out_shape=jax.ShapeDtypeStruct((depth, rows, n6), F32),
        compiler_params=_params(("parallel", "parallel")),
        name="mod",
    )(cvec, w_mod, b_mod.reshape(depth, 1, n6))


def _front_kernel(*refs, seq_len, tile, has_rope, want_f32_kv):
    it = iter(refs)
    x_ref, xp_ref, xn_ref, mod_ref, win_ref = next(it), next(it), next(it), next(it), next(it)
    poolw_ref, pscale_ref, dw_ref, cg_ref, cb_ref, pw_ref = (next(it) for _ in range(6))
    if has_rope:
        cos_ref, sin_ref = next(it), next(it)
    q_ref, k_ref, v_ref = next(it), next(it), next(it)
    if want_f32_kv:
        kf_ref, vf_ref = next(it), next(it)
    yp_ref, yc_ref = next(it), next(it)

    i = pl.program_id(1)
    n_ext = tile + 2 * HALO
    shift = mod_ref[0, 0:1, :]
    scale = mod_ref[0, 1:2, :]
    xe = jnp.concatenate([xp_ref[0], x_ref[0], xn_ref[0]], axis=0)
    u = (xe * (1.0 + scale) + shift).astype(BF16)
    proj = _dot(u, win_ref[0])

    row = lax.broadcasted_iota(jnp.int32, (n_ext, 1), 0)
    gpos = i * tile - HALO + row
    valid = (gpos >= 0) & (gpos < seq_len)

    a_ext = jnp.where(valid, proj[:, O_A:O_A + POOL_WIDTH], 0.0)

    def sh(z, s):
        return pltpu.roll(z, s % n_ext, axis=0)

    p2 = a_ext + sh(a_ext, 1)
    p4 = sh(p2, 1) + sh(p2, -1)
    p8 = sh(p4, 2) + sh(p4, -2)
    p16 = sh(p8, 4) + sh(p8, -4)
    t = gpos[HALO:HALO + tile]
    lane = lax.broadcasted_iota(jnp.int32, (1, POOL_WIDTH), 1)
    mean = None
    for g, (w, p) in enumerate(zip(POOL_WINDOWS, (p2, p4, p8, p16))):
        cnt = jnp.minimum(t + w // 2, seq_len) - jnp.maximum(t - w // 2, 0)
        m = p[HALO:HALO + tile] * (1.0 / cnt.astype(F32))
        mean = m if mean is None else jnp.where(lane >= g * POOL_GC, m, mean)
    pooled = mean - a_ext[HALO:HALO + tile]
    y_pool = _dot(pooled.astype(BF16), poolw_ref[0]) * pscale_ref[0]
    yp_ref[0] = y_pool.astype(BF16)

    ca = proj[:, O_C:O_C + CONV_WIDTH]
    cb = proj[:, O_C + CONV_WIDTH:O_C + 2 * CONV_WIDTH]
    h = jnp.where(valid, ca * jax.nn.sigmoid(cb), 0.0)
    acc = None
    for j in range(CONV_K):
        off = HALO - CONV_K // 2 + j
        term = h[off:off + tile] * dw_ref[0, j:j + 1, :]
        acc = term if acc is None else acc + term
    hn = _silu(_layer_norm(acc, cg_ref[0], cb_ref[0]))
    yc_ref[0] = _dot(hn.astype(BF16), pw_ref[0]).astype(BF16)

    qf = proj[HALO:HALO + tile, O_Q:O_Q + QK_WIDTH]
    kf = proj[HALO:HALO + tile, O_K:O_K + QK_WIDTH]
    vf = proj[HALO:HALO + tile, O_V:O_V + ATTN_WIDTH]
    if want_f32_kv:
        kf_ref[0] = kf
        vf_ref[0] = vf
    if has_rope:
        cos = cos_ref[...]
        sin = sin_ref[...]
        l128 = lax.broadcasted_iota(jnp.int32, (1, LANES), 1)
        first_half = (l128 & (AXIS_DIM - 1)) < (AXIS_DIM // 2)

        def rope(z):
            outs = []
            for c in range(QK_WIDTH // LANES):
                zc = z[:, c * LANES:(c + 1) * LANES]
                partner = jnp.where(first_half,
                                    pltpu.roll(zc, LANES - AXIS_DIM // 2, axis=1),
                                    pltpu.roll(zc, AXIS_DIM // 2, axis=1))
                outs.append(zc * cos + partner * sin)
            return jnp.concatenate(outs, axis=1)

        qf = rope(qf)
        kf = rope(kf)
    q_ref[0] = qf.astype(BF16)
    k_ref[0] = kf.astype(BF16)
    v_ref[0] = vf.astype(BF16)


def _front_call(x, mod, layer, w, rope, want_f32_kv, tile):
    b, seq_len, d = x.shape
    nt = seq_len // tile
    hb = tile // HALO
    n_hblk = seq_len // HALO
    mod_b = (lambda bi, i: (bi, 0, 0)) if mod.shape[0] == b else (lambda bi, i: (0, 0, 0))
    lw = lambda bi, i: (layer, 0, 0)
    in_specs = [
        pl.BlockSpec((1, tile, d), lambda bi, i: (bi, i, 0)),
        pl.BlockSpec((1, HALO, d), lambda bi, i: (bi, jnp.maximum(i * hb - 1, 0), 0)),
        pl.BlockSpec((1, HALO, d), lambda bi, i: (bi, jnp.minimum((i + 1) * hb, n_hblk - 1), 0)),
        pl.BlockSpec((1, 6, d), mod_b),
        pl.BlockSpec((1, d, IN_WIDTH), lw),
        pl.BlockSpec((1, POOL_WIDTH, POOL_WIDTH), lw),
        pl.BlockSpec((1, 1, POOL_WIDTH), lw),
        pl.BlockSpec((1, CONV_K, CONV_WIDTH), lw),
        pl.BlockSpec((1, 1, CONV_WIDTH), lw),
        pl.BlockSpec((1, 1, CONV_WIDTH), lw),
        pl.BlockSpec((1, CONV_WIDTH, CONV_WIDTH), lw),
    ]
    args = [x, x, x, mod, w["w_in"], w["pool_bd"], w["pool_scale"], w["conv_dw"], w["conv_ln_g"],
            w["conv_ln_b"], w["conv_pw"]]
    if rope is not None:
        in_specs += [pl.BlockSpec((tile, LANES), lambda bi, i: (i, 0))] * 2
        args += list(rope)
    tok = lambda width: pl.BlockSpec((1, tile, width), lambda bi, i: (bi, i, 0))
    out_specs = [tok(QK_WIDTH), tok(QK_WIDTH), tok(ATTN_WIDTH)]
    out_shape = [jax.ShapeDtypeStruct((b, seq_len, QK_WIDTH), BF16)] * 3
    if want_f32_kv:
        out_specs += [tok(QK_WIDTH), tok(ATTN_WIDTH)]
        out_shape += [jax.ShapeDtypeStruct((b, seq_len, QK_WIDTH), F32)] * 2
    out_specs += [tok(POOL_WIDTH), tok(CONV_WIDTH)]
    out_shape += [jax.ShapeDtypeStruct((b, seq_len, POOL_WIDTH), BF16)] * 2
    kern = functools.partial(_front_kernel, seq_len=seq_len, tile=tile, has_rope=rope is not None,
                             want_f32_kv=want_f32_kv)
    return pl.pallas_call(
        kern, grid=(b, nt), in_specs=in_specs, out_specs=out_specs, out_shape=out_shape,
        compiler_params=_params(("parallel", "parallel")), name="front",
    )(*args)


def _attn_kernel(*refs, lam_init, has_cache):
    if has_cache:
        lam_ref, g_ref, q_ref, k_ref, v_ref, ck_ref, cv_ref, o_ref = refs
    else:
        lam_ref, g_ref, q_ref, k_ref, v_ref, o_ref = refs
    lp = lam_ref[0]
    lam = (jnp.exp(jnp.sum(lp[0:1] * lp[1:2], axis=-1, keepdims=True))
           - jnp.exp(jnp.sum(lp[2:3] * lp[3:4], axis=-1, keepdims=True)) + lam_init)

    q = q_ref[0] * (1.0 / math.sqrt(HEAD_DIM))
    lane = lax.broadcasted_iota(jnp.int32, (1, LANES), 1)
    zero = jnp.zeros_like(q)
    q_maps = (jnp.where(lane < HEAD_DIM, q, zero), jnp.where(lane >= HEAD_DIM, q, zero))
    nt = (((1,), (1,)), ((), ()))
    keys = [k_ref[0]]
    vals = [v_ref[0]]
    if has_cache:
        keys.append(ck_ref[0, 0].astype(BF16))
        vals.append(cv_ref[0, 0].astype(BF16))

    probs = []
    for qm in q_maps:
        s = [lax.dot_general(qm, kk, nt, preferred_element_type=F32) for kk in keys]
        m = functools.reduce(jnp.maximum, [jnp.max(z, axis=-1, keepdims=True) for z in s])
        e = [jnp.exp(z - m) for z in s]
        den = functools.reduce(jnp.add, [jnp.sum(z, axis=-1, keepdims=True) for z in e])
        probs.append((e, 1.0 / den))
    (e1, r1), (e2, r2) = probs
    r2 = lam * r2
    o = None
    for z1, z2, vv in zip(e1, e2, vals):
        a = z1 * r1 - z2 * r2
        part = _dot(a.astype(BF16), vv)
        o = part if o is None else o + part
    o = o * lax.rsqrt(jnp.mean(o * o, axis=-1, keepdims=True) + LN_EPS) * g_ref[0] * (1.0 - lam_init)
    o_ref[0] = o.astype(BF16)


def _attn_call(q, k, v, cache, layer, w, lam_init, tq):
    b, seq_len, _ = q.shape
    lw = lambda bi, h, i: (layer, 0, 0)
    in_specs = [
        pl.BlockSpec((1, 4, HEAD_DIM), lw),
        pl.BlockSpec((1, 1, V_DIM), lw),
        pl.BlockSpec((1, tq, LANES), lambda bi, h, i: (bi, i, h)),
        pl.BlockSpec((1, seq_len, LANES), lambda bi, h, i: (bi, 0, h)),
        pl.BlockSpec((1, seq_len, V_DIM), lambda bi, h, i: (bi, 0, h)),
    ]
    args = [w["diff_lambda"], w["subln_g"], q, k, v]
    if cache is not None:
        past = cache[0].shape[2]
        cspec = pl.BlockSpec((1, 1, past, LANES), lambda bi, h, i: (bi, layer, 0, h))
        in_specs += [cspec, cspec]
        args += list(cache)
    kern = functools.partial(_attn_kernel, lam_init=lam_init, has_cache=cache is not None)
    return pl.pallas_call(
        kern, grid=(b, ATTN_HEADS, seq_len // tq), in_specs=in_specs,
        out_specs=pl.BlockSpec((1, tq, V_DIM), lambda bi, h, i: (bi, i, h)),
        out_shape=jax.ShapeDtypeStruct((b, seq_len, ATTN_WIDTH), BF16),
        compiler_params=_params(("parallel", "parallel", "parallel")), name="attn",
    )(*args)


def _post_kernel(yp_ref, ya_ref, yc_ref, x_ref, mod_ref, wout_ref, g_ref, b_ref, wr_ref, br_ref,
                 x1_ref, u2c_ref, ri_ref, rw_ref, cnt_ref, carry_ref, *, alpha, subs_per_tile):
    sub = pl.program_id(0) * pl.num_programs(1) + pl.program_id(1)

    @pl.when(sub % subs_per_tile == 0)
    def _():
        carry_ref[...] = jnp.zeros_like(carry_ref)

    o1 = POOL_WIDTH
    o2 = o1 + ATTN_WIDTH
    y = (_dot(yp_ref[0], wout_ref[0, 0:o1, :]) + _dot(ya_ref[0], wout_ref[0, o1:o2, :])
         + _dot(yc_ref[0], wout_ref[0, o2:, :]))
    gate1 = mod_ref[0, 2:3, :]
    shift2 = mod_ref[0, 3:4, :]
    scale2 = mod_ref[0, 4:5, :]
    x1 = _layer_norm(alpha * x_ref[0] + gate1 * y, g_ref[0], b_ref[0])
    x1_ref[0] = x1
    u2 = x1 * (1.0 + scale2) + shift2
    tile = u2.shape[0]
    for c in range(ROW_CHUNKS):
        u2c_ref[:, c, :, :] = u2[:, c * LANES:(c + 1) * LANES].reshape(tile // SUBLANES, SUBLANES, LANES)

    logits =jnp.dot(u2, wr_ref[0], preferred_element_type=F32,
                     precision=lax.Precision.HIGHEST) + br_ref[0]
    lane = lax.broadcasted_iota(jnp.int32, (1, ROUTER_LANES), 1)
    lane_f = lane.astype(F32)
    is_g = (lane >= N_EXPERTS) & (lane < N_EXPERTS + N_GROUPS)
    rmax = lambda z: jnp.max(z, axis=-1, keepdims=True)
    rmin = lambda z: jnp.min(z, axis=-1, keepdims=True)
    gl = jnp.where(is_g, logits, NEG_BIG)
    ge = jnp.where(is_g, jnp.exp(gl - rmax(gl)), 0.0)
    gp = ge / jnp.sum(ge, axis=-1, keepdims=True)
    g_w = rmax(gp)
    g_idx = rmin(jnp.where(is_g & (gp == g_w), lane_f - N_EXPERTS, 1e9))
    in_grp = (lane < N_EXPERTS) & (jnp.right_shift(lane, 3).astype(F32) == g_idx)
    el = jnp.where(in_grp, logits, NEG_BIG)
    v1 = rmax(el)
    i1 = rmin(jnp.where(in_grp & (el == v1), lane_f, 1e9))
    rest = in_grp & (lane_f != i1)
    el2 = jnp.where(rest, logits, NEG_BIG)
    v2 = rmax(el2)
    i2 = rmin(jnp.where(rest & (el2 == v2), lane_f, 1e9))
    tt = jnp.exp(v2 - v1)
    w1 = 1.0 / (1.0 + tt)
    w2 = tt / (1.0 + tt)

    oh1 = lane_f == i1
    oh2 = lane_f == i2
    member = jnp.where(oh1 | oh2, 1.0, 0.0)
    r_io = lax.broadcasted_iota(jnp.int32, (tile, tile), 0)
    c_io = lax.broadcasted_iota(jnp.int32, (tile, tile), 1)
    before = jnp.where(r_io > c_io, 1.0, 0.0).astype(BF16)
    prior = _dot(before, member.astype(BF16)) + carry_ref[0:1, :]
    rank1 = jnp.sum(jnp.where(oh1, prior, 0.0), axis=-1, keepdims=True)
    rank2 = jnp.sum(jnp.where(oh2, prior, 0.0), axis=-1, keepdims=True)
    total = carry_ref[0:1, :] + jnp.sum(member, axis=0, keepdims=True)
    carry_ref[...] = jnp.broadcast_to(total, carry_ref.shape)
    cnt_ref[0] = jnp.broadcast_to(total, carry_ref.shape).astype(jnp.int32)

    cols = (i1 * RANK_SPAN + rank1, i2 * RANK_SPAN + rank2, g_w * w1, g_w * w2)
    z = jnp.zeros((tile, ROUTER_LANES), F32)
    for j, col in enumerate(cols):
        z = jnp.where(lane == j, col, z)
    zt = z.T
    ri_ref[0] = zt[0:2, :].astype(jnp.int32)
    rw_ref[0] = zt[2:4, :]


def _post_call(y_pool, y_attn, y_conv, x, mod, layer, w, alpha, tile, moe_tile):
    b, seq_len, d = x.shape
    nt = seq_len // tile
    n_sub = b * nt
    subs_per_tile = moe_tile // tile
    mod_b = (lambda bi, i: (bi, 0, 0)) if mod.shape[0] == b else (lambda bi, i: (0, 0, 0))
    lw = lambda bi, i: (layer, 0, 0)
    tok = lambda width: pl.BlockSpec((1, tile, width), lambda bi, i: (bi, i, 0))
    sub = lambda bi, i: (bi * nt + i, 0, 0)
    in_specs = [tok(POOL_WIDTH), tok(ATTN_WIDTH), tok(CONV_WIDTH), tok(d),
                pl.BlockSpec((1, 6, d), mod_b),
                pl.BlockSpec((1, d, d), lw),
                pl.BlockSpec((1, 1, d), lw), pl.BlockSpec((1, 1, d), lw),
                pl.BlockSpec((1, d, ROUTER_LANES), lw), pl.BlockSpec((1, 1, ROUTER_LANES), lw)]
    out_specs = [tok(d),
                 pl.BlockSpec((tile // SUBLANES, ROW_CHUNKS, SUBLANES, LANES), lambda bi, i: (bi * nt + i, 0, 0, 0)),
                 pl.BlockSpec((1, 2, tile), sub), pl.BlockSpec((1, 2, tile), sub),
                 pl.BlockSpec((1, SUBLANES, ROUTER_LANES), lambda bi, i: ((bi * nt + i) // subs_per_tile, 0, 0))]
    out_shape = [jax.ShapeDtypeStruct((b, seq_len, d), F32),
                 jax.ShapeDtypeStruct((b * seq_len // SUBLANES, ROW_CHUNKS, SUBLANES, LANES), F32),
                 jax.ShapeDtypeStruct((n_sub, 2, tile), jnp.int32),
                 jax.ShapeDtypeStruct((n_sub, 2, tile), F32),
                 jax.ShapeDtypeStruct((n_sub // subs_per_tile, SUBLANES, ROUTER_LANES), jnp.int32)]
    return pl.pallas_call(
        functools.partial(_post_kernel, alpha=alpha, subs_per_tile=subs_per_tile),
        grid=(b, nt), in_specs=in_specs, out_specs=out_specs, out_shape=out_shape,
        scratch_shapes=[pltpu.VMEM((SUBLANES, ROUTER_LANES), F32)],
        compiler_params=_params(("arbitrary", "arbitrary")), name="post",
    )(y_pool, y_attn, y_conv, x, mod, w["w_out"], w["ln1_g"], w["ln1_b"], w["router_w"], w["router_b"])


def _row_slice(row):
    start = jnp.right_shift(row, 3) * (SUBLANES * ROW_CHUNKS) + jnp.bitwise_and(row, SUBLANES - 1)
    return pl.ds(start, ROW_CHUNKS, stride=SUBLANES)


def _moe_kernel(ri_ref, rw_ref, cnt_ref, u2c_ref, wg_ref, wu_ref, wd_ref, o_ref, sorted_ref, wgb_ref, wub_ref,
                wdb_ref, off_ref, pos_ref, *, tile, sub_tile):
    e = pl.program_id(1)
    groups = tile // SUBLANES
    sub_shift = sub_tile.bit_length() - 1
    rank_shift = RANK_SPAN.bit_length() - 1

    def route(n, k):
        return jnp.right_shift(n, sub_shift), k, jnp.bitwise_and(n, sub_tile - 1)

    @pl.when((pl.program_id(0) == 0) & (e == 0))
    def _():
        sorted_ref[...] = jnp.zeros_like(sorted_ref)

    @pl.when(e == 0)
    def _():
        def off_body(j, acc):
            off_ref[j] = acc
            return acc + ((cnt_ref[0, 0, j] + (SUBLANES - 1)) // SUBLANES) * SUBLANES
        lax.fori_loop(0, N_EXPERTS, off_body, 0)

        def disp_body(a, carry):
            for s in range(SUBLANES):
                n = a * SUBLANES + s
                row = u2c_ref[pl.ds(a * (SUBLANES * ROW_CHUNKS) + s, ROW_CHUNKS, stride=SUBLANES), :]
                for k in range(TOP_K):
                    code = ri_ref[route(n, k)]
                    p = off_ref[jnp.right_shift(code, rank_shift)] + jnp.bitwise_and(code, RANK_SPAN - 1)
                    pos_ref[k, n] = p
                    sorted_ref[_row_slice(p), :] = row
            return carry
        lax.fori_loop(0, groups, disp_body, 0)

    cnt = cnt_ref[0, 0, e]
    base = off_ref[e]
    wgb_ref[...] = wg_ref[0, 0].astype(BF16)
    wub_ref[...] = wu_ref[0, 0].astype(BF16)
    wdb_ref[...] = wd_ref[0, 0].astype(BF16)
    blk_groups = EXPERT_BLOCK // SUBLANES
    row_id = lax.broadcasted_iota(jnp.int32, (SUBLANES, 1), 0)

    def tile_at(start, g, c):
        return pl.ds(pl.multiple_of(start + (g * ROW_CHUNKS + c) * SUBLANES, SUBLANES), SUBLANES)

    def blk_body(bi, carry):
        start = (base + bi * EXPERT_BLOCK) * ROW_CHUNKS
        x = jnp.concatenate(
            [jnp.concatenate([sorted_ref[tile_at(start, g, c), :] for g in range(blk_groups)], axis=0)
             for c in range(ROW_CHUNKS)], axis=1).astype(BF16)
        h = _silu(_dot(x, wgb_ref[...])) * _dot(x, wub_ref[...])
        y = _dot(h.astype(BF16), wdb_ref[...])
        n_valid = cnt - bi * EXPERT_BLOCK
        for g in range(blk_groups):
            keep = (row_id + g * SUBLANES) < n_valid
            for c in range(ROW_CHUNKS):
                sl = tile_at(start, g, c)
                piece = y[g * SUBLANES:(g + 1) * SUBLANES, c * LANES:(c + 1) * LANES]
                sorted_ref[sl, :] = jnp.where(keep, piece, sorted_ref[sl, :])
        return carry
    lax.fori_loop(0, (cnt + EXPERT_BLOCK - 1) // EXPERT_BLOCK, blk_body, 0)

    @pl.when(e == N_EXPERTS - 1)
    def _():
        def comb_body(a, carry):
            for s in range(SUBLANES):
                n = a * SUBLANES + s
                acc = None
                for k in range(TOP_K):
                    term = sorted_ref[_row_slice(pos_ref[k, n]), :] * rw_ref[route(n, k)]
                    acc = term if acc is None else acc + term
                o_ref[pl.ds(a * (SUBLANES * ROW_CHUNKS) + s, ROW_CHUNKS, stride=SUBLANES), :] = acc
            return carry
        lax.fori_loop(0, groups, comb_body, 0)


def _moe_call(u2c, ri, rw, cnt, layer, w, tile, sub_tile):
    n_tok = u2c.shape[0] // ROW_CHUNKS
    n_tiles = n_tok // tile
    spt = tile // sub_tile
    d = D_MODEL
    sorted_rows = tile * TOP_K + N_EXPERTS * (SUBLANES - 1) + EXPERT_BLOCK
    sorted_rows = -(-sorted_rows // SUBLANES) * SUBLANES
    smem = lambda shape, imap: pl.BlockSpec(shape, imap, memory_space=pltpu.SMEM)
    in_specs = [smem((spt, 2, sub_tile), lambda t, e: (t, 0, 0)),
                smem((spt, 2, sub_tile), lambda t, e: (t, 0, 0)),
                smem((1, SUBLANES, ROUTER_LANES), lambda t, e: (t, 0, 0)),
                pl.BlockSpec((tile * ROW_CHUNKS, LANES), lambda t, e: (t, 0), pipeline_mode=pl.Buffered(1)),
                pl.BlockSpec((1, 1, d, EXPERT_FF), lambda t, e: (layer, e, 0, 0)),
                pl.BlockSpec((1, 1, d, EXPERT_FF), lambda t, e: (layer, e, 0, 0)),
                pl.BlockSpec((1, 1, EXPERT_FF, d), lambda t, e: (layer, e, 0, 0))]
    return pl.pallas_call(
        functools.partial(_moe_kernel, tile=tile, sub_tile=sub_tile),
        grid=(n_tiles, N_EXPERTS), in_specs=in_specs,
        out_specs=pl.BlockSpec((tile * ROW_CHUNKS, LANES), lambda t, e: (t, 0)),
        out_shape=jax.ShapeDtypeStruct((n_tok * ROW_CHUNKS, LANES), F32),
        scratch_shapes=[pltpu.VMEM((sorted_rows * ROW_CHUNKS, LANES), F32),
                        pltpu.VMEM((d, EXPERT_FF), BF16), pltpu.VMEM((d, EXPERT_FF), BF16),
                        pltpu.VMEM((EXPERT_FF, d), BF16),
                        pltpu.SMEM((N_EXPERTS,), jnp.int32),
                        pltpu.SMEM((TOP_K, tile), jnp.int32)],
        compiler_params=_params(("arbitrary", "arbitrary")), name="moe",
    )(ri, rw, cnt, u2c, w["w_gate"], w["w_up"], w["w_down"])


def _ln2_kernel(m_ref, x1_ref, mod_ref, g_ref, b_ref, o_ref, *, alpha):
    tile = x1_ref.shape[0]
    moe = jnp.concatenate([m_ref[:, c, :, :].reshape(tile, LANES) for c in range(ROW_CHUNKS)], axis=1)
    gate2 = mod_ref[0, 5:6, :]
    o_ref[...] = _layer_norm(alpha * x1_ref[...] + gate2 * moe, g_ref[0], b_ref[0])


def _ln2_call(moe_c, x1, mod, layer, w, alpha, tile):
    b, seq_len, d = x1.shape
    n_tok = b * seq_len
    per_seq = seq_len // tile
    mod_b = (lambda t: (t // per_seq, 0, 0)) if mod.shape[0] == b else (lambda t: (0, 0, 0))
    lw = lambda t: (layer, 0, 0)
    out = pl.pallas_call(
        functools.partial(_ln2_kernel, alpha=alpha),
        grid=(n_tok // tile,),
        in_specs=[pl.BlockSpec((tile // SUBLANES, ROW_CHUNKS, SUBLANES, LANES), lambda t: (t, 0, 0, 0)),
                  pl.BlockSpec((tile, d), lambda t: (t, 0)),
                  pl.BlockSpec((1, 6, d), mod_b),
                  pl.BlockSpec((1, 1, d), lw), pl.BlockSpec((1, 1, d), lw)],
        out_specs=pl.BlockSpec((tile, d), lambda t: (t, 0)),
        out_shape=jax.ShapeDtypeStruct((n_tok, d), F32),
        compiler_params=_params(("parallel",)), name="ln2",
    )(moe_c.reshape(n_tok // SUBLANES, ROW_CHUNKS, SUBLANES, LANES), x1.reshape(n_tok, d), mod,
      w["ln2_g"], w["ln2_b"])
    return out.reshape(b, seq_len, d)


def _rope_tables(seq_len):
    rows = seq_len // GRID_W
    row_id = jnp.repeat(jnp.arange(rows), GRID_W).astype(F32)
    col_id = jnp.tile(jnp.arange(GRID_W), rows).astype(F32)
    inv = ROPE_THETA ** (-jnp.arange(0, AXIS_DIM, 2, dtype=F32) / AXIS_DIM)
    ang = jnp.stack([row_id[:, None] * inv, col_id[:, None] * inv], axis=1)
    cos = jnp.cos(ang)[:, :, None, :]
    sin = jnp.sin(ang)[:, :, None, :]
    shape = (seq_len, 2, 2, AXIS_DIM // 2)
    cos = jnp.broadcast_to(cos, shape).reshape(seq_len, HEAD_DIM)
    sin = (jnp.broadcast_to(sin, shape) * jnp.array([-1.0, 1.0], F32)[None, None, :, None]).reshape(seq_len, HEAD_DIM)
    return jnp.tile(cos, (1, LANES // HEAD_DIM)), jnp.tile(sin, (1, LANES // HEAD_DIM))


def _trunk_layer(x, mod, layer, w, lam_init, alpha, rope, cache, want_f32_kv, tiles):
    outs = _front_call(x, mod, layer, w, rope, want_f32_kv, tiles["front"])
    if want_f32_kv:
        q, k, v, kf, vf, y_pool, y_conv = outs
    else:
        q, k, v, y_pool, y_conv = outs
        kf = vf = None
    y_attn = _attn_call(q, k, v, cache, layer, w, lam_init, tiles["attn"])
    x1, u2c, ri, rw, cnt = _post_call(y_pool, y_attn, y_conv, x, mod, layer, w, alpha, tiles["post"], tiles["moe"])
    moe_c = _moe_call(u2c.reshape(-1, LANES), ri, rw, cnt, layer, w, tiles["moe"], tiles["post"])
    x2 = _ln2_call(moe_c, x1, mod, layer, w, alpha, tiles["ln2"])
    return x2, kf, vf


def kernel(x_prompt, x_sample, cache_k, cache_v, c, c_ctx, w_mod, b_mod, w_in, w_out, pool_w, pool_scale, diff_lambda, subln_g, conv_dw, conv_ln_g, conv_ln_b, conv_pw, ln1_g, ln1_b, ln2_g, ln2_b, router_grp, router_grp_b, router_exp, router_exp_b, w_gate, w_up, w_down):
    depth = w_in.shape[0]
    batch, seq, d = x_prompt.shape
    dec_batch, dec_seq, _ = x_sample.shape
    past = cache_k.shape[2]
    alpha = (2 * depth) ** 0.25

    mod_rows = 16
    cvec = jnp.concatenate([c, c_ctx[None], jnp.zeros((mod_rows - dec_batch - 1, d), F32)], axis=0)
    mod_all = _mod_call(cvec, w_mod, b_mod)

    eye = jnp.eye(len(POOL_WINDOWS), dtype=F32)
    pool_bd = (pool_w[:, :, :, None, :] * eye[None, :, None, :, None]).reshape(depth, POOL_WIDTH, POOL_WIDTH)
    router_w = jnp.concatenate(
        [router_exp, router_grp, jnp.zeros((depth, d, ROUTER_LANES - N_EXPERTS - N_GROUPS), F32)], axis=-1)
    router_b = jnp.concatenate(
        [router_exp_b, router_grp_b, jnp.zeros((depth, ROUTER_LANES - N_EXPERTS - N_GROUPS), F32)], axis=-1)
    r3 = lambda z: z.reshape(depth, 1, z.shape[-1])
    w = {
        "w_in": w_in.astype(BF16), "w_out": w_out.astype(BF16), "pool_bd": pool_bd.astype(BF16),
        "pool_scale": r3(pool_scale), "diff_lambda": diff_lambda, "subln_g": r3(subln_g),
        "conv_dw": conv_dw, "conv_ln_g": r3(conv_ln_g), "conv_ln_b": r3(conv_ln_b),
        "conv_pw": conv_pw.astype(BF16), "ln1_g": r3(ln1_g), "ln1_b": r3(ln1_b), "ln2_g": r3(ln2_g),
        "ln2_b": r3(ln2_b), "router_w": router_w, "router_b": r3(router_b),
        "w_gate": w_gate, "w_up": w_up, "w_down": w_down,
    }
    rope = _rope_tables(dec_seq)
    cache = (cache_k.reshape(dec_batch, depth, past, QK_WIDTH), cache_v.reshape(dec_batch, depth, past, ATTN_WIDTH))
    ctx_tiles = {"front": seq, "attn": seq, "post": seq, "moe": 2048, "ln2": seq}
    lat_tiles = {"front": 512, "attn": 256, "post": 512, "moe": 2048, "ln2": 512}

    xp, xs = x_prompt, x_sample
    new_k, new_v = [], []
    for l in range(depth):
        lam_init = 0.8 - 0.6 * math.exp(-0.3 * l)
        mod_ctx = mod_all[l, dec_batch:dec_batch + 1].reshape(1, 6, d)
        mod_lat = mod_all[l, :dec_batch].reshape(dec_batch, 6, d)
        xp, kf, vf = _trunk_layer(xp, mod_ctx, l, w, lam_init, alpha, None, None, True, ctx_tiles)
        new_k.append(kf.reshape(batch, seq, ATTN_HEADS, 2, HEAD_DIM))
        new_v.append(vf.reshape(batch, seq, ATTN_HEADS, V_DIM))
        xs, _, _ = _trunk_layer(xs, mod_lat, l, w, lam_init, alpha, rope, cache, False, lat_tiles)
    return (xp, xs, jnp.stack(new_k, axis=1), jnp.stack(new_v, axis=1))
```

```python
import functools
import math

import jax
import jax.numpy as jnp
from jax import lax
from jax.experimental import pallas as pl
from jax.experimental.pallas import tpu as pltpu

F32 = jnp.float32
BF16 = jnp.bfloat16

D_MODEL = 1024
GRID_W = 64
POOL_WIDTH = 256
POOL_WINDOWS = (2, 4, 8, 16)
POOL_GC = 64
ATTN_HEADS = 4
HEAD_DIM = 64
V_DIM = 128
ATTN_WIDTH = 512
QK_WIDTH = 512
CONV_WIDTH = 256
CONV_K = 31
IN_WIDTH = 2304
AXIS_DIM = 32
ROPE_THETA = 10000.0
N_GROUPS = 4
EXPERTS_PER_GROUP = 8
N_EXPERTS = 32
EXPERT_FF = 256
LN_EPS = 1e-5

O_A = 0
O_Q = POOL_WIDTH
O_K = O_Q + QK_WIDTH
O_V = O_K + QK_WIDTH
O_C = O_V + ATTN_WIDTH

TOP_K = 2
HALO = 16
LANES = 128
SUBLANES = 8
ROW_CHUNKS = D_MODEL // LANES
ROUTER_LANES = 128
ROUTE_BLOCK = 512
EXPERT_BLOCK = 128
NEG_BIG = -1e30
VMEM_LIMIT = 56 * 1024 * 1024


def _silu(x):
    return x * jax.nn.sigmoid(x)


def _layer_norm(x, g, b):
    mu = jnp.mean(x, axis=-1, keepdims=True)
    xc = x - mu
    var = jnp.mean(xc * xc, axis=-1, keepdims=True)
    return xc * lax.rsqrt(var + LN_EPS) * g + b


def _dot(a, b):
    return jnp.dot(a, b, preferred_element_type=F32)


def _params(sem):
    return pltpu.CompilerParams(dimension_semantics=sem, vmem_limit_bytes=VMEM_LIMIT)


def _mod_kernel(c_ref, w_ref, b_ref, o_ref):
    s = _silu(c_ref[...])
    o_ref[0] = _dot(s.astype(BF16), w_ref[0].astype(BF16)) + b_ref[0]


def _mod_call(cvec, w_mod, b_mod):
    depth, d, n6 = w_mod.shape
    rows = cvec.shape[0]
    nb = 1536
    return pl.pallas_call(
        _mod_kernel,
        grid=(depth, n6 // nb),
        in_specs=[
            pl.BlockSpec((rows, d), lambda l, j: (0, 0)),
            pl.BlockSpec((1, d, nb), lambda l, j: (l, 0, j)),
            pl.BlockSpec((1, 1, nb), lambda l, j: (l, 0, j)),
        ],
        out_specs=pl.BlockSpec((1, rows, nb), lambda l, j: (l, 0, j)),
        out_shape=jax.ShapeDtypeStruct((depth, rows, n6), F32),
        compiler_params=_params(("parallel", "parallel")),
        name="mod",
    )(cvec, w_mod, b_mod.reshape(depth, 1, n6))


def _front_kernel(*refs, seq_len, tile, has_rope, want_f32_kv):
    it = iter(refs)
    x_ref, xp_ref, xn_ref, mod_ref, win_ref = next(it), next(it), next(it), next(it), next(it)
    poolw_ref, pscale_ref, dw_ref, cg_ref, cb_ref, pw_ref = (next(it) for _ in range(6))
    if has_rope:
        cos_ref, sin_ref = next(it), next(it)
    q_ref, k_ref, v_ref = next(it), next(it), next(it)
    if want_f32_kv:
        kf_ref, vf_ref = next(it), next(it)
    yp_ref, yc_ref = next(it), next(it)

    i = pl.program_id(1)
    n_ext = tile + 2 * HALO
    shift = mod_ref[0, 0:1, :]
    scale = mod_ref[0, 1:2, :]
    xe = jnp.concatenate([xp_ref[0], x_ref[0], xn_ref[0]], axis=0)
    u = (xe * (1.0 + scale) + shift).astype(BF16)
    proj = _dot(u, win_ref[0])

    row = lax.broadcasted_iota(jnp.int32, (n_ext, 1), 0)
    gpos = i * tile - HALO + row
    valid = (gpos >= 0) & (gpos < seq_len)

    a_ext = jnp.where(valid, proj[:, O_A:O_A + POOL_WIDTH], 0.0)

    def sh(z, s):
        return pltpu.roll(z, s % n_ext, axis=0)

    p2 = a_ext + sh(a_ext, 1)
    p4 = sh(p2, 1) + sh(p2, -1)
    p8 = sh(p4, 2) + sh(p4, -2)
    p16 = sh(p8, 4) + sh(p8, -4)
    t = gpos[HALO:HALO + tile]
    lane = lax.broadcasted_iota(jnp.int32, (1, POOL_WIDTH), 1)
    mean = None
    for g, (w, p) in enumerate(zip(POOL_WINDOWS, (p2, p4, p8, p16))):
        cnt = jnp.minimum(t + w // 2, seq_len) - jnp.maximum(t - w // 2, 0)
        m = p[HALO:HALO + tile] * (1.0 / cnt.astype(F32))
        mean = m if mean is None else jnp.where(lane >= g * POOL_GC, m, mean)
    pooled = mean - a_ext[HALO:HALO + tile]
    y_pool = _dot(pooled.astype(BF16), poolw_ref[0]) * pscale_ref[0]
    yp_ref[0] = y_pool.astype(BF16)

    ca = proj[:, O_C:O_C + CONV_WIDTH]
    cb = proj[:, O_C + CONV_WIDTH:O_C + 2 * CONV_WIDTH]
    h = jnp.where(valid, ca * jax.nn.sigmoid(cb), 0.0)
    acc = None
    for j in range(CONV_K):
        off = HALO - CONV_K // 2 + j
        term = h[off:off + tile] * dw_ref[0, j:j + 1, :]
        acc = term if acc is None else acc + term
    hn = _silu(_layer_norm(acc, cg_ref[0], cb_ref[0]))
    yc_ref[0] = _dot(hn.astype(BF16), pw_ref[0]).astype(BF16)

    qf = proj[HALO:HALO + tile, O_Q:O_Q + QK_WIDTH]
    kf = proj[HALO:HALO + tile, O_K:O_K + QK_WIDTH]
    vf = proj[HALO:HALO + tile, O_V:O_V + ATTN_WIDTH]
    if want_f32_kv:
        kf_ref[0] = kf
        vf_ref[0] = vf
    if has_rope:
        cos = cos_ref[...]
        sin = sin_ref[...]
        l128 = lax.broadcasted_iota(jnp.int32, (1, LANES), 1)
        first_half = (l128 & (AXIS_DIM - 1)) < (AXIS_DIM // 2)

        def rope(z):
            outs = []
            for c in range(QK_WIDTH // LANES):
                zc = z[:, c * LANES:(c + 1) * LANES]
                partner = jnp.where(first_half,
                                    pltpu.roll(zc, LANES - AXIS_DIM // 2, axis=1),
                                    pltpu.roll(zc, AXIS_DIM // 2, axis=1))
                outs.append(zc * cos + partner * sin)
            return jnp.concatenate(outs, axis=1)

        qf = rope(qf)
        kf = rope(kf)
    q_ref[0] = qf.astype(BF16)
    k_ref[0] = kf.astype(BF16)
    v_ref[0] = vf.astype(BF16)


def _front_call(x, mod, layer, w, rope, want_f32_kv, tile):
    b, seq_len, d = x.shape
    nt = seq_len // tile
    hb = tile // HALO
    n_hblk = seq_len // HALO
    mod_b = (lambda bi, i: (bi, 0, 0)) if mod.shape[0] == b else (lambda bi, i: (0, 0, 0))
    lw = lambda bi, i: (layer, 0, 0)
    in_specs = [
        pl.BlockSpec((1, tile, d), lambda bi, i: (bi, i, 0)),
        pl.BlockSpec((1, HALO, d), lambda bi, i: (bi, jnp.maximum(i * hb - 1, 0), 0)),
        pl.BlockSpec((1, HALO, d), lambda bi, i: (bi, jnp.minimum((i + 1) * hb, n_hblk - 1), 0)),
        pl.BlockSpec((1, 6, d), mod_b),
        pl.BlockSpec((1, d, IN_WIDTH), lw),
        pl.BlockSpec((1, POOL_WIDTH, POOL_WIDTH), lw),
        pl.BlockSpec((1, 1, POOL_WIDTH), lw),
        pl.BlockSpec((1, CONV_K, CONV_WIDTH), lw),
        pl.BlockSpec((1, 1, CONV_WIDTH), lw),
        pl.BlockSpec((1, 1, CONV_WIDTH), lw),
        pl.BlockSpec((1, CONV_WIDTH, CONV_WIDTH), lw),
    ]
    args = [x, x, x, mod, w["w_in"], w["pool_bd"], w["pool_scale"], w["conv_dw"], w["conv_ln_g"],
            w["conv_ln_b"], w["conv_pw"]]
    if rope is not None:
        in_specs += [pl.BlockSpec((tile, LANES), lambda bi, i: (i, 0))] * 2
        args += list(rope)
    tok = lambda width: pl.BlockSpec((1, tile, width), lambda bi, i: (bi, i, 0))
    out_specs = [tok(QK_WIDTH), tok(QK_WIDTH), tok(ATTN_WIDTH)]
    out_shape = [jax.ShapeDtypeStruct((b, seq_len, QK_WIDTH), BF16)] * 3
    if want_f32_kv:
        out_specs += [tok(QK_WIDTH), tok(ATTN_WIDTH)]
        out_shape += [jax.ShapeDtypeStruct((b, seq_len, QK_WIDTH), F32)] * 2
    out_specs += [tok(POOL_WIDTH), tok(CONV_WIDTH)]
    out_shape += [jax.ShapeDtypeStruct((b, seq_len, POOL_WIDTH), BF16)] * 2
    kern = functools.partial(_front_kernel, seq_len=seq_len, tile=tile, has_rope=rope is not None,
                             want_f32_kv=want_f32_kv)
    return pl.pallas_call(
        kern, grid=(b, nt), in_specs=in_specs, out_specs=out_specs, out_shape=out_shape,
        compiler_params=_params(("parallel", "parallel")), name="front",
    )(*args)


def _attn_kernel(*refs, lam_init, has_cache):
    if has_cache:
        lam_ref, g_ref, q_ref, k_ref, v_ref, ck_ref, cv_ref, o_ref = refs
    else:
        lam_ref, g_ref, q_ref, k_ref, v_ref, o_ref = refs
    lp = lam_ref[0]
    lam = (jnp.exp(jnp.sum(lp[0:1] * lp[1:2], axis=-1, keepdims=True))
           - jnp.exp(jnp.sum(lp[2:3] * lp[3:4], axis=-1, keepdims=True)) + lam_init)

    q = q_ref[0] * (1.0 / math.sqrt(HEAD_DIM))
    lane = lax.broadcasted_iota(jnp.int32, (1, LANES), 1)
    zero = jnp.zeros_like(q)
    q_maps = (jnp.where(lane < HEAD_DIM, q, zero), jnp.where(lane >= HEAD_DIM, q, zero))
    nt = (((1,), (1,)), ((), ()))
    keys = [k_ref[0]]
    vals = [v_ref[0]]
    if has_cache:
        keys.append(ck_ref[0, 0].astype(BF16))
        vals.append(cv_ref[0, 0].astype(BF16))

    probs = []
    for qm in q_maps:
        s = [lax.dot_general(qm, kk, nt, preferred_element_type=F32) for kk in keys]
        m = functools.reduce(jnp.maximum, [jnp.max(z, axis=-1, keepdims=True) for z in s])
        e = [jnp.exp(z - m) for z in s]
        den = functools.reduce(jnp.add, [jnp.sum(z, axis=-1, keepdims=True) for z in e])
        probs.append((e, 1.0 / den))
    (e1, r1), (e2, r2) = probs
    r2 = lam * r2
    o = None
    for z1, z2, vv in zip(e1, e2, vals):
        a = z1 * r1 - z2 * r2
        part = _dot(a.astype(BF16), vv)
        o = part if o is None else o + part
    o = o * lax.rsqrt(jnp.mean(o * o, axis=-1, keepdims=True) + LN_EPS) * g_ref[0] * (1.0 - lam_init)
    o_ref[0] = o.astype(BF16)


def _attn_call(q, k, v, cache, layer, w, lam_init, tq):
    b, seq_len, _ = q.shape
    lw = lambda bi, h, i: (layer, 0, 0)
    in_specs = [
        pl.BlockSpec((1, 4, HEAD_DIM), lw),
        pl.BlockSpec((1, 1, V_DIM), lw),
        pl.BlockSpec((1, tq, LANES), lambda bi, h, i: (bi, i, h)),
        pl.BlockSpec((1, seq_len, LANES), lambda bi, h, i: (bi, 0, h)),
        pl.BlockSpec((1, seq_len, V_DIM), lambda bi, h, i: (bi, 0, h)),
    ]
    args = [w["diff_lambda"], w["subln_g"], q, k, v]
    if cache is not None:
        past = cache[0].shape[2]
        cspec = pl.BlockSpec((1, 1, past, LANES), lambda bi, h, i: (bi, layer, 0, h))
        in_specs += [cspec, cspec]
        args += list(cache)
    kern = functools.partial(_attn_kernel, lam_init=lam_init, has_cache=cache is not None)
    return pl.pallas_call(
        kern, grid=(b, ATTN_HEADS, seq_len // tq), in_specs=in_specs,
        out_specs=pl.BlockSpec((1, tq, V_DIM), lambda bi, h, i: (bi, i, h)),
        out_shape=jax.ShapeDtypeStruct((b, seq_len, ATTN_WIDTH), BF16),
        compiler_params=_params(("parallel", "parallel", "parallel")), name="attn",
    )(*args)


def _post_kernel(yp_ref, ya_ref, yc_ref, x_ref, mod_ref, wout_ref, g_ref, b_ref, wr_ref, br_ref,
                 x1_ref, u2c_ref, sel_ref, *, alpha):
    o1 = POOL_WIDTH
    o2 = o1 + ATTN_WIDTH
    y = (_dot(yp_ref[0], wout_ref[0, 0:o1, :]) + _dot(ya_ref[0], wout_ref[0, o1:o2, :])
         + _dot(yc_ref[0], wout_ref[0, o2:, :]))
    gate1 = mod_ref[0, 2:3, :]
    shift2 = mod_ref[0, 3:4, :]
    scale2 = mod_ref[0, 4:5, :]
    x1 = _layer_norm(alpha * x_ref[0] + gate1 * y, g_ref[0], b_ref[0])
    x1_ref[0] = x1
    u2 = x1 * (1.0 + scale2) + shift2
    tile = u2.shape[0]
    for c in range(ROW_CHUNKS):
        u2c_ref[:, c, :, :] = u2[:, c * LANES:(c + 1) * LANES].reshape(tile // SUBLANES, SUBLANES, LANES)

    logits =jnp.dot(u2, wr_ref[0], preferred_element_type=F32,
                     precision=lax.Precision.HIGHEST) + br_ref[0]
    lane = lax.broadcasted_iota(jnp.int32, (1, ROUTER_LANES), 1)
    lane_f = lane.astype(F32)
    is_g = (lane >= N_EXPERTS) & (lane < N_EXPERTS + N_GROUPS)
    rmax = lambda z: jnp.max(z, axis=-1, keepdims=True)
    rmin = lambda z: jnp.min(z, axis=-1, keepdims=True)
    gl = jnp.where(is_g, logits, NEG_BIG)
    ge = jnp.where(is_g, jnp.exp(gl - rmax(gl)), 0.0)
    gp = ge / jnp.sum(ge, axis=-1, keepdims=True)
    g_w = rmax(gp)
    g_idx = rmin(jnp.where(is_g & (gp == g_w), lane_f - N_EXPERTS, 1e9))
    in_grp = (lane < N_EXPERTS) & (jnp.right_shift(lane, 3).astype(F32) == g_idx)
    el = jnp.where(in_grp, logits, NEG_BIG)
    v1 = rmax(el)
    i1 = rmin(jnp.where(in_grp & (el == v1), lane_f, 1e9))
    rest = in_grp & (lane_f != i1)
    el2 = jnp.where(rest, logits, NEG_BIG)
    v2 = rmax(el2)
    i2 = rmin(jnp.where(rest & (el2 == v2), lane_f, 1e9))
    tt = jnp.exp(v2 - v1)
    w1 = 1.0 / (1.0 + tt)
    w2 = tt / (1.0 + tt)
    sel = jnp.zeros((tile, ROUTER_LANES), F32)
    for j, col in enumerate((i1, i2, g_w * w1, g_w * w2)):
        sel = jnp.where(lane == j, col, sel)
    sel_ref[0] = sel


def _post_call(y_pool, y_attn, y_conv, x, mod, layer, w, alpha, tile):
    b, seq_len, d = x.shape
    nt = seq_len // tile
    mod_b = (lambda bi, i: (bi, 0, 0)) if mod.shape[0] == b else (lambda bi, i: (0, 0, 0))
    lw = lambda bi, i: (layer, 0, 0)
    tok = lambda width: pl.BlockSpec((1, tile, width), lambda bi, i: (bi, i, 0))
    in_specs = [tok(POOL_WIDTH), tok(ATTN_WIDTH), tok(CONV_WIDTH), tok(d),
                pl.BlockSpec((1, 6, d), mod_b),
                pl.BlockSpec((1, d, d), lw),
                pl.BlockSpec((1, 1, d), lw), pl.BlockSpec((1, 1, d), lw),
                pl.BlockSpec((1, d, ROUTER_LANES), lw), pl.BlockSpec((1, 1, ROUTER_LANES), lw)]
    out_specs = [tok(d),
                 pl.BlockSpec((tile // SUBLANES, ROW_CHUNKS, SUBLANES, LANES), lambda bi, i: (bi * nt + i, 0, 0, 0)),
                 tok(ROUTER_LANES)]
    out_shape = [jax.ShapeDtypeStruct((b, seq_len, d), F32),
                 jax.ShapeDtypeStruct((b * seq_len // SUBLANES, ROW_CHUNKS, SUBLANES, LANES), F32),
                 jax.ShapeDtypeStruct((b, seq_len, ROUTER_LANES), F32)]
    return pl.pallas_call(
        functools.partial(_post_kernel, alpha=alpha),
        grid=(b, nt), in_specs=in_specs, out_specs=out_specs, out_shape=out_shape,
        compiler_params=_params(("parallel", "parallel")), name="post",
    )(y_pool, y_attn, y_conv, x, mod, w["w_out"], w["ln1_g"], w["ln1_b"], w["router_w"], w["router_b"])


def _route_kernel(sel_ref, ri_ref, rw_ref, seg_ref, *, tile):
    lane = lax.broadcasted_iota(jnp.int32, (1, ROUTER_LANES), 1)
    lane_f = lane.astype(F32)
    blk = ROUTE_BLOCK
    r_io = lax.broadcasted_iota(jnp.int32, (blk, blk), 0)
    c_io = lax.broadcasted_iota(jnp.int32, (blk, blk), 1)
    before = jnp.where(r_io > c_io, 1.0, 0.0).astype(BF16)

    def picks(j):
        s = sel_ref[j * blk:(j + 1) * blk, :]
        return s, lane_f == s[:, 0:1], lane_f == s[:, 1:2]

    carry = jnp.zeros((1, ROUTER_LANES), F32)
    prior = []
    for j in range(tile // blk):
        _, oh1, oh2 = picks(j)
        member = jnp.where(oh1 | oh2, 1.0, 0.0)
        prior.append(_dot(before, member.astype(BF16)) + carry)
        carry = carry + jnp.sum(member, axis=0, keepdims=True)

    cnt = jnp.broadcast_to(carry, (SUBLANES, ROUTER_LANES)).astype(jnp.int32)
    padded = jnp.where(lane < N_EXPERTS, jnp.right_shift(cnt + (SUBLANES - 1), 3) * SUBLANES, 0)
    incl = padded
    step = 1
    while step < N_EXPERTS:
        incl = incl + jnp.where(lane >= step, pltpu.roll(incl, step, axis=1), 0)
        step *= 2
    start = incl - padded
    seg_ref[0] = jnp.where(lane < N_EXPERTS, cnt, pltpu.roll(start, N_EXPERTS, axis=1))
    start_f = start[0:1, :].astype(F32)

    for j in range(tile // blk):
        s, oh1, oh2 = picks(j)
        cols = []
        for oh in (oh1, oh2):
            p = jnp.sum(jnp.where(oh, start_f + prior[j], 0.0), axis=-1, keepdims=True).astype(jnp.int32)
            addr = jnp.right_shift(p, 3) * (SUBLANES * ROW_CHUNKS) + jnp.bitwise_and(p, SUBLANES - 1)
            cols.append(addr.astype(F32))
        z = jnp.where(lane == 0, cols[0], jnp.where(lane == 1, cols[1], s))
        zt = z.T
        ri_ref[0, :, j * blk:(j + 1) * blk] = zt[0:2, :].astype(jnp.int32)
        rw_ref[0, :, j * blk:(j + 1) * blk] = zt[2:4, :]


def _route_call(sel, tile):
    n_tok = sel.shape[0]
    n_tiles = n_tok // tile
    return pl.pallas_call(
        functools.partial(_route_kernel, tile=tile),
        grid=(n_tiles,),
        in_specs=[pl.BlockSpec((tile, ROUTER_LANES), lambda t: (t, 0))],
        out_specs=[pl.BlockSpec((1, TOP_K, tile), lambda t: (t, 0, 0)),
                   pl.BlockSpec((1, TOP_K, tile), lambda t: (t, 0, 0)),
                   pl.BlockSpec((1, SUBLANES, ROUTER_LANES), lambda t: (t, 0, 0))],
        out_shape=[jax.ShapeDtypeStruct((n_tiles, TOP_K, tile), jnp.int32),
                   jax.ShapeDtypeStruct((n_tiles, TOP_K, tile), F32),
                   jax.ShapeDtypeStruct((n_tiles, SUBLANES, ROUTER_LANES), jnp.int32)],
        compiler_params=_params(("parallel",)), name="route",
    )(sel)


def _row_at(start):
    return pl.ds(start, ROW_CHUNKS, stride=SUBLANES)


def _moe_kernel(ri_ref, rw_ref, seg_ref, u2c_ref, wg_ref, wu_ref, wd_ref, o_ref, sorted_ref, wgb_ref, wub_ref,
                wdb_ref, *, tile):
    e = pl.program_id(1)
    groups = tile // SUBLANES
    group_rows = SUBLANES * ROW_CHUNKS

    @pl.when((pl.program_id(0) == 0) & (e == 0))
    def _():
        sorted_ref[...] = jnp.zeros_like(sorted_ref)

    @pl.when(e == 0)
    def _():
        def disp_body(a, carry):
            for s in range(SUBLANES):
                row = u2c_ref[_row_at(a * group_rows + s), :]
                for k in range(TOP_K):
                    sorted_ref[_row_at(ri_ref[0, k, a * SUBLANES + s]), :] = row
            return carry
        lax.fori_loop(0, groups, disp_body, 0)

    cnt = seg_ref[0, 0, e]
    base = seg_ref[0, 0, N_EXPERTS + e]
    wgb_ref[...] = wg_ref[0, 0].astype(BF16)
    wub_ref[...] = wu_ref[0, 0].astype(BF16)
    wdb_ref[...] = wd_ref[0, 0].astype(BF16)
    blk_groups = EXPERT_BLOCK // SUBLANES
    row_id = lax.broadcasted_iota(jnp.int32, (SUBLANES, 1), 0)

    def tile_at(start, g, c):
        return pl.ds(pl.multiple_of(start + (g * ROW_CHUNKS + c) * SUBLANES, SUBLANES), SUBLANES)

    def blk_body(bi, carry):
        start = (base + bi * EXPERT_BLOCK) * ROW_CHUNKS
        x = jnp.concatenate(
            [jnp.concatenate([sorted_ref[tile_at(start, g, c), :] for g in range(blk_groups)], axis=0)
             for c in range(ROW_CHUNKS)], axis=1).astype(BF16)
        h = _silu(_dot(x, wgb_ref[...])) * _dot(x, wub_ref[...])
        y = _dot(h.astype(BF16), wdb_ref[...])
        n_valid = cnt - bi * EXPERT_BLOCK
        for g in range(blk_groups):
            keep = (row_id + g * SUBLANES) < n_valid
            for c in range(ROW_CHUNKS):
                sl = tile_at(start, g, c)
                piece = y[g * SUBLANES:(g + 1) * SUBLANES, c * LANES:(c + 1) * LANES]
                sorted_ref[sl, :] = jnp.where(keep, piece, sorted_ref[sl, :])
        return carry
    lax.fori_loop(0, (cnt + EXPERT_BLOCK - 1) // EXPERT_BLOCK, blk_body, 0)

    @pl.when(e == N_EXPERTS - 1)
    def _():
        def comb_body(a, carry):
            for s in range(SUBLANES):
                n = a * SUBLANES + s
                acc = None
                for k in range(TOP_K):
                    term = sorted_ref[_row_at(ri_ref[0, k, n]), :] * rw_ref[0, k, n]
                    acc = term if acc is None else acc + term
                o_ref[_row_at(a * group_rows + s), :] = acc
            return carry
        lax.fori_loop(0, groups, comb_body, 0)


def _moe_call(u2c, ri, rw, seg, layer, w, tile):
    n_tok = u2c.shape[0] // ROW_CHUNKS
    n_tiles = n_tok // tile
    d = D_MODEL
    sorted_rows = tile * TOP_K + N_EXPERTS * (SUBLANES - 1) + EXPERT_BLOCK
    sorted_rows = -(-sorted_rows // SUBLANES) * SUBLANES
    smem = lambda shape, imap: pl.BlockSpec(shape, imap, memory_space=pltpu.SMEM)
    in_specs = [smem((1, TOP_K, tile), lambda t, e: (t, 0, 0)),
                smem((1, TOP_K, tile), lambda t, e: (t, 0, 0)),
                smem((1, SUBLANES, ROUTER_LANES), lambda t, e: (t, 0, 0)),
                pl.BlockSpec((tile * ROW_CHUNKS, LANES), lambda t, e: (t, 0), pipeline_mode=pl.Buffered(1)),
                pl.BlockSpec((1, 1, d, EXPERT_FF), lambda t, e: (layer, e, 0, 0)),
                pl.BlockSpec((1, 1, d, EXPERT_FF), lambda t, e: (layer, e, 0, 0)),
                pl.BlockSpec((1, 1, EXPERT_FF, d), lambda t, e: (layer, e, 0, 0))]
    return pl.pallas_call(
        functools.partial(_moe_kernel, tile=tile),
        grid=(n_tiles, N_EXPERTS), in_specs=in_specs,
        out_specs=pl.BlockSpec((tile * ROW_CHUNKS, LANES), lambda t, e: (t, 0)),
        out_shape=jax.ShapeDtypeStruct((n_tok * ROW_CHUNKS, LANES), F32),
        scratch_shapes=[pltpu.VMEM((sorted_rows * ROW_CHUNKS, LANES), F32),
                        pltpu.VMEM((d, EXPERT_FF), BF16), pltpu.VMEM((d, EXPERT_FF), BF16),
                        pltpu.VMEM((EXPERT_FF, d), BF16)],
        compiler_params=_params(("arbitrary", "arbitrary")), name="moe",
    )(ri, rw, seg, u2c, w["w_gate"], w["w_up"], w["w_down"])


def _ln2_kernel(m_ref, x1_ref, mod_ref, g_ref, b_ref, o_ref, *, alpha):
    tile = x1_ref.shape[0]
    moe = jnp.concatenate([m_ref[:, c, :, :].reshape(tile, LANES) for c in range(ROW_CHUNKS)], axis=1)
    gate2 = mod_ref[0, 5:6, :]
    o_ref[...] = _layer_norm(alpha * x1_ref[...] + gate2 * moe, g_ref[0], b_ref[0])


def _ln2_call(moe_c, x1, mod, layer, w, alpha, tile):
    b, seq_len, d = x1.shape
    n_tok = b * seq_len
    per_seq = seq_len // tile
    mod_b = (lambda t: (t // per_seq, 0, 0)) if mod.shape[0] == b else (lambda t: (0, 0, 0))
    lw = lambda t: (layer, 0, 0)
    out = pl.pallas_call(
        functools.partial(_ln2_kernel, alpha=alpha),
        grid=(n_tok // tile,),
        in_specs=[pl.BlockSpec((tile // SUBLANES, ROW_CHUNKS, SUBLANES, LANES), lambda t: (t, 0, 0, 0)),
                  pl.BlockSpec((tile, d), lambda t: (t, 0)),
                  pl.BlockSpec((1, 6, d), mod_b),
                  pl.BlockSpec((1, 1, d), lw), pl.BlockSpec((1, 1, d), lw)],
        out_specs=pl.BlockSpec((tile, d), lambda t: (t, 0)),
        out_shape=jax.ShapeDtypeStruct((n_tok, d), F32),
        compiler_params=_params(("parallel",)), name="ln2",
    )(moe_c.reshape(n_tok // SUBLANES, ROW_CHUNKS, SUBLANES, LANES), x1.reshape(n_tok, d), mod,
      w["ln2_g"], w["ln2_b"])
    return out.reshape(b, seq_len, d)


def _rope_tables(seq_len):
    rows = seq_len // GRID_W
    row_id = jnp.repeat(jnp.arange(rows), GRID_W).astype(F32)
    col_id = jnp.tile(jnp.arange(GRID_W), rows).astype(F32)
    inv = ROPE_THETA ** (-jnp.arange(0, AXIS_DIM, 2, dtype=F32) / AXIS_DIM)
    ang = jnp.stack([row_id[:, None] * inv, col_id[:, None] * inv], axis=1)
    cos = jnp.cos(ang)[:, :, None, :]
    sin = jnp.sin(ang)[:, :, None, :]
    shape = (seq_len, 2, 2, AXIS_DIM // 2)
    cos = jnp.broadcast_to(cos, shape).reshape(seq_len, HEAD_DIM)
    sin = (jnp.broadcast_to(sin, shape) * jnp.array([-1.0, 1.0], F32)[None, None, :, None]).reshape(seq_len, HEAD_DIM)
    return jnp.tile(cos, (1, LANES // HEAD_DIM)), jnp.tile(sin, (1, LANES // HEAD_DIM))


def _trunk_layer(x, mod, layer, w, lam_init, alpha, rope, cache, want_f32_kv, tiles):
    outs = _front_call(x, mod, layer, w, rope, want_f32_kv, tiles["front"])
    if want_f32_kv:
        q, k, v, kf, vf, y_pool, y_conv = outs
    else:
        q, k, v, y_pool, y_conv = outs
        kf = vf = None
    y_attn = _attn_call(q, k, v, cache, layer, w, lam_init, tiles["attn"])
    x1, u2c, sel = _post_call(y_pool, y_attn, y_conv, x, mod, layer, w, alpha, tiles["post"])
    ri, rw, seg = _route_call(sel.reshape(-1, ROUTER_LANES), tiles["moe"])
    moe_c = _moe_call(u2c.reshape(-1, LANES), ri, rw, seg, layer, w, tiles["moe"])
    x2 = _ln2_call(moe_c, x1, mod, layer, w, alpha, tiles["ln2"])
    return x2, kf, vf


def kernel(x_prompt, x_sample, cache_k, cache_v, c, c_ctx, w_mod, b_mod, w_in, w_out, pool_w, pool_scale, diff_lambda, subln_g, conv_dw, conv_ln_g, conv_ln_b, conv_pw, ln1_g, ln1_b, ln2_g, ln2_b, router_grp, router_grp_b, router_exp, router_exp_b, w_gate, w_up, w_down):
    depth = w_in.shape[0]
    batch, seq, d = x_prompt.shape
    dec_batch, dec_seq, _ = x_sample.shape
    past = cache_k.shape[2]
    alpha = (2 * depth) ** 0.25

    mod_rows = 16
    cvec = jnp.concatenate([c, c_ctx[None], jnp.zeros((mod_rows - dec_batch - 1, d), F32)], axis=0)
    mod_all = _mod_call(cvec, w_mod, b_mod)

    eye = jnp.eye(len(POOL_WINDOWS), dtype=F32)
    pool_bd = (pool_w[:, :, :, None, :] * eye[None, :, None, :, None]).reshape(depth, POOL_WIDTH, POOL_WIDTH)
    router_w = jnp.concatenate(
        [router_exp, router_grp, jnp.zeros((depth, d, ROUTER_LANES - N_EXPERTS - N_GROUPS), F32)], axis=-1)
    router_b = jnp.concatenate(
        [router_exp_b, router_grp_b, jnp.zeros((depth, ROUTER_LANES - N_EXPERTS - N_GROUPS), F32)], axis=-1)
    r3 = lambda z: z.reshape(depth, 1, z.shape[-1])
    w = {
        "w_in": w_in.astype(BF16), "w_out": w_out.astype(BF16), "pool_bd": pool_bd.astype(BF16),
        "pool_scale": r3(pool_scale), "diff_lambda": diff_lambda, "subln_g": r3(subln_g),
        "conv_dw": conv_dw, "conv_ln_g": r3(conv_ln_g), "conv_ln_b": r3(conv_ln_b),
        "conv_pw": conv_pw.astype(BF16), "ln1_g": r3(ln1_g), "ln1_b": r3(ln1_b), "ln2_g": r3(ln2_g),
        "ln2_b": r3(ln2_b), "router_w": router_w, "router_b": r3(router_b),
        "w_gate": w_gate, "w_up": w_up, "w_down": w_down,
    }
    rope = _rope_tables(dec_seq)
    cache = (cache_k.reshape(dec_batch, depth, past, QK_WIDTH), cache_v.reshape(dec_batch, depth, past, ATTN_WIDTH))
    ctx_tiles = {"front": seq, "attn": seq, "post": seq, "moe": 2048, "ln2": seq}
    lat_tiles = {"front": 512, "attn": 256, "post": 512, "moe": 2048, "ln2": 512}

    xp, xs = x_prompt, x_sample
    new_k, new_v = [], []
    for l in range(depth):
        lam_init = 0.8 - 0.6 * math.exp(-0.3 * l)
        mod_ctx = mod_all[l, dec_batch:dec_batch + 1].reshape(1, 6, d)
        mod_lat = mod_all[l, :dec_batch].reshape(dec_batch, 6, d)
        xp, kf, vf = _trunk_layer(xp, mod_ctx, l, w, lam_init, alpha, None, None, True, ctx_tiles)
        new_k.append(kf.reshape(batch, seq, ATTN_HEADS, 2, HEAD_DIM))
        new_v.append(vf.reshape(batch, seq, ATTN_HEADS, V_DIM))
        xs, _, _ = _trunk_layer(xs, mod_lat, l, w, lam_init, alpha, rope, cache, False, lat_tiles)
    return (xp, xs, jnp.stack(new_k, axis=1), jnp.stack(new_v, axis=1))
```

```python
import functools
import math

import jax
import jax.numpy as jnp
from jax import lax
from jax.experimental import pallas as pl
from jax.experimental.pallas import tpu as pltpu

F32 = jnp.float32
BF16 = jnp.bfloat16

D_MODEL = 1024
GRID_W = 64
POOL_WIDTH = 256
POOL_WINDOWS = (2, 4, 8, 16)
POOL_GC = 64
ATTN_HEADS = 4
HEAD_DIM = 64
V_DIM = 128
ATTN_WIDTH = 512
QK_WIDTH = 512
CONV_WIDTH = 256
CONV_K = 31
IN_WIDTH = 2304
AXIS_DIM = 32
ROPE_THETA = 10000.0
N_GROUPS = 4
EXPERTS_PER_GROUP = 8
N_EXPERTS = 32
EXPERT_FF = 256
LN_EPS = 1e-5

O_A = 0
O_Q = POOL_WIDTH
O_K = O_Q + QK_WIDTH
O_V = O_K + QK_WIDTH
O_C = O_V + ATTN_WIDTH

TOP_K = 2
HALO = 16
LANES = 128
SUBLANES = 8
ROW_CHUNKS = D_MODEL // LANES
ROUTER_LANES = 128
ROUTE_BLOCK = 512
EXPERT_BLOCK = 128
NEG_BIG = -1e30
VMEM_LIMIT = 56 * 1024 * 1024


def _silu(x):
    return x * jax.nn.sigmoid(x)


def _layer_norm(x, g, b):
    mu = jnp.mean(x, axis=-1, keepdims=True)
    xc = x - mu
    var = jnp.mean(xc * xc, axis=-1, keepdims=True)
    return xc * lax.rsqrt(var + LN_EPS) * g + b


def _dot(a, b):
    return jnp.dot(a, b, preferred_element_type=F32)


def _params(sem):
    return pltpu.CompilerParams(dimension_semantics=sem, vmem_limit_bytes=VMEM_LIMIT)


def _mod_kernel(c_ref, w_ref, b_ref, o_ref):
    s = _silu(c_ref[...])
    o_ref[0] = _dot(s.astype(BF16), w_ref[0].astype(BF16)) + b_ref[0]


def _mod_call(cvec, w_mod, b_mod):
    depth, d, n6 = w_mod.shape
    rows = cvec.shape[0]
    nb = 1536
    return pl.pallas_call(
        _mod_kernel,
        grid=(depth, n6 // nb),
        in_specs=[
            pl.BlockSpec((rows, d), lambda l, j: (0, 0)),
            pl.BlockSpec((1, d, nb), lambda l, j: (l, 0, j)),
            pl.BlockSpec((1, 1, nb), lambda l, j: (l, 0, j)),
        ],
        out_specs=pl.BlockSpec((1, rows, nb), lambda l, j: (l, 0, j)),
        out_shape=jax.ShapeDtypeStruct((depth, rows, n6), F32),
        compiler_params=_params(("parallel", "parallel")),
        name="mod",
    )(cvec, w_mod, b_mod.reshape(depth, 1, n6))


def _front_kernel(*refs, seq_len, tile, has_rope, want_f32_kv):
    it = iter(refs)
    x_ref, xp_ref, xn_ref, mod_ref, win_ref = next(it), next(it), next(it), next(it), next(it)
    poolw_ref, pscale_ref, dw_ref, cg_ref, cb_ref, pw_ref = (next(it) for _ in range(6))
    if has_rope:
        cos_ref, sin_ref = next(it), next(it)
    q_ref, k_ref, v_ref = next(it), next(it), next(it)
    if want_f32_kv:
        kf_ref, vf_ref = next(it), next(it)
    yp_ref, yc_ref = next(it), next(it)

    i = pl.program_id(1)
    n_ext = tile + 2 * HALO
    shift = mod_ref[0, 0:1, :]
    scale = mod_ref[0, 1:2, :]
    xe = jnp.concatenate([xp_ref[0], x_ref[0], xn_ref[0]], axis=0)
    u = (xe * (1.0 + scale) + shift).astype(BF16)
    proj = _dot(u, win_ref[0])

    row = lax.broadcasted_iota(jnp.int32, (n_ext, 1), 0)
    gpos = i * tile - HALO + row
    valid = (gpos >= 0) & (gpos < seq_len)

    a_ext = jnp.where(valid, proj[:, O_A:O_A + POOL_WIDTH], 0.0)

    def sh(z, s):
        return pltpu.roll(z, s % n_ext, axis=0)

    p2 = a_ext + sh(a_ext, 1)
    p4 = sh(p2, 1) + sh(p2, -1)
    p8 = sh(p4, 2) + sh(p4, -2)
    p16 = sh(p8, 4) + sh(p8, -4)
    t = gpos[HALO:HALO + tile]
    lane = lax.broadcasted_iota(jnp.int32, (1, POOL_WIDTH), 1)
    mean = None
    for g, (w, p) in enumerate(zip(POOL_WINDOWS, (p2, p4, p8, p16))):
        cnt = jnp.minimum(t + w // 2, seq_len) - jnp.maximum(t - w // 2, 0)
        m = p[HALO:HALO + tile] * (1.0 / cnt.astype(F32))
        mean = m if mean is None else jnp.where(lane >= g * POOL_GC, m, mean)
    pooled = mean - a_ext[HALO:HALO + tile]
    y_pool = _dot(pooled.astype(BF16), poolw_ref[0]) * pscale_ref[0]
    yp_ref[0] = y_pool.astype(BF16)

    ca = proj[:, O_C:O_C + CONV_WIDTH]
    cb = proj[:, O_C + CONV_WIDTH:O_C + 2 * CONV_WIDTH]
    h = jnp.where(valid, ca * jax.nn.sigmoid(cb), 0.0)
    acc = None
    for j in range(CONV_K):
        off = HALO - CONV_K // 2 + j
        term = h[off:off + tile] * dw_ref[0, j:j + 1, :]
        acc = term if acc is None else acc + term
    hn = _silu(_layer_norm(acc, cg_ref[0], cb_ref[0]))
    yc_ref[0] = _dot(hn.astype(BF16), pw_ref[0]).astype(BF16)

    qf = proj[HALO:HALO + tile, O_Q:O_Q + QK_WIDTH]
    kf = proj[HALO:HALO + tile, O_K:O_K + QK_WIDTH]
    vf = proj[HALO:HALO + tile, O_V:O_V + ATTN_WIDTH]
    if want_f32_kv:
        kf_ref[0] = kf
        vf_ref[0] = vf
    if has_rope:
        cos = cos_ref[...]
        sin = sin_ref[...]
        l128 = lax.broadcasted_iota(jnp.int32, (1, LANES), 1)
        first_half = (l128 & (AXIS_DIM - 1)) < (AXIS_DIM // 2)

        def rope(z):
            outs = []
            for c in range(QK_WIDTH // LANES):
                zc = z[:, c * LANES:(c + 1) * LANES]
                partner = jnp.where(first_half,
                                    pltpu.roll(zc, LANES - AXIS_DIM // 2, axis=1),
                                    pltpu.roll(zc, AXIS_DIM // 2, axis=1))
                outs.append(zc * cos + partner * sin)
            return jnp.concatenate(outs, axis=1)

        qf = rope(qf)
        kf = rope(kf)
    q_ref[0] = qf.astype(BF16)
    k_ref[0] = kf.astype(BF16)
    v_ref[0] = vf.astype(BF16)


def _front_call(x, mod, layer, w, rope, want_f32_kv, tile):
    b, seq_len, d = x.shape
    nt = seq_len // tile
    hb = tile // HALO
    n_hblk = seq_len // HALO
    mod_b = (lambda bi, i: (bi, 0, 0)) if mod.shape[0] == b else (lambda bi, i: (0, 0, 0))
    lw = lambda bi, i: (layer, 0, 0)
    in_specs = [
        pl.BlockSpec((1, tile, d), lambda bi, i: (bi, i, 0)),
        pl.BlockSpec((1, HALO, d), lambda bi, i: (bi, jnp.maximum(i * hb - 1, 0), 0)),
        pl.BlockSpec((1, HALO, d), lambda bi, i: (bi, jnp.minimum((i + 1) * hb, n_hblk - 1), 0)),
        pl.BlockSpec((1, 6, d), mod_b),
        pl.BlockSpec((1, d, IN_WIDTH), lw),
        pl.BlockSpec((1, POOL_WIDTH, POOL_WIDTH), lw),
        pl.BlockSpec((1, 1, POOL_WIDTH), lw),
        pl.BlockSpec((1, CONV_K, CONV_WIDTH), lw),
        pl.BlockSpec((1, 1, CONV_WIDTH), lw),
        pl.BlockSpec((1, 1, CONV_WIDTH), lw),
        pl.BlockSpec((1, CONV_WIDTH, CONV_WIDTH), lw),
    ]
    args = [x, x, x, mod, w["w_in"], w["pool_bd"], w["pool_scale"], w["conv_dw"], w["conv_ln_g"],
            w["conv_ln_b"], w["conv_pw"]]
    if rope is not None:
        in_specs += [pl.BlockSpec((tile, LANES), lambda bi, i: (i, 0))] * 2
        args += list(rope)
    tok = lambda width: pl.BlockSpec((1, tile, width), lambda bi, i: (bi, i, 0))
    out_specs = [tok(QK_WIDTH), tok(QK_WIDTH), tok(ATTN_WIDTH)]
    out_shape = [jax.ShapeDtypeStruct((b, seq_len, QK_WIDTH), BF16)] * 3
    if want_f32_kv:
        out_specs += [tok(QK_WIDTH), tok(ATTN_WIDTH)]
        out_shape += [jax.ShapeDtypeStruct((b, seq_len, QK_WIDTH), F32)] * 2
    out_specs += [tok(POOL_WIDTH), tok(CONV_WIDTH)]
    out_shape += [jax.ShapeDtypeStruct((b, seq_len, POOL_WIDTH), BF16)] * 2
    kern = functools.partial(_front_kernel, seq_len=seq_len, tile=tile, has_rope=rope is not None,
                             want_f32_kv=want_f32_kv)
    return pl.pallas_call(
        kern, grid=(b, nt), in_specs=in_specs, out_specs=out_specs, out_shape=out_shape,
        compiler_params=_params(("parallel", "parallel")), name="front",
    )(*args)


def _attn_kernel(*refs, lam_init, has_cache):
    if has_cache:
        lam_ref, g_ref, q_ref, k_ref, v_ref, ck_ref, cv_ref, o_ref = refs
    else:
        lam_ref, g_ref, q_ref, k_ref, v_ref, o_ref = refs
    lp = lam_ref[0]
    lam = (jnp.exp(jnp.sum(lp[0:1] * lp[1:2], axis=-1, keepdims=True))
           - jnp.exp(jnp.sum(lp[2:3] * lp[3:4], axis=-1, keepdims=True)) + lam_init)

    q = q_ref[0] * (1.0 / math.sqrt(HEAD_DIM))
    lane = lax.broadcasted_iota(jnp.int32, (1, LANES), 1)
    zero = jnp.zeros_like(q)
    q_maps = (jnp.where(lane < HEAD_DIM, q, zero), jnp.where(lane >= HEAD_DIM, q, zero))
    nt = (((1,), (1,)), ((), ()))
    keys = [k_ref[0]]
    vals = [v_ref[0]]
    if has_cache:
        keys.append(ck_ref[0, 0].astype(BF16))
        vals.append(cv_ref[0, 0].astype(BF16))

    probs = []
    for qm in q_maps:
        s = [lax.dot_general(qm, kk, nt, preferred_element_type=F32) for kk in keys]
        m = functools.reduce(jnp.maximum, [jnp.max(z, axis=-1, keepdims=True) for z in s])
        e = [jnp.exp(z - m) for z in s]
        den = functools.reduce(jnp.add, [jnp.sum(z, axis=-1, keepdims=True) for z in e])
        probs.append((e, 1.0 / den))
    (e1, r1), (e2, r2) = probs
    r2 = lam * r2
    o = None
    for z1, z2, vv in zip(e1, e2, vals):
        a = z1 * r1 - z2 * r2
        part = _dot(a.astype(BF16), vv)
        o = part if o is None else o + part
    o = o * lax.rsqrt(jnp.mean(o * o, axis=-1, keepdims=True) + LN_EPS) * g_ref[0] * (1.0 - lam_init)
    o_ref[0] = o.astype(BF16)


def _attn_call(q, k, v, cache, layer, w, lam_init, tq):
    b, seq_len, _ = q.shape
    lw = lambda bi, h, i: (layer, 0, 0)
    in_specs = [
        pl.BlockSpec((1, 4, HEAD_DIM), lw),
        pl.BlockSpec((1, 1, V_DIM), lw),
        pl.BlockSpec((1, tq, LANES), lambda bi, h, i: (bi, i, h)),
        pl.BlockSpec((1, seq_len, LANES), lambda bi, h, i: (bi, 0, h)),
        pl.BlockSpec((1, seq_len, V_DIM), lambda bi, h, i: (bi, 0, h)),
    ]
    args = [w["diff_lambda"], w["subln_g"], q, k, v]
    if cache is not None:
        past = cache[0].shape[2]
        cspec = pl.BlockSpec((1, 1, past, LANES), lambda bi, h, i: (bi, layer, 0, h))
        in_specs += [cspec, cspec]
        args += list(cache)
    kern = functools.partial(_attn_kernel, lam_init=lam_init, has_cache=cache is not None)
    return pl.pallas_call(
        kern, grid=(b, ATTN_HEADS, seq_len // tq), in_specs=in_specs,
        out_specs=pl.BlockSpec((1, tq, V_DIM), lambda bi, h, i: (bi, i, h)),
        out_shape=jax.ShapeDtypeStruct((b, seq_len, ATTN_WIDTH), BF16),
        compiler_params=_params(("parallel", "parallel", "parallel")), name="attn",
    )(*args)


def _post_kernel(yp_ref, ya_ref, yc_ref, x_ref, mod_ref, wout_ref, g_ref, b_ref, wr_ref, br_ref,
                 x1_ref, u2c_ref, sel_ref, *, alpha):
    o1 = POOL_WIDTH
    o2 = o1 + ATTN_WIDTH
    y = (_dot(yp_ref[0], wout_ref[0, 0:o1, :]) + _dot(ya_ref[0], wout_ref[0, o1:o2, :])
         + _dot(yc_ref[0], wout_ref[0, o2:, :]))
    gate1 = mod_ref[0, 2:3, :]
    shift2 = mod_ref[0, 3:4, :]
    scale2 = mod_ref[0, 4:5, :]
    x1 = _layer_norm(alpha * x_ref[0] + gate1 * y, g_ref[0], b_ref[0])
    x1_ref[0] = x1
    u2 = x1 * (1.0 + scale2) + shift2
    tile = u2.shape[0]
    for c in range(ROW_CHUNKS):
        u2c_ref[:, c, :, :] = u2[:, c * LANES:(c + 1) * LANES].reshape(tile // SUBLANES, SUBLANES, LANES)

    logits =jnp.dot(u2, wr_ref[0], preferred_element_type=F32,
                     precision=lax.Precision.HIGHEST) + br_ref[0]
    lane = lax.broadcasted_iota(jnp.int32, (1, ROUTER_LANES), 1)
    lane_f = lane.astype(F32)
    is_g = (lane >= N_EXPERTS) & (lane < N_EXPERTS + N_GROUPS)
    rmax = lambda z: jnp.max(z, axis=-1, keepdims=True)
    rmin = lambda z: jnp.min(z, axis=-1, keepdims=True)
    gl = jnp.where(is_g, logits, NEG_BIG)
    ge = jnp.where(is_g, jnp.exp(gl - rmax(gl)), 0.0)
    gp = ge / jnp.sum(ge, axis=-1, keepdims=True)
    g_w = rmax(gp)
    g_idx = rmin(jnp.where(is_g & (gp == g_w), lane_f - N_EXPERTS, 1e9))
    in_grp = (lane < N_EXPERTS) & (jnp.right_shift(lane, 3).astype(F32) == g_idx)
    el = jnp.where(in_grp, logits, NEG_BIG)
    v1 = rmax(el)
    i1 = rmin(jnp.where(in_grp & (el == v1), lane_f, 1e9))
    rest = in_grp & (lane_f != i1)
    el2 = jnp.where(rest, logits, NEG_BIG)
    v2 = rmax(el2)
    i2 = rmin(jnp.where(rest & (el2 == v2), lane_f, 1e9))
    tt = jnp.exp(v2 - v1)
    w1 = 1.0 / (1.0 + tt)
    w2 = tt / (1.0 + tt)
    sel = jnp.zeros((tile, ROUTER_LANES), F32)
    for j, col in enumerate((i1, i2, g_w * w1, g_w * w2)):
        sel = jnp.where(lane == j, col, sel)
    sel_ref[0] = sel


def _post_call(y_pool, y_attn, y_conv, x, mod, layer, w, alpha, tile):
    b, seq_len, d = x.shape
    nt = seq_len // tile
    mod_b = (lambda bi, i: (bi, 0, 0)) if mod.shape[0] == b else (lambda bi, i: (0, 0, 0))
    lw = lambda bi, i: (layer, 0, 0)
    tok = lambda width: pl.BlockSpec((1, tile, width), lambda bi, i: (bi, i, 0))
    in_specs = [tok(POOL_WIDTH), tok(ATTN_WIDTH), tok(CONV_WIDTH), tok(d),
                pl.BlockSpec((1, 6, d), mod_b),
                pl.BlockSpec((1, d, d), lw),
                pl.BlockSpec((1, 1, d), lw), pl.BlockSpec((1, 1, d), lw),
                pl.BlockSpec((1, d, ROUTER_LANES), lw), pl.BlockSpec((1, 1, ROUTER_LANES), lw)]
    out_specs = [tok(d),
                 pl.BlockSpec((tile // SUBLANES, ROW_CHUNKS, SUBLANES, LANES), lambda bi, i: (bi * nt + i, 0, 0, 0)),
                 tok(ROUTER_LANES)]
    out_shape = [jax.ShapeDtypeStruct((b, seq_len, d), F32),
                 jax.ShapeDtypeStruct((b * seq_len // SUBLANES, ROW_CHUNKS, SUBLANES, LANES), F32),
                 jax.ShapeDtypeStruct((b, seq_len, ROUTER_LANES), F32)]
    return pl.pallas_call(
        functools.partial(_post_kernel, alpha=alpha),
        grid=(b, nt), in_specs=in_specs, out_specs=out_specs, out_shape=out_shape,
        compiler_params=_params(("parallel", "parallel")), name="post",
    )(y_pool, y_attn, y_conv, x, mod, w["w_out"], w["ln1_g"], w["ln1_b"], w["router_w"], w["router_b"])


def _route_kernel(sel_ref, ri_ref, rw_ref, seg_ref, *, tile):
    lane = lax.broadcasted_iota(jnp.int32, (1, ROUTER_LANES), 1)
    lane_f = lane.astype(F32)
    blk = ROUTE_BLOCK
    r_io = lax.broadcasted_iota(jnp.int32, (blk, blk), 0)
    c_io = lax.broadcasted_iota(jnp.int32, (blk, blk), 1)
    before = jnp.where(r_io > c_io, 1.0, 0.0).astype(BF16)

    def picks(j):
        s = sel_ref[j * blk:(j + 1) * blk, :]
        return s, lane_f == s[:, 0:1], lane_f == s[:, 1:2]

    carry = jnp.zeros((1, ROUTER_LANES), F32)
    prior = []
    for j in range(tile // blk):
        _, oh1, oh2 = picks(j)
        member = jnp.where(oh1 | oh2, 1.0, 0.0)
        prior.append(_dot(before, member.astype(BF16)) + carry)
        carry = carry + jnp.sum(member, axis=0, keepdims=True)

    cnt = jnp.broadcast_to(carry, (SUBLANES, ROUTER_LANES)).astype(jnp.int32)
    padded = jnp.where(lane < N_EXPERTS, jnp.right_shift(cnt + (SUBLANES - 1), 3) * SUBLANES, 0)
    incl = padded
    step = 1
    while step < N_EXPERTS:
        incl = incl + jnp.where(lane >= step, pltpu.roll(incl, step, axis=1), 0)
        step *= 2
    start = incl - padded
    seg_ref[0] = jnp.where(lane < N_EXPERTS, cnt, pltpu.roll(start, N_EXPERTS, axis=1))
    start_f = start[0:1, :].astype(F32)

    for j in range(tile // blk):
        s, oh1, oh2 = picks(j)
        cols = []
        for oh in (oh1, oh2):
            p = jnp.sum(jnp.where(oh, start_f + prior[j], 0.0), axis=-1, keepdims=True).astype(jnp.int32)
            addr = jnp.right_shift(p, 3) * (SUBLANES * ROW_CHUNKS) + jnp.bitwise_and(p, SUBLANES - 1)
            cols.append(addr.astype(F32))
        z = jnp.where(lane == 0, cols[0], jnp.where(lane == 1, cols[1], s))
        zt = z.T
        ri_ref[0, :, j * blk:(j + 1) * blk] = zt[0:2, :].astype(jnp.int32)
        rw_ref[0, :, j * blk:(j + 1) * blk] = zt[2:4, :]


def _route_call(sel, tile):
    n_tok = sel.shape[0]
    n_tiles = n_tok // tile
    return pl.pallas_call(
        functools.partial(_route_kernel, tile=tile),
        grid=(n_tiles,),
        in_specs=[pl.BlockSpec((tile, ROUTER_LANES), lambda t: (t, 0))],
        out_specs=[pl.BlockSpec((1, TOP_K, tile), lambda t: (t, 0, 0)),
                   pl.BlockSpec((1, TOP_K, tile), lambda t: (t, 0, 0)),
                   pl.BlockSpec((1, SUBLANES, ROUTER_LANES), lambda t: (t, 0, 0))],
        out_shape=[jax.ShapeDtypeStruct((n_tiles, TOP_K, tile), jnp.int32),
                   jax.ShapeDtypeStruct((n_tiles, TOP_K, tile), F32),
                   jax.ShapeDtypeStruct((n_tiles, SUBLANES, ROUTER_LANES), jnp.int32)],
        compiler_params=_params(("parallel",)), name="route",
    )(sel)


def _row_at(start):
    return pl.ds(start, ROW_CHUNKS, stride=SUBLANES)


def _moe_kernel(ri_ref, rw_ref, seg_ref, u2c_ref, wg_ref, wu_ref, wd_ref, o_ref, sorted_ref, *, tile):
    e = pl.program_id(1)
    groups = tile // SUBLANES
    group_rows = SUBLANES * ROW_CHUNKS

    @pl.when((pl.program_id(0) == 0) & (e == 0))
    def _():
        sorted_ref[...] = jnp.zeros_like(sorted_ref)

    @pl.when(e == 0)
    def _():
        def disp_body(a, carry):
            for s in range(SUBLANES):
                row = u2c_ref[_row_at(a * group_rows + s), :]
                for k in range(TOP_K):
                    sorted_ref[_row_at(ri_ref[k * tile + a * SUBLANES + s]), :] = row
            return carry
        lax.fori_loop(0, groups, disp_body, 0)

    cnt = seg_ref[0, 0, e]
    base = seg_ref[0, 0, N_EXPERTS + e]
    blk_groups = EXPERT_BLOCK // SUBLANES
    row_id = lax.broadcasted_iota(jnp.int32, (SUBLANES, 1), 0)

    def tile_at(start, g, c):
        return pl.ds(pl.multiple_of(start + (g * ROW_CHUNKS + c) * SUBLANES, SUBLANES), SUBLANES)

    def blk_body(bi, carry):
        start = (base + bi * EXPERT_BLOCK) * ROW_CHUNKS
        x = jnp.concatenate(
            [jnp.concatenate([sorted_ref[tile_at(start, g, c), :] for g in range(blk_groups)], axis=0)
             for c in range(ROW_CHUNKS)], axis=1).astype(BF16)
        h = _silu(_dot(x, wg_ref[0, 0])) * _dot(x, wu_ref[0, 0])
        y = _dot(h.astype(BF16), wd_ref[0, 0])
        n_valid = cnt - bi * EXPERT_BLOCK
        for g in range(blk_groups):
            keep = (row_id + g * SUBLANES) < n_valid
            for c in range(ROW_CHUNKS):
                sl = tile_at(start, g, c)
                piece = y[g * SUBLANES:(g + 1) * SUBLANES, c * LANES:(c + 1) * LANES]
                sorted_ref[sl, :] = jnp.where(keep, piece, sorted_ref[sl, :])
        return carry
    lax.fori_loop(0, (cnt + EXPERT_BLOCK - 1) // EXPERT_BLOCK, blk_body, 0)

    @pl.when(e == N_EXPERTS - 1)
    def _():
        def comb_body(a, carry):
            for s in range(SUBLANES):
                n = a * SUBLANES + s
                acc = None
                for k in range(TOP_K):
                    term = sorted_ref[_row_at(ri_ref[k * tile + n]), :] * rw_ref[k * tile + n]
                    acc = term if acc is None else acc + term
                o_ref[_row_at(a * group_rows + s), :] = acc
            return carry
        lax.fori_loop(0, groups, comb_body, 0)


def _moe_call(u2c, ri, rw, seg, layer, w, tile):
    n_tok = u2c.shape[0] // ROW_CHUNKS
    n_tiles = n_tok // tile
    d = D_MODEL
    sorted_rows = tile * TOP_K + N_EXPERTS * (SUBLANES - 1) + EXPERT_BLOCK
    sorted_rows = -(-sorted_rows // SUBLANES) * SUBLANES
    smem = lambda shape, imap: pl.BlockSpec(shape, imap, memory_space=pltpu.SMEM)
    in_specs = [smem((TOP_K * tile,), lambda t, e: (t,)),
                smem((TOP_K * tile,), lambda t, e: (t,)),
                smem((1, SUBLANES, ROUTER_LANES), lambda t, e: (t, 0, 0)),
                pl.BlockSpec((tile * ROW_CHUNKS, LANES), lambda t, e: (t, 0), pipeline_mode=pl.Buffered(1)),
                pl.BlockSpec((1, 1, d, EXPERT_FF), lambda t, e: (layer, e, 0, 0)),
                pl.BlockSpec((1, 1, d, EXPERT_FF), lambda t, e: (layer, e, 0, 0)),
                pl.BlockSpec((1, 1, EXPERT_FF, d), lambda t, e: (layer, e, 0, 0))]
    return pl.pallas_call(
        functools.partial(_moe_kernel, tile=tile),
        grid=(n_tiles, N_EXPERTS), in_specs=in_specs,
        out_specs=pl.BlockSpec((tile * ROW_CHUNKS, LANES), lambda t, e: (t, 0)),
        out_shape=jax.ShapeDtypeStruct((n_tok * ROW_CHUNKS, LANES), F32),
        scratch_shapes=[pltpu.VMEM((sorted_rows * ROW_CHUNKS, LANES), F32)],
        compiler_params=_params(("arbitrary", "arbitrary")), name="moe",
    )(ri.reshape(-1), rw.reshape(-1), seg, u2c, w["w_gate"], w["w_up"], w["w_down"])


def _ln2_kernel(m_ref, x1_ref, mod_ref, g_ref, b_ref, o_ref, *, alpha):
    tile = x1_ref.shape[0]
    moe = jnp.concatenate([m_ref[:, c, :, :].reshape(tile, LANES) for c in range(ROW_CHUNKS)], axis=1)
    gate2 = mod_ref[0, 5:6, :]
    o_ref[...] = _layer_norm(alpha * x1_ref[...] + gate2 * moe, g_ref[0], b_ref[0])


def _ln2_call(moe_c, x1, mod, layer, w, alpha, tile):
    b, seq_len, d = x1.shape
    n_tok = b * seq_len
    per_seq = seq_len // tile
    mod_b = (lambda t: (t // per_seq, 0, 0)) if mod.shape[0] == b else (lambda t: (0, 0, 0))
    lw = lambda t: (layer, 0, 0)
    out = pl.pallas_call(
        functools.partial(_ln2_kernel, alpha=alpha),
        grid=(n_tok // tile,),
        in_specs=[pl.BlockSpec((tile // SUBLANES, ROW_CHUNKS, SUBLANES, LANES), lambda t: (t, 0, 0, 0)),
                  pl.BlockSpec((tile, d), lambda t: (t, 0)),
                  pl.BlockSpec((1, 6, d), mod_b),
                  pl.BlockSpec((1, 1, d), lw), pl.BlockSpec((1, 1, d), lw)],
        out_specs=pl.BlockSpec((tile, d), lambda t: (t, 0)),
        out_shape=jax.ShapeDtypeStruct((n_tok, d), F32),
        compiler_params=_params(("parallel",)), name="ln2",
    )(moe_c.reshape(n_tok // SUBLANES, ROW_CHUNKS, SUBLANES, LANES), x1.reshape(n_tok, d), mod,
      w["ln2_g"], w["ln2_b"])
    return out.reshape(b, seq_len, d)


def _rope_tables(seq_len):
    rows = seq_len // GRID_W
    row_id = jnp.repeat(jnp.arange(rows), GRID_W).astype(F32)
    col_id = jnp.tile(jnp.arange(GRID_W), rows).astype(F32)
    inv = ROPE_THETA ** (-jnp.arange(0, AXIS_DIM, 2, dtype=F32) / AXIS_DIM)
    ang = jnp.stack([row_id[:, None] * inv, col_id[:, None] * inv], axis=1)
    cos = jnp.cos(ang)[:, :, None, :]
    sin = jnp.sin(ang)[:, :, None, :]
    shape = (seq_len, 2, 2, AXIS_DIM // 2)
    cos = jnp.broadcast_to(cos, shape).reshape(seq_len, HEAD_DIM)
    sin = (jnp.broadcast_to(sin, shape) * jnp.array([-1.0, 1.0], F32)[None, None, :, None]).reshape(seq_len, HEAD_DIM)
    return jnp.tile(cos, (1, LANES // HEAD_DIM)), jnp.tile(sin, (1, LANES // HEAD_DIM))


def _trunk_layer(x, mod, layer, w, lam_init, alpha, rope, cache, want_f32_kv, tiles):
    outs = _front_call(x, mod, layer, w, rope, want_f32_kv, tiles["front"])
    if want_f32_kv:
        q, k, v, kf, vf, y_pool, y_conv = outs
    else:
        q, k, v, y_pool, y_conv = outs
        kf = vf = None
    y_attn = _attn_call(q, k, v, cache, layer, w, lam_init, tiles["attn"])
    x1, u2c, sel = _post_call(y_pool, y_attn, y_conv, x, mod, layer, w, alpha, tiles["post"])
    ri, rw, seg = _route_call(sel.reshape(-1, ROUTER_LANES), tiles["moe"])
    moe_c = _moe_call(u2c.reshape(-1, LANES), ri, rw, seg, layer, w, tiles["moe"])
    x2 = _ln2_call(moe_c, x1, mod, layer, w, alpha, tiles["ln2"])
    return x2, kf, vf


def kernel(x_prompt, x_sample, cache_k, cache_v, c, c_ctx, w_mod, b_mod, w_in, w_out, pool_w, pool_scale, diff_lambda, subln_g, conv_dw, conv_ln_g, conv_ln_b, conv_pw, ln1_g, ln1_b, ln2_g, ln2_b, router_grp, router_grp_b, router_exp, router_exp_b, w_gate, w_up, w_down):
    depth = w_in.shape[0]
    batch, seq, d = x_prompt.shape
    dec_batch, dec_seq, _ = x_sample.shape
    past = cache_k.shape[2]
    alpha = (2 * depth) ** 0.25

    mod_rows = 16
    cvec = jnp.concatenate([c, c_ctx[None], jnp.zeros((mod_rows - dec_batch - 1, d), F32)], axis=0)
    mod_all = _mod_call(cvec, w_mod, b_mod)

    eye = jnp.eye(len(POOL_WINDOWS), dtype=F32)
    pool_bd = (pool_w[:, :, :, None, :] * eye[None, :, None, :, None]).reshape(depth, POOL_WIDTH, POOL_WIDTH)
    router_w = jnp.concatenate(
        [router_exp, router_grp, jnp.zeros((depth, d, ROUTER_LANES - N_EXPERTS - N_GROUPS), F32)], axis=-1)
    router_b = jnp.concatenate(
        [router_exp_b, router_grp_b, jnp.zeros((depth, ROUTER_LANES - N_EXPERTS - N_GROUPS), F32)], axis=-1)
    r3 = lambda z: z.reshape(depth, 1, z.shape[-1])
    w = {
        "w_in": w_in.astype(BF16), "w_out": w_out.astype(BF16), "pool_bd": pool_bd.astype(BF16),
        "pool_scale": r3(pool_scale), "diff_lambda": diff_lambda, "subln_g": r3(subln_g),
        "conv_dw": conv_dw, "conv_ln_g": r3(conv_ln_g), "conv_ln_b": r3(conv_ln_b),
        "conv_pw": conv_pw.astype(BF16), "ln1_g": r3(ln1_g), "ln1_b": r3(ln1_b), "ln2_g": r3(ln2_g),
        "ln2_b": r3(ln2_b), "router_w": router_w, "router_b": r3(router_b),
        "w_gate": w_gate.astype(BF16), "w_up": w_up.astype(BF16), "w_down": w_down.astype(BF16),
    }
    rope = _rope_tables(dec_seq)
    cache = (cache_k.reshape(dec_batch, depth, past, QK_WIDTH), cache_v.reshape(dec_batch, depth, past, ATTN_WIDTH))
    ctx_tiles = {"front": seq, "attn": seq, "post": seq, "moe": 2048, "ln2": seq}
    lat_tiles = {"front": 512, "attn": 256, "post": 512, "moe": 2048, "ln2": 512}

    xp, xs = x_prompt, x_sample
    new_k, new_v = [], []
    for l in range(depth):
        lam_init = 0.8 - 0.6 * math.exp(-0.3 * l)
        mod_ctx = mod_all[l, dec_batch:dec_batch + 1].reshape(1, 6, d)
        mod_lat = mod_all[l, :dec_batch].reshape(dec_batch, 6, d)
        xp, kf, vf = _trunk_layer(xp, mod_ctx, l, w, lam_init, alpha, None, None, True, ctx_tiles)
        new_k.append(kf.reshape(batch, seq, ATTN_HEADS, 2, HEAD_DIM))
        new_v.append(vf.reshape(batch, seq, ATTN_HEADS, V_DIM))
        xs, _, _ = _trunk_layer(xs, mod_lat, l, w, lam_init, alpha, rope, cache, False, lat_tiles)
    return (xp, xs, jnp.stack(new_k, axis=1), jnp.stack(new_v, axis=1))
```

```python
import functools
import math

import jax
import jax.numpy as jnp
from jax import lax
from jax.experimental import pallas as pl
from jax.experimental.pallas import tpu as pltpu

F32 = jnp.float32
BF16 = jnp.bfloat16

D_MODEL = 1024
GRID_W = 64
POOL_WIDTH = 256
POOL_WINDOWS = (2, 4, 8, 16)
POOL_GC = 64
ATTN_HEADS = 4
HEAD_DIM = 64
V_DIM = 128
ATTN_WIDTH = 512
QK_WIDTH = 512
CONV_WIDTH = 256
CONV_K = 31
IN_WIDTH = 2304
AXIS_DIM = 32
ROPE_THETA = 10000.0
N_GROUPS = 4
EXPERTS_PER_GROUP = 8
N_EXPERTS = 32
EXPERT_FF = 256
LN_EPS = 1e-5

O_A = 0
O_Q = POOL_WIDTH
O_K = O_Q + QK_WIDTH
O_V = O_K + QK_WIDTH
O_C = O_V + ATTN_WIDTH

TOP_K = 2
HALO = 16
LANES = 128
SUBLANES = 8
ROW_CHUNKS = D_MODEL // LANES
ROUTER_LANES = 128
ROUTE_BLOCK = 512
EXPERT_BLOCK = 128
ATTN_ROWS = 128
LOG2E = 1.4426950408889634
NEG_BIG = -1e30
VMEM_LIMIT = 56 * 1024 * 1024


def _silu(x):
    return x * jax.nn.sigmoid(x)


def _layer_norm(x, g, b):
    mu = jnp.mean(x, axis=-1, keepdims=True)
    xc = x - mu
    var = jnp.mean(xc * xc, axis=-1, keepdims=True)
    return xc * lax.rsqrt(var + LN_EPS) * g + b


def _dot(a, b):
    return jnp.dot(a, b, preferred_element_type=F32)


def _params(sem):
    return pltpu.CompilerParams(dimension_semantics=sem, vmem_limit_bytes=VMEM_LIMIT)


def _mod_kernel(c_ref, w_ref, b_ref, o_ref):
    s = _silu(c_ref[...])
    o_ref[0] = _dot(s.astype(BF16), w_ref[0].astype(BF16)) + b_ref[0]


def _mod_call(cvec, w_mod, b_mod):
    depth, d, n6 = w_mod.shape
    rows = cvec.shape[0]
    nb = 1536
    return pl.pallas_call(
        _mod_kernel,
        grid=(depth, n6 // nb),
        in_specs=[
            pl.BlockSpec((rows, d), lambda l, j: (0, 0)),
            pl.BlockSpec((1, d, nb), lambda l, j: (l, 0, j)),
            pl.BlockSpec((1, 1, nb), lambda l, j: (l, 0, j)),
        ],
        out_specs=pl.BlockSpec((1, rows, nb), lambda l, j: (l, 0, j)),
        out_shape=jax.ShapeDtypeStruct((depth, rows, n6), F32),
        compiler_params=_params(("parallel", "parallel")),
        name="mod",
    )(cvec, w_mod, b_mod.reshape(depth, 1, n6))


def _front_kernel(*refs, seq_len, tile, has_rope, want_f32_kv):
    it = iter(refs)
    x_ref, xp_ref, xn_ref, mod_ref, win_ref = next(it), next(it), next(it), next(it), next(it)
    poolw_ref, pscale_ref, dw_ref, cg_ref, cb_ref, pw_ref = (next(it) for _ in range(6))
    if has_rope:
        cos_ref, sin_ref = next(it), next(it)
    q_ref, k_ref, v_ref = next(it), next(it), next(it)
    if want_f32_kv:
        kf_ref, vf_ref = next(it), next(it)
    yp_ref, yc_ref = next(it), next(it)

    i = pl.program_id(1)
    n_ext = tile + 2 * HALO
    shift = mod_ref[0, 0:1, :]
    scale = mod_ref[0, 1:2, :]
    xe = jnp.concatenate([xp_ref[0], x_ref[0], xn_ref[0]], axis=0)
    u = (xe * (1.0 + scale) + shift).astype(BF16)
    proj = _dot(u, win_ref[0])

    row = lax.broadcasted_iota(jnp.int32, (n_ext, 1), 0)
    gpos = i * tile - HALO + row
    valid = (gpos >= 0) & (gpos < seq_len)

    a_ext = jnp.where(valid, proj[:, O_A:O_A + POOL_WIDTH], 0.0)

    def sh(z, s):
        return pltpu.roll(z, s % n_ext, axis=0)

    p2 = a_ext + sh(a_ext, 1)
    p4 = sh(p2, 1) + sh(p2, -1)
    p8 = sh(p4, 2) + sh(p4, -2)
    p16 = sh(p8, 4) + sh(p8, -4)
    t = gpos[HALO:HALO + tile]
    lane = lax.broadcasted_iota(jnp.int32, (1, POOL_WIDTH), 1)
    mean = None
    for g, (w, p) in enumerate(zip(POOL_WINDOWS, (p2, p4, p8, p16))):
        cnt = jnp.minimum(t + w // 2, seq_len) - jnp.maximum(t - w // 2, 0)
        m = p[HALO:HALO + tile] * (1.0 / cnt.astype(F32))
        mean = m if mean is None else jnp.where(lane >= g * POOL_GC, m, mean)
    pooled = mean - a_ext[HALO:HALO + tile]
    y_pool = _dot(pooled.astype(BF16), poolw_ref[0]) * pscale_ref[0]
    yp_ref[0] = y_pool.astype(BF16)

    ca = proj[:, O_C:O_C + CONV_WIDTH]
    cb = proj[:, O_C + CONV_WIDTH:O_C + 2 * CONV_WIDTH]
    h = jnp.where(valid, ca * jax.nn.sigmoid(cb), 0.0)
    acc = None
    for j in range(CONV_K):
        off = HALO - CONV_K // 2 + j
        term = h[off:off + tile] * dw_ref[0, j:j + 1, :]
        acc = term if acc is None else acc + term
    hn = _silu(_layer_norm(acc, cg_ref[0], cb_ref[0]))
    yc_ref[0] = _dot(hn.astype(BF16), pw_ref[0]).astype(BF16)

    qf = proj[HALO:HALO + tile, O_Q:O_Q + QK_WIDTH]
    kf = proj[HALO:HALO + tile, O_K:O_K + QK_WIDTH]
    vf = proj[HALO:HALO + tile, O_V:O_V + ATTN_WIDTH]
    if want_f32_kv:
        kf_ref[0] = kf
        vf_ref[0] = vf
    if has_rope:
        cos = cos_ref[...]
        sin = sin_ref[...]
        l128 = lax.broadcasted_iota(jnp.int32, (1, LANES), 1)
        first_half = (l128 & (AXIS_DIM - 1)) < (AXIS_DIM // 2)

        def rope(z):
            outs = []
            for c in range(QK_WIDTH // LANES):
                zc = z[:, c * LANES:(c + 1) * LANES]
                partner = jnp.where(first_half,
                                    pltpu.roll(zc, LANES - AXIS_DIM // 2, axis=1),
                                    pltpu.roll(zc, AXIS_DIM // 2, axis=1))
                outs.append(zc * cos + partner * sin)
            return jnp.concatenate(outs, axis=1)

        qf = rope(qf)
        kf = rope(kf)
    q_ref[0] = (qf * (LOG2E / math.sqrt(HEAD_DIM))).astype(BF16)
    k_ref[0] = kf.astype(BF16)
    v_ref[0] = vf.astype(BF16)


def _front_call(x, mod, layer, w, rope, want_f32_kv, tile):
    b, seq_len, d = x.shape
    nt = seq_len // tile
    hb = tile // HALO
    n_hblk = seq_len // HALO
    mod_b = (lambda bi, i: (bi, 0, 0)) if mod.shape[0] == b else (lambda bi, i: (0, 0, 0))
    lw = lambda bi, i: (layer, 0, 0)
    in_specs = [
        pl.BlockSpec((1, tile, d), lambda bi, i: (bi, i, 0)),
        pl.BlockSpec((1, HALO, d), lambda bi, i: (bi, jnp.maximum(i * hb - 1, 0), 0)),
        pl.BlockSpec((1, HALO, d), lambda bi, i: (bi, jnp.minimum((i + 1) * hb, n_hblk - 1), 0)),
        pl.BlockSpec((1, 6, d), mod_b),
        pl.BlockSpec((1, d, IN_WIDTH), lw),
        pl.BlockSpec((1, POOL_WIDTH, POOL_WIDTH), lw),
        pl.BlockSpec((1, 1, POOL_WIDTH), lw),
        pl.BlockSpec((1, CONV_K, CONV_WIDTH), lw),
        pl.BlockSpec((1, 1, CONV_WIDTH), lw),
        pl.BlockSpec((1, 1, CONV_WIDTH), lw),
        pl.BlockSpec((1, CONV_WIDTH, CONV_WIDTH), lw),
    ]
    args = [x, x, x, mod, w["w_in"], w["pool_bd"], w["pool_scale"], w["conv_dw"], w["conv_ln_g"],
            w["conv_ln_b"], w["conv_pw"]]
    if rope is not None:
        in_specs += [pl.BlockSpec((tile, LANES), lambda bi, i: (i, 0))] * 2
        args += list(rope)
    tok = lambda width: pl.BlockSpec((1, tile, width), lambda bi, i: (bi, i, 0))
    out_specs = [tok(QK_WIDTH), tok(QK_WIDTH), tok(ATTN_WIDTH)]
    out_shape = [jax.ShapeDtypeStruct((b, seq_len, QK_WIDTH), BF16)] * 3
    if want_f32_kv:
        out_specs += [tok(QK_WIDTH), tok(ATTN_WIDTH)]
        out_shape += [jax.ShapeDtypeStruct((b, seq_len, QK_WIDTH), F32)] * 2
    out_specs += [tok(POOL_WIDTH), tok(CONV_WIDTH)]
    out_shape += [jax.ShapeDtypeStruct((b, seq_len, POOL_WIDTH), BF16)] * 2
    kern = functools.partial(_front_kernel, seq_len=seq_len, tile=tile, has_rope=rope is not None,
                             want_f32_kv=want_f32_kv)
    return pl.pallas_call(
        kern, grid=(b, nt), in_specs=in_specs, out_specs=out_specs, out_shape=out_shape,
        compiler_params=_params(("parallel", "parallel")), name="front",
    )(*args)


def _attn_kernel(*refs, lam_init, has_cache):
    if has_cache:
        lam_ref, g_ref, q_ref, k_ref, v_ref, ck_ref, cv_ref, o_ref = refs
    else:
        lam_ref, g_ref, q_ref, k_ref, v_ref, o_ref = refs
    lp = lam_ref[0]
    lam = (jnp.exp(jnp.sum(lp[0:1] * lp[1:2], axis=-1, keepdims=True))
           - jnp.exp(jnp.sum(lp[2:3] * lp[3:4], axis=-1, keepdims=True)) + lam_init)

    lane = lax.broadcasted_iota(jnp.int32, (1, LANES), 1)
    nt = (((1,), (1,)), ((), ()))
    keys = [k_ref[0]]
    vals = [v_ref[0]]
    if has_cache:
        keys.append(ck_ref[0, 0].astype(BF16))
        vals.append(cv_ref[0, 0].astype(BF16))

    def scores(q):
        zero = jnp.zeros_like(q)
        return [[lax.dot_general(qm, kk, nt, preferred_element_type=F32) for kk in keys]
                for qm in (jnp.where(lane < HEAD_DIM, q, zero), jnp.where(lane >= HEAD_DIM, q, zero))]

    def finish(score_maps):
        parts = []
        for s in score_maps:
            m = functools.reduce(jnp.maximum, [jnp.max(z, axis=-1, keepdims=True) for z in s])
            e = [jnp.exp2(z - m) for z in s]
            den = functools.reduce(jnp.add, [jnp.sum(z, axis=-1, keepdims=True) for z in e])
            parts.append((e, den))
        (e1, den1), (e2, den2) = parts
        c = lam * den1 / den2
        o = None
        for z1, z2, vv in zip(e1, e2, vals):
            part = _dot((z1 - z2 * c).astype(BF16), vv)
            o = part if o is None else o + part
        o = o * (1.0 / den1)
        o = o * lax.rsqrt(jnp.mean(o * o, axis=-1, keepdims=True) + LN_EPS) * g_ref[0] * (1.0 - lam_init)
        return o.astype(BF16)

    tq = q_ref.shape[1]
    n_sub = max(1, tq // ATTN_ROWS)
    step = tq // n_sub
    pending = scores(q_ref[0, 0:step, :])
    for j in range(n_sub):
        current = pending
        if j + 1 < n_sub:
            pending = scores(q_ref[0, (j + 1) * step:(j + 2) * step, :])
        o_ref[0, j * step:(j + 1) * step, :] = finish(current)


def _attn_call(q, k, v, cache, layer, w, lam_init, tq):
    b, seq_len, _ = q.shape
    lw = lambda bi, h, i: (layer, 0, 0)
    in_specs = [
        pl.BlockSpec((1, 4, HEAD_DIM), lw),
        pl.BlockSpec((1, 1, V_DIM), lw),
        pl.BlockSpec((1, tq, LANES), lambda bi, h, i: (bi, i, h)),
        pl.BlockSpec((1, seq_len, LANES), lambda bi, h, i: (bi, 0, h)),
        pl.BlockSpec((1, seq_len, V_DIM), lambda bi, h, i: (bi, 0, h)),
    ]
    args = [w["diff_lambda"], w["subln_g"], q, k, v]
    if cache is not None:
        past = cache[0].shape[2]
        cspec = pl.BlockSpec((1, 1, past, LANES), lambda bi, h, i: (bi, layer, 0, h))
        in_specs += [cspec, cspec]
        args += list(cache)
    kern = functools.partial(_attn_kernel, lam_init=lam_init, has_cache=cache is not None)
    return pl.pallas_call(
        kern, grid=(b, ATTN_HEADS, seq_len // tq), in_specs=in_specs,
        out_specs=pl.BlockSpec((1, tq, V_DIM), lambda bi, h, i: (bi, i, h)),
        out_shape=jax.ShapeDtypeStruct((b, seq_len, ATTN_WIDTH), BF16),
        compiler_params=_params(("parallel", "parallel", "parallel")), name="attn",
    )(*args)


def _post_kernel(yp_ref, ya_ref, yc_ref, x_ref, mod_ref, wout_ref, g_ref, b_ref, wr_ref, br_ref,
                 x1_ref, u2c_ref, sel_ref, *, alpha):
    o1 = POOL_WIDTH
    o2 = o1 + ATTN_WIDTH
    y = (_dot(yp_ref[0], wout_ref[0, 0:o1, :]) + _dot(ya_ref[0], wout_ref[0, o1:o2, :])
         + _dot(yc_ref[0], wout_ref[0, o2:, :]))
    gate1 = mod_ref[0, 2:3, :]
    shift2 = mod_ref[0, 3:4, :]
    scale2 = mod_ref[0, 4:5, :]
    x1 = _layer_norm(alpha * x_ref[0] + gate1 * y, g_ref[0], b_ref[0])
    x1_ref[0] = x1
    u2 = x1 * (1.0 + scale2) + shift2
    tile = u2.shape[0]
    for c in range(ROW_CHUNKS):
        u2c_ref[:, c, :, :] = u2[:, c * LANES:(c + 1) * LANES].reshape(tile // SUBLANES, SUBLANES, LANES)

    logits =jnp.dot(u2, wr_ref[0], preferred_element_type=F32,
                     precision=lax.Precision.HIGHEST) + br_ref[0]
    lane = lax.broadcasted_iota(jnp.int32, (1, ROUTER_LANES), 1)
    lane_f = lane.astype(F32)
    is_g = (lane >= N_EXPERTS) & (lane < N_EXPERTS + N_GROUPS)
    rmax = lambda z: jnp.max(z, axis=-1, keepdims=True)
    rmin = lambda z: jnp.min(z, axis=-1, keepdims=True)
    gl = jnp.where(is_g, logits, NEG_BIG)
    ge = jnp.where(is_g, jnp.exp(gl - rmax(gl)), 0.0)
    gp = ge / jnp.sum(ge, axis=-1, keepdims=True)
    g_w = rmax(gp)
    g_idx = rmin(jnp.where(is_g & (gp == g_w), lane_f - N_EXPERTS, 1e9))
    in_grp = (lane < N_EXPERTS) & (jnp.right_shift(lane, 3).astype(F32) == g_idx)
    el = jnp.where(in_grp, logits, NEG_BIG)
    v1 = rmax(el)
    i1 = rmin(jnp.where(in_grp & (el == v1), lane_f, 1e9))
    rest = in_grp & (lane_f != i1)
    el2 = jnp.where(rest, logits, NEG_BIG)
    v2 = rmax(el2)
    i2 = rmin(jnp.where(rest & (el2 == v2), lane_f, 1e9))
    tt = jnp.exp(v2 - v1)
    w1 = 1.0 / (1.0 + tt)
    w2 = tt / (1.0 + tt)
    sel = jnp.zeros((tile, ROUTER_LANES), F32)
    for j, col in enumerate((i1, i2, g_w * w1, g_w * w2)):
        sel = jnp.where(lane == j, col, sel)
    sel_ref[0] = sel


def _post_call(y_pool, y_attn, y_conv, x, mod, layer, w, alpha, tile):
    b, seq_len, d = x.shape
    nt = seq_len // tile
    mod_b = (lambda bi, i: (bi, 0, 0)) if mod.shape[0] == b else (lambda bi, i: (0, 0, 0))
    lw = lambda bi, i: (layer, 0, 0)
    tok = lambda width: pl.BlockSpec((1, tile, width), lambda bi, i: (bi, i, 0))
    in_specs = [tok(POOL_WIDTH), tok(ATTN_WIDTH), tok(CONV_WIDTH), tok(d),
                pl.BlockSpec((1, 6, d), mod_b),
                pl.BlockSpec((1, d, d), lw),
                pl.BlockSpec((1, 1, d), lw), pl.BlockSpec((1, 1, d), lw),
                pl.BlockSpec((1, d, ROUTER_LANES), lw), pl.BlockSpec((1, 1, ROUTER_LANES), lw)]
    out_specs = [tok(d),
                 pl.BlockSpec((tile // SUBLANES, ROW_CHUNKS, SUBLANES, LANES), lambda bi, i: (bi * nt + i, 0, 0, 0)),
                 tok(ROUTER_LANES)]
    out_shape = [jax.ShapeDtypeStruct((b, seq_len, d), F32),
                 jax.ShapeDtypeStruct((b * seq_len // SUBLANES, ROW_CHUNKS, SUBLANES, LANES), F32),
                 jax.ShapeDtypeStruct((b, seq_len, ROUTER_LANES), F32)]
    return pl.pallas_call(
        functools.partial(_post_kernel, alpha=alpha),
        grid=(b, nt), in_specs=in_specs, out_specs=out_specs, out_shape=out_shape,
        compiler_params=_params(("parallel", "parallel")), name="post",
    )(y_pool, y_attn, y_conv, x, mod, w["w_out"], w["ln1_g"], w["ln1_b"], w["router_w"], w["router_b"])


def _route_kernel(sel_ref, ri_ref, rw_ref, seg_ref, *, tile):
    lane = lax.broadcasted_iota(jnp.int32, (1, ROUTER_LANES), 1)
    lane_f = lane.astype(F32)
    blk = ROUTE_BLOCK
    r_io = lax.broadcasted_iota(jnp.int32, (blk, blk), 0)
    c_io = lax.broadcasted_iota(jnp.int32, (blk, blk), 1)
    before = jnp.where(r_io > c_io, 1.0, 0.0).astype(BF16)

    def picks(j):
        s = sel_ref[j * blk:(j + 1) * blk, :]
        return s, lane_f == s[:, 0:1], lane_f == s[:, 1:2]

    carry = jnp.zeros((1, ROUTER_LANES), F32)
    prior = []
    for j in range(tile // blk):
        _, oh1, oh2 = picks(j)
        member = jnp.where(oh1 | oh2, 1.0, 0.0)
        prior.append(_dot(before, member.astype(BF16)) + carry)
        carry = carry + jnp.sum(member, axis=0, keepdims=True)

    cnt = jnp.broadcast_to(carry, (SUBLANES, ROUTER_LANES)).astype(jnp.int32)
    padded = jnp.where(lane < N_EXPERTS, jnp.right_shift(cnt + (SUBLANES - 1), 3) * SUBLANES, 0)
    incl = padded
    step = 1
    while step < N_EXPERTS:
        incl = incl + jnp.where(lane >= step, pltpu.roll(incl, step, axis=1), 0)
        step *= 2
    start = incl - padded
    seg_ref[0] = jnp.where(lane < N_EXPERTS, cnt, pltpu.roll(start, N_EXPERTS, axis=1))
    start_f = start[0:1, :].astype(F32)

    for j in range(tile // blk):
        s, oh1, oh2 = picks(j)
        cols = []
        for oh in (oh1, oh2):
            p = jnp.sum(jnp.where(oh, start_f + prior[j], 0.0), axis=-1, keepdims=True).astype(jnp.int32)
            addr = jnp.right_shift(p, 3) * (SUBLANES * ROW_CHUNKS) + jnp.bitwise_and(p, SUBLANES - 1)
            cols.append(addr.astype(F32))
        z = jnp.where(lane == 0, cols[0], jnp.where(lane == 1, cols[1], s))
        zt = z.T
        ri_ref[0, :, j * blk:(j + 1) * blk] = zt[0:2, :].astype(jnp.int32)
        rw_ref[0, :, j * blk:(j + 1) * blk] = zt[2:4, :]


def _route_call(sel, tile):
    n_tok = sel.shape[0]
    n_tiles = n_tok // tile
    return pl.pallas_call(
        functools.partial(_route_kernel, tile=tile),
        grid=(n_tiles,),
        in_specs=[pl.BlockSpec((tile, ROUTER_LANES), lambda t: (t, 0))],
        out_specs=[pl.BlockSpec((1, TOP_K, tile), lambda t: (t, 0, 0)),
                   pl.BlockSpec((1, TOP_K, tile), lambda t: (t, 0, 0)),
                   pl.BlockSpec((1, SUBLANES, ROUTER_LANES), lambda t: (t, 0, 0))],
        out_shape=[jax.ShapeDtypeStruct((n_tiles, TOP_K, tile), jnp.int32),
                   jax.ShapeDtypeStruct((n_tiles, TOP_K, tile), F32),
                   jax.ShapeDtypeStruct((n_tiles, SUBLANES, ROUTER_LANES), jnp.int32)],
        compiler_params=_params(("parallel",)), name="route",
    )(sel)


def _row_at(start):
    return pl.ds(start, ROW_CHUNKS, stride=SUBLANES)


def _moe_kernel(ri_ref, rw_ref, seg_ref, u2c_ref, wg_ref, wu_ref, wd_ref, o_ref, sorted_ref, *, tile):
    e = pl.program_id(1)
    groups = tile // SUBLANES
    group_rows = SUBLANES * ROW_CHUNKS

    @pl.when((pl.program_id(0) == 0) & (e == 0))
    def _():
        sorted_ref[...] = jnp.zeros_like(sorted_ref)

    @pl.when(e == 0)
    def _():
        def disp_body(a, carry):
            for s in range(SUBLANES):
                row = u2c_ref[_row_at(a * group_rows + s), :]
                for k in range(TOP_K):
                    sorted_ref[_row_at(ri_ref[k * tile + a * SUBLANES + s]), :] = row
            return carry
        lax.fori_loop(0, groups, disp_body, 0)

    cnt = seg_ref[0, 0, e]
    base = seg_ref[0, 0, N_EXPERTS + e]
    blk_groups = EXPERT_BLOCK // SUBLANES
    row_id = lax.broadcasted_iota(jnp.int32, (SUBLANES, 1), 0)

    def tile_at(start, g, c):
        return pl.ds(pl.multiple_of(start + (g * ROW_CHUNKS + c) * SUBLANES, SUBLANES), SUBLANES)

    def blk_body(bi, carry):
        start = (base + bi * EXPERT_BLOCK) * ROW_CHUNKS
        x = jnp.concatenate(
            [jnp.concatenate([sorted_ref[tile_at(start, g, c), :] for g in range(blk_groups)], axis=0)
             for c in range(ROW_CHUNKS)], axis=1).astype(BF16)
        h = _silu(_dot(x, wg_ref[0, 0])) * _dot(x, wu_ref[0, 0])
        y = _dot(h.astype(BF16), wd_ref[0, 0])
        n_valid = cnt - bi * EXPERT_BLOCK
        for g in range(blk_groups):
            keep = (row_id + g * SUBLANES) < n_valid
            for c in range(ROW_CHUNKS):
                sl = tile_at(start, g, c)
                piece = y[g * SUBLANES:(g + 1) * SUBLANES, c * LANES:(c + 1) * LANES]
                sorted_ref[sl, :] = jnp.where(keep, piece, sorted_ref[sl, :])
        return carry
    lax.fori_loop(0, (cnt + EXPERT_BLOCK - 1) // EXPERT_BLOCK, blk_body, 0)

    @pl.when(e == N_EXPERTS - 1)
    def _():
        def comb_body(a, carry):
            for s in range(SUBLANES):
                n = a * SUBLANES + s
                acc = None
                for k in range(TOP_K):
                    term = sorted_ref[_row_at(ri_ref[k * tile + n]), :] * rw_ref[k * tile + n]
                    acc = term if acc is None else acc + term
                o_ref[_row_at(a * group_rows + s), :] = acc
            return carry
        lax.fori_loop(0, groups, comb_body, 0)


def _moe_call(u2c, ri, rw, seg, layer, w, tile):
    n_tok = u2c.shape[0] // ROW_CHUNKS
    n_tiles = n_tok // tile
    d = D_MODEL
    sorted_rows = tile * TOP_K + N_EXPERTS * (SUBLANES - 1) + EXPERT_BLOCK
    sorted_rows = -(-sorted_rows // SUBLANES) * SUBLANES
    smem = lambda shape, imap: pl.BlockSpec(shape, imap, memory_space=pltpu.SMEM)
    in_specs = [smem((TOP_K * tile,), lambda t, e: (t,)),
                smem((TOP_K * tile,), lambda t, e: (t,)),
                smem((1, SUBLANES, ROUTER_LANES), lambda t, e: (t, 0, 0)),
                pl.BlockSpec((tile * ROW_CHUNKS, LANES), lambda t, e: (t, 0), pipeline_mode=pl.Buffered(1)),
                pl.BlockSpec((1, 1, d, EXPERT_FF), lambda t, e: (layer, e, 0, 0)),
                pl.BlockSpec((1, 1, d, EXPERT_FF), lambda t, e: (layer, e, 0, 0)),
                pl.BlockSpec((1, 1, EXPERT_FF, d), lambda t, e: (layer, e, 0, 0))]
    return pl.pallas_call(
        functools.partial(_moe_kernel, tile=tile),
        grid=(n_tiles, N_EXPERTS), in_specs=in_specs,
        out_specs=pl.BlockSpec((tile * ROW_CHUNKS, LANES), lambda t, e: (t, 0)),
        out_shape=jax.ShapeDtypeStruct((n_tok * ROW_CHUNKS, LANES), F32),
        scratch_shapes=[pltpu.VMEM((sorted_rows * ROW_CHUNKS, LANES), F32)],
        compiler_params=_params(("arbitrary", "arbitrary")), name="moe",
    )(ri.reshape(-1), rw.reshape(-1), seg, u2c, w["w_gate"], w["w_up"], w["w_down"])


def _ln2_kernel(m_ref, x1_ref, mod_ref, g_ref, b_ref, o_ref, *, alpha):
    tile = x1_ref.shape[0]
    moe = jnp.concatenate([m_ref[:, c, :, :].reshape(tile, LANES) for c in range(ROW_CHUNKS)], axis=1)
    gate2 = mod_ref[0, 5:6, :]
    o_ref[...] = _layer_norm(alpha * x1_ref[...] + gate2 * moe, g_ref[0], b_ref[0])


def _ln2_call(moe_c, x1, mod, layer, w, alpha, tile):
    b, seq_len, d = x1.shape
    n_tok = b * seq_len
    per_seq = seq_len // tile
    mod_b = (lambda t: (t // per_seq, 0, 0)) if mod.shape[0] == b else (lambda t: (0, 0, 0))
    lw = lambda t: (layer, 0, 0)
    out = pl.pallas_call(
        functools.partial(_ln2_kernel, alpha=alpha),
        grid=(n_tok // tile,),
        in_specs=[pl.BlockSpec((tile // SUBLANES, ROW_CHUNKS, SUBLANES, LANES), lambda t: (t, 0, 0, 0)),
                  pl.BlockSpec((tile, d), lambda t: (t, 0)),
                  pl.BlockSpec((1, 6, d), mod_b),
                  pl.BlockSpec((1, 1, d), lw), pl.BlockSpec((1, 1, d), lw)],
        out_specs=pl.BlockSpec((tile, d), lambda t: (t, 0)),
        out_shape=jax.ShapeDtypeStruct((n_tok, d), F32),
        compiler_params=_params(("parallel",)), name="ln2",
    )(moe_c.reshape(n_tok // SUBLANES, ROW_CHUNKS, SUBLANES, LANES), x1.reshape(n_tok, d), mod,
      w["ln2_g"], w["ln2_b"])
    return out.reshape(b, seq_len, d)


def _rope_tables(seq_len):
    rows = seq_len // GRID_W
    row_id = jnp.repeat(jnp.arange(rows), GRID_W).astype(F32)
    col_id = jnp.tile(jnp.arange(GRID_W), rows).astype(F32)
    inv = ROPE_THETA ** (-jnp.arange(0, AXIS_DIM, 2, dtype=F32) / AXIS_DIM)
    ang = jnp.stack([row_id[:, None] * inv, col_id[:, None] * inv], axis=1)
    cos = jnp.cos(ang)[:, :, None, :]
    sin = jnp.sin(ang)[:, :, None, :]
    shape = (seq_len, 2, 2, AXIS_DIM // 2)
    cos = jnp.broadcast_to(cos, shape).reshape(seq_len, HEAD_DIM)
    sin = (jnp.broadcast_to(sin, shape) * jnp.array([-1.0, 1.0], F32)[None, None, :, None]).reshape(seq_len, HEAD_DIM)
    return jnp.tile(cos, (1, LANES // HEAD_DIM)), jnp.tile(sin, (1, LANES // HEAD_DIM))


def _trunk_layer(x, mod, layer, w, lam_init, alpha, rope, cache, want_f32_kv, tiles):
    outs = _front_call(x, mod, layer, w, rope, want_f32_kv, tiles["front"])
    if want_f32_kv:
        q, k, v, kf, vf, y_pool, y_conv = outs
    else:
        q, k, v, y_pool, y_conv = outs
        kf = vf = None
    y_attn = _attn_call(q, k, v, cache, layer, w, lam_init, tiles["attn"])
    x1, u2c, sel = _post_call(y_pool, y_attn, y_conv, x, mod, layer, w, alpha, tiles["post"])
    ri, rw, seg = _route_call(sel.reshape(-1, ROUTER_LANES), tiles["moe"])
    moe_c = _moe_call(u2c.reshape(-1, LANES), ri, rw, seg, layer, w, tiles["moe"])
    x2 = _ln2_call(moe_c, x1, mod, layer, w, alpha, tiles["ln2"])
    return x2, kf, vf


def kernel(x_prompt, x_sample, cache_k, cache_v, c, c_ctx, w_mod, b_mod, w_in, w_out, pool_w, pool_scale, diff_lambda, subln_g, conv_dw, conv_ln_g, conv_ln_b, conv_pw, ln1_g, ln1_b, ln2_g, ln2_b, router_grp, router_grp_b, router_exp, router_exp_b, w_gate, w_up, w_down):
    depth = w_in.shape[0]
    batch, seq, d = x_prompt.shape
    dec_batch, dec_seq, _ = x_sample.shape
    past = cache_k.shape[2]
    alpha = (2 * depth) ** 0.25

    mod_rows = 16
    cvec = jnp.concatenate([c, c_ctx[None], jnp.zeros((mod_rows - dec_batch - 1, d), F32)], axis=0)
    mod_all = _mod_call(cvec, w_mod, b_mod)

    eye = jnp.eye(len(POOL_WINDOWS), dtype=F32)
    pool_bd = (pool_w[:, :, :, None, :] * eye[None, :, None, :, None]).reshape(depth, POOL_WIDTH, POOL_WIDTH)
    router_w = jnp.concatenate(
        [router_exp, router_grp, jnp.zeros((depth, d, ROUTER_LANES - N_EXPERTS - N_GROUPS), F32)], axis=-1)
    router_b = jnp.concatenate(
        [router_exp_b, router_grp_b, jnp.zeros((depth, ROUTER_LANES - N_EXPERTS - N_GROUPS), F32)], axis=-1)
    r3 = lambda z: z.reshape(depth, 1, z.shape[-1])
    w = {
        "w_in": w_in.astype(BF16), "w_out": w_out.astype(BF16), "pool_bd": pool_bd.astype(BF16),
        "pool_scale": r3(pool_scale), "diff_lambda": diff_lambda, "subln_g": r3(subln_g),
        "conv_dw": conv_dw, "conv_ln_g": r3(conv_ln_g), "conv_ln_b": r3(conv_ln_b),
        "conv_pw": conv_pw.astype(BF16), "ln1_g": r3(ln1_g), "ln1_b": r3(ln1_b), "ln2_g": r3(ln2_g),
        "ln2_b": r3(ln2_b), "router_w": router_w, "router_b": r3(router_b),
        "w_gate": w_gate.astype(BF16), "w_up": w_up.astype(BF16), "w_down": w_down.astype(BF16),
    }
    rope = _rope_tables(dec_seq)
    cache = (cache_k.reshape(dec_batch, depth, past, QK_WIDTH), cache_v.reshape(dec_batch, depth, past, ATTN_WIDTH))
    ctx_tiles = {"front": seq, "attn": seq, "post": seq, "moe": 2048, "ln2": seq}
    lat_tiles = {"front": 512, "attn": 1024, "post": 512, "moe": 2048, "ln2": 512}

    xp, xs = x_prompt, x_sample
    new_k, new_v = [], []
    for l in range(depth):
        lam_init = 0.8 - 0.6 * math.exp(-0.3 * l)
        mod_ctx = mod_all[l, dec_batch:dec_batch + 1].reshape(1, 6, d)
        mod_lat = mod_all[l, :dec_batch].reshape(dec_batch, 6, d)
        xp, kf, vf = _trunk_layer(xp, mod_ctx, l, w, lam_init, alpha, None, None, True, ctx_tiles)
        new_k.append(kf.reshape(batch, seq, ATTN_HEADS, 2, HEAD_DIM))
        new_v.append(vf.reshape(batch, seq, ATTN_HEADS, V_DIM))
        xs, _, _ = _trunk_layer(xs, mod_lat, l, w, lam_init, alpha, rope, cache, False, lat_tiles)
    return (xp, xs, jnp.stack(new_k, axis=1), jnp.stack(new_v, axis=1))
```

```python
import functools
import math

import jax
import jax.numpy as jnp
from jax import lax
from jax.experimental import pallas as pl
from jax.experimental.pallas import tpu as pltpu

F32 = jnp.float32
BF16 = jnp.bfloat16

D_MODEL = 1024
GRID_W = 64
POOL_WIDTH = 256
POOL_WINDOWS = (2, 4, 8, 16)
POOL_GC = 64
ATTN_HEADS = 4
HEAD_DIM = 64
V_DIM = 128
ATTN_WIDTH = 512
QK_WIDTH = 512
CONV_WIDTH = 256
CONV_K = 31
IN_WIDTH = 2304
AXIS_DIM = 32
ROPE_THETA = 10000.0
N_GROUPS = 4
EXPERTS_PER_GROUP = 8
N_EXPERTS = 32
EXPERT_FF = 256
LN_EPS = 1e-5

O_A = 0
O_Q = POOL_WIDTH
O_K = O_Q + QK_WIDTH
O_V = O_K + QK_WIDTH
O_C = O_V + ATTN_WIDTH

TOP_K = 2
HALO = 16
LANES = 128
SUBLANES = 8
ROW_CHUNKS = D_MODEL // LANES
ROUTER_LANES = 128
ROUTE_BLOCK = 512
EXPERT_BLOCK = 128
BLOCKS_PER_TRIP = 2
EXPERTS_PER_STEP = 2
ATTN_ROWS = 128
LOG2E = 1.4426950408889634
NEG_BIG = -1e30
VMEM_LIMIT = 56 * 1024 * 1024


def _silu(x):
    return x * jax.nn.sigmoid(x)


def _layer_norm(x, g, b):
    mu = jnp.mean(x, axis=-1, keepdims=True)
    xc = x - mu
    var = jnp.mean(xc * xc, axis=-1, keepdims=True)
    return xc * lax.rsqrt(var + LN_EPS) * g + b


def _dot(a, b):
    return jnp.dot(a, b, preferred_element_type=F32)


def _params(sem):
    return pltpu.CompilerParams(dimension_semantics=sem, vmem_limit_bytes=VMEM_LIMIT)


def _mod_kernel(c_ref, w_ref, b_ref, o_ref):
    s = _silu(c_ref[...])
    o_ref[0] = _dot(s.astype(BF16), w_ref[0].astype(BF16)) + b_ref[0]


def _mod_call(cvec, w_mod, b_mod):
    depth, d, n6 = w_mod.shape
    rows = cvec.shape[0]
    nb = 1536
    return pl.pallas_call(
        _mod_kernel,
        grid=(depth, n6 // nb),
        in_specs=[
            pl.BlockSpec((rows, d), lambda l, j: (0, 0)),
            pl.BlockSpec((1, d, nb), lambda l, j: (l, 0, j)),
            pl.BlockSpec((1, 1, nb), lambda l, j: (l, 0, j)),
        ],
        out_specs=pl.BlockSpec((1, rows, nb), lambda l, j: (l, 0, j)),
        out_shape=jax.ShapeDtypeStruct((depth, rows, n6), F32),
        compiler_params=_params(("parallel", "parallel")),
        name="mod",
    )(cvec, w_mod, b_mod.reshape(depth, 1, n6))


def _front_kernel(*refs, seq_len, tile, has_rope, want_f32_kv):
    it = iter(refs)
    x_ref, xp_ref, xn_ref, mod_ref, win_ref = next(it), next(it), next(it), next(it), next(it)
    poolw_ref, pscale_ref, dw_ref, cg_ref, cb_ref, pw_ref = (next(it) for _ in range(6))
    if has_rope:
        cos_ref, sin_ref = next(it), next(it)
    q_ref, k_ref, v_ref = next(it), next(it), next(it)
    if want_f32_kv:
        kf_ref, vf_ref = next(it), next(it)
    yp_ref, yc_ref = next(it), next(it)

    i = pl.program_id(1)
    n_ext = tile + 2 * HALO
    shift = mod_ref[0, 0:1, :]
    scale = mod_ref[0, 1:2, :]
    xe = jnp.concatenate([xp_ref[0], x_ref[0], xn_ref[0]], axis=0)
    u = (xe * (1.0 + scale) + shift).astype(BF16)
    proj = _dot(u, win_ref[0])

    row = lax.broadcasted_iota(jnp.int32, (n_ext, 1), 0)
    gpos = i * tile - HALO + row
    valid = (gpos >= 0) & (gpos < seq_len)

    a_ext = jnp.where(valid, proj[:, O_A:O_A + POOL_WIDTH], 0.0)

    def sh(z, s):
        return pltpu.roll(z, s % n_ext, axis=0)

    p2 = a_ext + sh(a_ext, 1)
    p4 = sh(p2, 1) + sh(p2, -1)
    p8 = sh(p4, 2) + sh(p4, -2)
    p16 = sh(p8, 4) + sh(p8, -4)
    t = gpos[HALO:HALO + tile]
    lane = lax.broadcasted_iota(jnp.int32, (1, POOL_WIDTH), 1)
    mean = None
    for g, (w, p) in enumerate(zip(POOL_WINDOWS, (p2, p4, p8, p16))):
        cnt = jnp.minimum(t + w // 2, seq_len) - jnp.maximum(t - w // 2, 0)
        m = p[HALO:HALO + tile] * (1.0 / cnt.astype(F32))
        mean = m if mean is None else jnp.where(lane >= g * POOL_GC, m, mean)
    pooled = mean - a_ext[HALO:HALO + tile]
    y_pool = _dot(pooled.astype(BF16), poolw_ref[0]) * pscale_ref[0]
    yp_ref[0] = y_pool.astype(BF16)

    ca = proj[:, O_C:O_C + CONV_WIDTH]
    cb = proj[:, O_C + CONV_WIDTH:O_C + 2 * CONV_WIDTH]
    h = jnp.where(valid, ca * jax.nn.sigmoid(cb), 0.0)
    acc = None
    for r in range(SUBLANES):
        hr = h if r == 0 else pltpu.roll(h, n_ext - r, axis=0)
        for j in range(CONV_K):
            off = HALO - CONV_K // 2 + j
            if off % SUBLANES != r:
                continue
            term = hr[off - r:off - r + tile] * dw_ref[0, j:j + 1, :]
            acc = term if acc is None else acc + term
    hn = _silu(_layer_norm(acc, cg_ref[0], cb_ref[0]))
    yc_ref[0] = _dot(hn.astype(BF16), pw_ref[0]).astype(BF16)

    qf = proj[HALO:HALO + tile, O_Q:O_Q + QK_WIDTH]
    kf = proj[HALO:HALO + tile, O_K:O_K + QK_WIDTH]
    vf = proj[HALO:HALO + tile, O_V:O_V + ATTN_WIDTH]
    if want_f32_kv:
        kf_ref[0] = kf
        vf_ref[0] = vf
    if has_rope:
        cos = cos_ref[...]
        sin = sin_ref[...]
        l128 = lax.broadcasted_iota(jnp.int32, (1, LANES), 1)
        first_half = (l128 & (AXIS_DIM - 1)) < (AXIS_DIM // 2)

        def rope(z):
            outs = []
            for c in range(QK_WIDTH // LANES):
                zc = z[:, c * LANES:(c + 1) * LANES]
                partner = jnp.where(first_half,
                                    pltpu.roll(zc, LANES - AXIS_DIM // 2, axis=1),
                                    pltpu.roll(zc, AXIS_DIM // 2, axis=1))
                outs.append(zc * cos + partner * sin)
            return jnp.concatenate(outs, axis=1)

        qf = rope(qf)
        kf = rope(kf)
    q_ref[0] = (qf * (LOG2E / math.sqrt(HEAD_DIM))).astype(BF16)
    k_ref[0] = kf.astype(BF16)
    v_ref[0] = vf.astype(BF16)


def _front_call(x, mod, layer, w, rope, want_f32_kv, tile):
    b, seq_len, d = x.shape
    nt = seq_len // tile
    hb = tile // HALO
    n_hblk = seq_len // HALO
    mod_b = (lambda bi, i: (bi, 0, 0)) if mod.shape[0] == b else (lambda bi, i: (0, 0, 0))
    lw = lambda bi, i: (layer, 0, 0)
    in_specs = [
        pl.BlockSpec((1, tile, d), lambda bi, i: (bi, i, 0)),
        pl.BlockSpec((1, HALO, d), lambda bi, i: (bi, jnp.maximum(i * hb - 1, 0), 0)),
        pl.BlockSpec((1, HALO, d), lambda bi, i: (bi, jnp.minimum((i + 1) * hb, n_hblk - 1), 0)),
        pl.BlockSpec((1, 6, d), mod_b),
        pl.BlockSpec((1, d, IN_WIDTH), lw),
        pl.BlockSpec((1, POOL_WIDTH, POOL_WIDTH), lw),
        pl.BlockSpec((1, 1, POOL_WIDTH), lw),
        pl.BlockSpec((1, CONV_K, CONV_WIDTH), lw),
        pl.BlockSpec((1, 1, CONV_WIDTH), lw),
        pl.BlockSpec((1, 1, CONV_WIDTH), lw),
        pl.BlockSpec((1, CONV_WIDTH, CONV_WIDTH), lw),
    ]
    args = [x, x, x, mod, w["w_in"], w["pool_bd"], w["pool_scale"], w["conv_dw"], w["conv_ln_g"],
            w["conv_ln_b"], w["conv_pw"]]
    if rope is not None:
        in_specs += [pl.BlockSpec((tile, LANES), lambda bi, i: (i, 0))] * 2
        args += list(rope)
    tok = lambda width: pl.BlockSpec((1, tile, width), lambda bi, i: (bi, i, 0))
    out_specs = [tok(QK_WIDTH), tok(QK_WIDTH), tok(ATTN_WIDTH)]
    out_shape = [jax.ShapeDtypeStruct((b, seq_len, QK_WIDTH), BF16)] * 3
    if want_f32_kv:
        out_specs += [tok(QK_WIDTH), tok(ATTN_WIDTH)]
        out_shape += [jax.ShapeDtypeStruct((b, seq_len, QK_WIDTH), F32)] * 2
    out_specs += [tok(POOL_WIDTH), tok(CONV_WIDTH)]
    out_shape += [jax.ShapeDtypeStruct((b, seq_len, POOL_WIDTH), BF16)] * 2
    kern = functools.partial(_front_kernel, seq_len=seq_len, tile=tile, has_rope=rope is not None,
                             want_f32_kv=want_f32_kv)
    return pl.pallas_call(
        kern, grid=(b, nt), in_specs=in_specs, out_specs=out_specs, out_shape=out_shape,
        compiler_params=_params(("parallel", "parallel")), name="front",
    )(*args)


def _attn_kernel(*refs, lam_init, has_cache):
    if has_cache:
        lam_ref, g_ref, q_ref, k_ref, v_ref, ck_ref, cv_ref, o_ref = refs
    else:
        lam_ref, g_ref, q_ref, k_ref, v_ref, o_ref = refs
    lp = lam_ref[0]
    lam = (jnp.exp(jnp.sum(lp[0:1] * lp[1:2], axis=-1, keepdims=True))
           - jnp.exp(jnp.sum(lp[2:3] * lp[3:4], axis=-1, keepdims=True)) + lam_init)

    lane = lax.broadcasted_iota(jnp.int32, (1, LANES), 1)
    nt = (((1,), (1,)), ((), ()))
    keys = [k_ref[0]]
    vals = [v_ref[0]]
    if has_cache:
        keys.append(ck_ref[0, 0].astype(BF16))
        vals.append(cv_ref[0, 0].astype(BF16))

    def scores(q):
        zero = jnp.zeros_like(q)
        return [[lax.dot_general(qm, kk, nt, preferred_element_type=F32) for kk in keys]
                for qm in (jnp.where(lane < HEAD_DIM, q, zero), jnp.where(lane >= HEAD_DIM, q, zero))]

    def finish(score_maps):
        parts = []
        for s in score_maps:
            m = functools.reduce(jnp.maximum, [jnp.max(z, axis=-1, keepdims=True) for z in s])
            e = [jnp.exp2(z - m) for z in s]
            den = functools.reduce(jnp.add, [jnp.sum(z, axis=-1, keepdims=True) for z in e])
            parts.append((e, den))
        (e1, den1), (e2, den2) = parts
        c = lam * den1 / den2
        o = None
        for z1, z2, vv in zip(e1, e2, vals):
            part = _dot((z1 - z2 * c).astype(BF16), vv)
            o = part if o is None else o + part
        o = o * (1.0 / den1)
        o = o * lax.rsqrt(jnp.mean(o * o, axis=-1, keepdims=True) + LN_EPS) * g_ref[0] * (1.0 - lam_init)
        return o.astype(BF16)

    tq = q_ref.shape[1]
    n_sub = max(1, tq // ATTN_ROWS)
    step = tq // n_sub
    pending = scores(q_ref[0, 0:step, :])
    for j in range(n_sub):
        current = pending
        if j + 1 < n_sub:
            pending = scores(q_ref[0, (j + 1) * step:(j + 2) * step, :])
        o_ref[0, j * step:(j + 1) * step, :] = finish(current)


def _attn_call(q, k, v, cache, layer, w, lam_init, tq):
    b, seq_len, _ = q.shape
    lw = lambda bi, h, i: (layer, 0, 0)
    in_specs = [
        pl.BlockSpec((1, 4, HEAD_DIM), lw),
        pl.BlockSpec((1, 1, V_DIM), lw),
        pl.BlockSpec((1, tq, LANES), lambda bi, h, i: (bi, i, h)),
        pl.BlockSpec((1, seq_len, LANES), lambda bi, h, i: (bi, 0, h)),
        pl.BlockSpec((1, seq_len, V_DIM), lambda bi, h, i: (bi, 0, h)),
    ]
    args = [w["diff_lambda"], w["subln_g"], q, k, v]
    if cache is not None:
        past = cache[0].shape[2]
        cspec = pl.BlockSpec((1, 1, past, LANES), lambda bi, h, i: (bi, layer, 0, h))
        in_specs += [cspec, cspec]
        args += list(cache)
    kern = functools.partial(_attn_kernel, lam_init=lam_init, has_cache=cache is not None)
    return pl.pallas_call(
        kern, grid=(b, ATTN_HEADS, seq_len // tq), in_specs=in_specs,
        out_specs=pl.BlockSpec((1, tq, V_DIM), lambda bi, h, i: (bi, i, h)),
        out_shape=jax.ShapeDtypeStruct((b, seq_len, ATTN_WIDTH), BF16),
        compiler_params=_params(("parallel", "parallel", "parallel")), name="attn",
    )(*args)


def _post_kernel(yp_ref, ya_ref, yc_ref, x_ref, mod_ref, wout_ref, g_ref, b_ref, wr_ref, br_ref,
                 x1_ref, u2c_ref, sel_ref, *, alpha):
    o1 = POOL_WIDTH
    o2 = o1 + ATTN_WIDTH
    y = (_dot(yp_ref[0], wout_ref[0, 0:o1, :]) + _dot(ya_ref[0], wout_ref[0, o1:o2, :])
         + _dot(yc_ref[0], wout_ref[0, o2:, :]))
    gate1 = mod_ref[0, 2:3, :]
    shift2 = mod_ref[0, 3:4, :]
    scale2 = mod_ref[0, 4:5, :]
    x1 = _layer_norm(alpha * x_ref[0] + gate1 * y, g_ref[0], b_ref[0])
    x1_ref[0] = x1
    u2 = x1 * (1.0 + scale2) + shift2
    tile = u2.shape[0]
    for c in range(ROW_CHUNKS):
        u2c_ref[:, c, :, :] = u2[:, c * LANES:(c + 1) * LANES].reshape(tile // SUBLANES, SUBLANES, LANES)

    u_hi = u2.astype(BF16)
    u_lo = (u2 - u_hi.astype(F32)).astype(BF16)
    both = _dot(u_hi, wr_ref[0]) + _dot(u_lo, wr_ref[0])
    logits = both[:, :ROUTER_LANES] + both[:, ROUTER_LANES:] + br_ref[0]
    lane = lax.broadcasted_iota(jnp.int32, (1, ROUTER_LANES), 1)
    lane_f = lane.astype(F32)
    is_g = (lane >= N_EXPERTS) & (lane < N_EXPERTS + N_GROUPS)
    rmax = lambda z: jnp.max(z, axis=-1, keepdims=True)
    rmin = lambda z: jnp.min(z, axis=-1, keepdims=True)
    gl = jnp.where(is_g, logits, NEG_BIG)
    ge = jnp.where(is_g, jnp.exp(gl - rmax(gl)), 0.0)
    gp = ge / jnp.sum(ge, axis=-1, keepdims=True)
    g_w = rmax(gp)
    g_idx = rmin(jnp.where(is_g & (gp == g_w), lane_f - N_EXPERTS, 1e9))
    in_grp = (lane < N_EXPERTS) & (jnp.right_shift(lane, 3).astype(F32) == g_idx)
    el = jnp.where(in_grp, logits, NEG_BIG)
    v1 = rmax(el)
    i1 = rmin(jnp.where(in_grp & (el == v1), lane_f, 1e9))
    rest = in_grp & (lane_f != i1)
    el2 = jnp.where(rest, logits, NEG_BIG)
    v2 = rmax(el2)
    i2 = rmin(jnp.where(rest & (el2 == v2), lane_f, 1e9))
    tt = jnp.exp(v2 - v1)
    w1 = 1.0 / (1.0 + tt)
    w2 = tt / (1.0 + tt)
    sel = jnp.zeros((tile, ROUTER_LANES), F32)
    for j, col in enumerate((i1, i2, g_w * w1, g_w * w2)):
        sel = jnp.where(lane == j, col, sel)
    sel_ref[0] = sel


def _post_call(y_pool, y_attn, y_conv, x, mod, layer, w, alpha, tile):
    b, seq_len, d = x.shape
    nt = seq_len // tile
    mod_b = (lambda bi, i: (bi, 0, 0)) if mod.shape[0] == b else (lambda bi, i: (0, 0, 0))
    lw = lambda bi, i: (layer, 0, 0)
    tok = lambda width: pl.BlockSpec((1, tile, width), lambda bi, i: (bi, i, 0))
    in_specs = [tok(POOL_WIDTH), tok(ATTN_WIDTH), tok(CONV_WIDTH), tok(d),
                pl.BlockSpec((1, 6, d), mod_b),
                pl.BlockSpec((1, d, d), lw),
                pl.BlockSpec((1, 1, d), lw), pl.BlockSpec((1, 1, d), lw),
                pl.BlockSpec((1, d, 2 * ROUTER_LANES), lw), pl.BlockSpec((1, 1, ROUTER_LANES), lw)]
    out_specs = [tok(d),
                 pl.BlockSpec((tile // SUBLANES, ROW_CHUNKS, SUBLANES, LANES), lambda bi, i: (bi * nt + i, 0, 0, 0)),
                 tok(ROUTER_LANES)]
    out_shape = [jax.ShapeDtypeStruct((b, seq_len, d), F32),
                 jax.ShapeDtypeStruct((b * seq_len // SUBLANES, ROW_CHUNKS, SUBLANES, LANES), F32),
                 jax.ShapeDtypeStruct((b, seq_len, ROUTER_LANES), F32)]
    return pl.pallas_call(
        functools.partial(_post_kernel, alpha=alpha),
        grid=(b, nt), in_specs=in_specs, out_specs=out_specs, out_shape=out_shape,
        compiler_params=_params(("parallel", "parallel")), name="post",
    )(y_pool, y_attn, y_conv, x, mod, w["w_out"], w["ln1_g"], w["ln1_b"], w["router_w"], w["router_b"])


def _route_kernel(sel_ref, ri_ref, rw_ref, seg_ref, *, tile):
    lane = lax.broadcasted_iota(jnp.int32, (1, ROUTER_LANES), 1)
    lane_f = lane.astype(F32)
    blk = ROUTE_BLOCK
    r_io = lax.broadcasted_iota(jnp.int32, (blk, blk), 0)
    c_io = lax.broadcasted_iota(jnp.int32, (blk, blk), 1)
    before = jnp.where(r_io > c_io, 1.0, 0.0).astype(BF16)

    def picks(j):
        s = sel_ref[j * blk:(j + 1) * blk, :]
        return s, lane_f == s[:, 0:1], lane_f == s[:, 1:2]

    carry = jnp.zeros((1, ROUTER_LANES), F32)
    prior = []
    for j in range(tile // blk):
        _, oh1, oh2 = picks(j)
        member = jnp.where(oh1 | oh2, 1.0, 0.0)
        prior.append(_dot(before, member.astype(BF16)) + carry)
        carry = carry + jnp.sum(member, axis=0, keepdims=True)

    cnt = jnp.broadcast_to(carry, (SUBLANES, ROUTER_LANES)).astype(jnp.int32)
    padded = jnp.where(lane < N_EXPERTS, jnp.right_shift(cnt + (SUBLANES - 1), 3) * SUBLANES, 0)
    incl = padded
    step = 1
    while step < N_EXPERTS:
        incl = incl + jnp.where(lane >= step, pltpu.roll(incl, step, axis=1), 0)
        step *= 2
    start = incl - padded
    seg_ref[0] = jnp.where(lane < N_EXPERTS, cnt, pltpu.roll(start, N_EXPERTS, axis=1))
    start_f = start[0:1, :].astype(F32)

    for j in range(tile // blk):
        s, oh1, oh2 = picks(j)
        cols = []
        for oh in (oh1, oh2):
            p = jnp.sum(jnp.where(oh, start_f + prior[j], 0.0), axis=-1, keepdims=True).astype(jnp.int32)
            addr = jnp.right_shift(p, 3) * (SUBLANES * ROW_CHUNKS) + jnp.bitwise_and(p, SUBLANES - 1)
            cols.append(addr.astype(F32))
        z = jnp.where(lane == 0, cols[0], jnp.where(lane == 1, cols[1], s))
        zt = z.T
        ri_ref[0, :, j * blk:(j + 1) * blk] = zt[0:2, :].astype(jnp.int32)
        rw_ref[0, :, j * blk:(j + 1) * blk] = zt[2:4, :]


def _route_call(sel, tile):
    n_tok = sel.shape[0]
    n_tiles = n_tok // tile
    return pl.pallas_call(
        functools.partial(_route_kernel, tile=tile),
        grid=(n_tiles,),
        in_specs=[pl.BlockSpec((tile, ROUTER_LANES), lambda t: (t, 0))],
        out_specs=[pl.BlockSpec((1, TOP_K, tile), lambda t: (t, 0, 0)),
                   pl.BlockSpec((1, TOP_K, tile), lambda t: (t, 0, 0)),
                   pl.BlockSpec((1, SUBLANES, ROUTER_LANES), lambda t: (t, 0, 0))],
        out_shape=[jax.ShapeDtypeStruct((n_tiles, TOP_K, tile), jnp.int32),
                   jax.ShapeDtypeStruct((n_tiles, TOP_K, tile), F32),
                   jax.ShapeDtypeStruct((n_tiles, SUBLANES, ROUTER_LANES), jnp.int32)],
        compiler_params=_params(("parallel",)), name="route",
    )(sel)


def _row_at(start):
    return pl.ds(start, ROW_CHUNKS, stride=SUBLANES)


def _moe_kernel(ri_ref, rw_ref, seg_ref, u2c_ref, wg_ref, wu_ref, wd_ref, o_ref, sorted_ref, *, tile):
    step = pl.program_id(1)
    groups = tile // SUBLANES
    group_rows = SUBLANES * ROW_CHUNKS

    @pl.when((pl.program_id(0) == 0) & (step == 0))
    def _():
        sorted_ref[...] = jnp.zeros_like(sorted_ref)

    @pl.when(step == 0)
    def _():
        def disp_body(a, carry):
            for s in range(SUBLANES):
                row = u2c_ref[_row_at(a * group_rows + s), :]
                for k in range(TOP_K):
                    sorted_ref[_row_at(ri_ref[k * tile + a * SUBLANES + s]), :] = row
            return carry
        lax.fori_loop(0, groups, disp_body, 0)

    blk_groups = EXPERT_BLOCK // SUBLANES
    row_id = lax.broadcasted_iota(jnp.int32, (SUBLANES, 1), 0)

    def tile_at(start, g, c):
        return pl.ds(pl.multiple_of(start + (g * ROW_CHUNKS + c) * SUBLANES, SUBLANES), SUBLANES)

    for j in range(EXPERTS_PER_STEP):
        e = step * EXPERTS_PER_STEP + j
        cnt = seg_ref[0, 0, e]
        base = seg_ref[0, 0, N_EXPERTS + e]

        def blk_body(bi, carry, j=j, cnt=cnt, base=base):
            firsts = [(bi * BLOCKS_PER_TRIP + u) * EXPERT_BLOCK for u in range(BLOCKS_PER_TRIP)]
            starts = [(base + first) * ROW_CHUNKS for first in firsts]
            xs = [jnp.concatenate(
                [jnp.concatenate([sorted_ref[tile_at(start, g, c), :] for g in range(blk_groups)], axis=0)
                 for c in range(ROW_CHUNKS)], axis=1).astype(BF16) for start in starts]
            hs = [_silu(_dot(x, wg_ref[0, j])) * _dot(x, wu_ref[0, j]) for x in xs]
            ys = [_dot(h.astype(BF16), wd_ref[0, j]) for h in hs]
            for first, start, y in zip(firsts, starts, ys):
                n_valid = cnt - first
                for g in range(blk_groups):
                    keep = (row_id + g * SUBLANES) < n_valid
                    for c in range(ROW_CHUNKS):
                        sl = tile_at(start, g, c)
                        piece = y[g * SUBLANES:(g + 1) * SUBLANES, c * LANES:(c + 1) * LANES]
                        sorted_ref[sl, :] = jnp.where(keep, piece, sorted_ref[sl, :])
            return carry
        trip_rows = EXPERT_BLOCK * BLOCKS_PER_TRIP
        lax.fori_loop(0, (cnt + trip_rows - 1) // trip_rows, blk_body, 0)

    @pl.when(step == pl.num_programs(1) - 1)
    def _():
        def comb_body(a, carry):
            for s in range(SUBLANES):
                n = a * SUBLANES + s
                acc = None
                for k in range(TOP_K):
                    term = sorted_ref[_row_at(ri_ref[k * tile + n]), :] * rw_ref[k * tile + n]
                    acc = term if acc is None else acc + term
                o_ref[_row_at(a * group_rows + s), :] = acc
            return carry
        lax.fori_loop(0, groups, comb_body, 0)


def _moe_call(u2c, ri, rw, seg, layer, w, tile):
    n_tok = u2c.shape[0] // ROW_CHUNKS
    n_tiles = n_tok // tile
    d = D_MODEL
    sorted_rows = tile * TOP_K + N_EXPERTS * (SUBLANES - 1) + EXPERT_BLOCK * BLOCKS_PER_TRIP
    sorted_rows = -(-sorted_rows // SUBLANES) * SUBLANES
    eps = EXPERTS_PER_STEP
    smem = lambda shape, imap: pl.BlockSpec(shape, imap, memory_space=pltpu.SMEM)
    in_specs = [smem((TOP_K * tile,), lambda t, e: (t,)),
                smem((TOP_K * tile,), lambda t, e: (t,)),
                smem((1, SUBLANES, ROUTER_LANES), lambda t, e: (t, 0, 0)),
                pl.BlockSpec((tile * ROW_CHUNKS, LANES), lambda t, e: (t, 0), pipeline_mode=pl.Buffered(1)),
                pl.BlockSpec((1, eps, d, EXPERT_FF), lambda t, e: (layer, e, 0, 0)),
                pl.BlockSpec((1, eps, d, EXPERT_FF), lambda t, e: (layer, e, 0, 0)),
                pl.BlockSpec((1, eps, EXPERT_FF, d), lambda t, e: (layer, e, 0, 0))]
    return pl.pallas_call(
        functools.partial(_moe_kernel, tile=tile),
        grid=(n_tiles, N_EXPERTS // eps), in_specs=in_specs,
        out_specs=pl.BlockSpec((tile * ROW_CHUNKS, LANES), lambda t, e: (t, 0)),
        out_shape=jax.ShapeDtypeStruct((n_tok * ROW_CHUNKS, LANES), F32),
        scratch_shapes=[pltpu.VMEM((sorted_rows * ROW_CHUNKS, LANES), F32)],
        compiler_params=_params(("arbitrary", "arbitrary")), name="moe",
    )(ri.reshape(-1), rw.reshape(-1), seg, u2c, w["w_gate"], w["w_up"], w["w_down"])


def _ln2_kernel(m_ref, x1_ref, mod_ref, g_ref, b_ref, o_ref, *, alpha):
    tile = x1_ref.shape[0]
    moe = jnp.concatenate([m_ref[:, c, :, :].reshape(tile, LANES) for c in range(ROW_CHUNKS)], axis=1)
    gate2 = mod_ref[0, 5:6, :]
    o_ref[...] = _layer_norm(alpha * x1_ref[...] + gate2 * moe, g_ref[0], b_ref[0])


def _ln2_call(moe_c, x1, mod, layer, w, alpha, tile):
    b, seq_len, d = x1.shape
    n_tok = b * seq_len
    per_seq = seq_len // tile
    mod_b = (lambda t: (t // per_seq, 0, 0)) if mod.shape[0] == b else (lambda t: (0, 0, 0))
    lw = lambda t: (layer, 0, 0)
    out = pl.pallas_call(
        functools.partial(_ln2_kernel, alpha=alpha),
        grid=(n_tok // tile,),
        in_specs=[pl.BlockSpec((tile // SUBLANES, ROW_CHUNKS, SUBLANES, LANES), lambda t: (t, 0, 0, 0)),
                  pl.BlockSpec((tile, d), lambda t: (t, 0)),
                  pl.BlockSpec((1, 6, d), mod_b),
                  pl.BlockSpec((1, 1, d), lw), pl.BlockSpec((1, 1, d), lw)],
        out_specs=pl.BlockSpec((tile, d), lambda t: (t, 0)),
        out_shape=jax.ShapeDtypeStruct((n_tok, d), F32),
        compiler_params=_params(("parallel",)), name="ln2",
    )(moe_c.reshape(n_tok // SUBLANES, ROW_CHUNKS, SUBLANES, LANES), x1.reshape(n_tok, d), mod,
      w["ln2_g"], w["ln2_b"])
    return out.reshape(b, seq_len, d)


def _rope_tables(seq_len):
    rows = seq_len // GRID_W
    row_id = jnp.repeat(jnp.arange(rows), GRID_W).astype(F32)
    col_id = jnp.tile(jnp.arange(GRID_W), rows).astype(F32)
    inv = ROPE_THETA ** (-jnp.arange(0, AXIS_DIM, 2, dtype=F32) / AXIS_DIM)
    ang = jnp.stack([row_id[:, None] * inv, col_id[:, None] * inv], axis=1)
    cos = jnp.cos(ang)[:, :, None, :]
    sin = jnp.sin(ang)[:, :, None, :]
    shape = (seq_len, 2, 2, AXIS_DIM // 2)
    cos = jnp.broadcast_to(cos, shape).reshape(seq_len, HEAD_DIM)
    sin = (jnp.broadcast_to(sin, shape) * jnp.array([-1.0, 1.0], F32)[None, None, :, None]).reshape(seq_len, HEAD_DIM)
    return jnp.tile(cos, (1, LANES // HEAD_DIM)), jnp.tile(sin, (1, LANES // HEAD_DIM))


def _trunk_layer(x, mod, layer, w, lam_init, alpha, rope, cache, want_f32_kv, tiles):
    outs = _front_call(x, mod, layer, w, rope, want_f32_kv, tiles["front"])
    if want_f32_kv:
        q, k, v, kf, vf, y_pool, y_conv = outs
    else:
        q, k, v, y_pool, y_conv = outs
        kf = vf = None
    y_attn = _attn_call(q, k, v, cache, layer, w, lam_init, tiles["attn"])
    x1, u2c, sel = _post_call(y_pool, y_attn, y_conv, x, mod, layer, w, alpha, tiles["post"])
    ri, rw, seg = _route_call(sel.reshape(-1, ROUTER_LANES), tiles["moe"])
    moe_c = _moe_call(u2c.reshape(-1, LANES), ri, rw, seg, layer, w, tiles["moe"])
    x2 = _ln2_call(moe_c, x1, mod, layer, w, alpha, tiles["ln2"])
    return x2, kf, vf


def kernel(x_prompt, x_sample, cache_k, cache_v, c, c_ctx, w_mod, b_mod, w_in, w_out, pool_w, pool_scale, diff_lambda, subln_g, conv_dw, conv_ln_g, conv_ln_b, conv_pw, ln1_g, ln1_b, ln2_g, ln2_b, router_grp, router_grp_b, router_exp, router_exp_b, w_gate, w_up, w_down):
    depth = w_in.shape[0]
    batch, seq, d = x_prompt.shape
    dec_batch, dec_seq, _ = x_sample.shape
    past = cache_k.shape[2]
    alpha = (2 * depth) ** 0.25

    mod_rows = 16
    cvec = jnp.concatenate([c, c_ctx[None], jnp.zeros((mod_rows - dec_batch - 1, d), F32)], axis=0)
    mod_all = _mod_call(cvec, w_mod, b_mod)

    eye = jnp.eye(len(POOL_WINDOWS), dtype=F32)
    pool_bd = (pool_w[:, :, :, None, :] * eye[None, :, None, :, None]).reshape(depth, POOL_WIDTH, POOL_WIDTH)
    router_w = jnp.concatenate(
        [router_exp, router_grp, jnp.zeros((depth, d, ROUTER_LANES - N_EXPERTS - N_GROUPS), F32)], axis=-1)
    router_hi = router_w.astype(BF16)
    router_lo = (router_w - router_hi.astype(F32)).astype(BF16)
    router_b = jnp.concatenate(
        [router_exp_b, router_grp_b, jnp.zeros((depth, ROUTER_LANES - N_EXPERTS - N_GROUPS), F32)], axis=-1)
    r3 = lambda z: z.reshape(depth, 1, z.shape[-1])
    w = {
        "w_in": w_in.astype(BF16), "w_out": w_out.astype(BF16), "pool_bd": pool_bd.astype(BF16),
        "pool_scale": r3(pool_scale), "diff_lambda": diff_lambda, "subln_g": r3(subln_g),
        "conv_dw": conv_dw, "conv_ln_g": r3(conv_ln_g), "conv_ln_b": r3(conv_ln_b),
        "conv_pw": conv_pw.astype(BF16), "ln1_g": r3(ln1_g), "ln1_b": r3(ln1_b), "ln2_g": r3(ln2_g),
        "ln2_b": r3(ln2_b), "router_w": jnp.concatenate([router_hi, router_lo], axis=-1), "router_b": r3(router_b),
        "w_gate": w_gate.astype(BF16), "w_up": w_up.astype(BF16), "w_down": w_down.astype(BF16),
    }
    rope = _rope_tables(dec_seq)
    cache = (cache_k.reshape(dec_batch, depth, past, QK_WIDTH), cache_v.reshape(dec_batch, depth, past, ATTN_WIDTH))
    ctx_tiles = {"front": seq, "attn": seq, "post": seq, "moe": 2048, "ln2": seq}
    lat_tiles = {"front": 512, "attn": 1024, "post": 512, "moe": 2048, "ln2": 512}

    xp, xs = x_prompt, x_sample
    new_k, new_v = [], []
    for l in range(depth):
        lam_init = 0.8 - 0.6 * math.exp(-0.3 * l)
        mod_ctx = mod_all[l, dec_batch:dec_batch + 1].reshape(1, 6, d)
        mod_lat = mod_all[l, :dec_batch].reshape(dec_batch, 6, d)
        xp, kf, vf = _trunk_layer(xp, mod_ctx, l, w, lam_init, alpha, None, None, True, ctx_tiles)
        new_k.append(kf.reshape(batch, seq, ATTN_HEADS, 2, HEAD_DIM))
        new_v.append(vf.reshape(batch, seq, ATTN_HEADS, V_DIM))
        xs, _, _ = _trunk_layer(xs, mod_lat, l, w, lam_init, alpha, rope, cache, False, lat_tiles)
    return (xp, xs, jnp.stack(new_k, axis=1), jnp.stack(new_v, axis=1))
```

```python
import functools
import math

import jax
import jax.numpy as jnp
from jax import lax
from jax.experimental import pallas as pl
from jax.experimental.pallas import tpu as pltpu

F32 = jnp.float32
BF16 = jnp.bfloat16

D_MODEL = 1024
GRID_W = 64
POOL_WIDTH = 256
POOL_WINDOWS = (2, 4, 8, 16)
POOL_GC = 64
ATTN_HEADS = 4
HEAD_DIM = 64
V_DIM = 128
ATTN_WIDTH = 512
QK_WIDTH = 512
CONV_WIDTH = 256
CONV_K = 31
IN_WIDTH = 2304
AXIS_DIM = 32
ROPE_THETA = 10000.0
N_GROUPS = 4
EXPERTS_PER_GROUP = 8
N_EXPERTS = 32
EXPERT_FF = 256
LN_EPS = 1e-5

O_A = 0
O_Q = POOL_WIDTH
O_K = O_Q + QK_WIDTH
O_V = O_K + QK_WIDTH
O_C = O_V + ATTN_WIDTH

TOP_K = 2
HALO = 16
LANES = 128
SUBLANES = 8
ROW_CHUNKS = D_MODEL // LANES
ROUTER_LANES = 128
ROUTE_BLOCK = 512
EXPERT_BLOCK = 128
BLOCKS_PER_TRIP = 2
EXPERTS_PER_STEP = 2
ATTN_ROWS = 128
LOG2E = 1.4426950408889634
NEG_BIG = -1e30
VMEM_LIMIT = 56 * 1024 * 1024


def _silu(x):
    return x * jax.nn.sigmoid(x)


def _layer_norm(x, g, b):
    mu = jnp.mean(x, axis=-1, keepdims=True)
    xc = x - mu
    var = jnp.mean(xc * xc, axis=-1, keepdims=True)
    return xc * lax.rsqrt(var + LN_EPS) * g + b


def _dot(a, b):
    return jnp.dot(a, b, preferred_element_type=F32)


def _params(sem):
    return pltpu.CompilerParams(dimension_semantics=sem, vmem_limit_bytes=VMEM_LIMIT)


def _mod_kernel(c_ref, w_ref, b_ref, o_ref):
    s = _silu(c_ref[...])
    o_ref[0] = _dot(s.astype(BF16), w_ref[0].astype(BF16)) + b_ref[0]


def _mod_call(cvec, w_mod, b_mod):
    depth, d, n6 = w_mod.shape
    rows = cvec.shape[0]
    nb = 1536
    return pl.pallas_call(
        _mod_kernel,
        grid=(depth, n6 // nb),
        in_specs=[
            pl.BlockSpec((rows, d), lambda l, j: (0, 0)),
            pl.BlockSpec((1, d, nb), lambda l, j: (l, 0, j)),
            pl.BlockSpec((1, 1, nb), lambda l, j: (l, 0, j)),
        ],
        out_specs=pl.BlockSpec((1, rows, nb), lambda l, j: (l, 0, j)),
        out_shape=jax.ShapeDtypeStruct((depth, rows, n6), F32),
        compiler_params=_params(("parallel", "parallel")),
        name="mod",
    )(cvec, w_mod, b_mod.reshape(depth, 1, n6))


def _front_kernel(*refs, seq_len, tile, has_rope, want_f32_kv):
    it = iter(refs)
    x_ref, xp_ref, xn_ref, mod_ref, win_ref = next(it), next(it), next(it), next(it), next(it)
    poolw_ref, pscale_ref, dw_ref, cg_ref, cb_ref, pw_ref = (next(it) for _ in range(6))
    if has_rope:
        cos_ref, sin_ref = next(it), next(it)
    q_ref, k_ref, v_ref = next(it), next(it), next(it)
    if want_f32_kv:
        kf_ref, vf_ref = next(it), next(it)
    yp_ref, yc_ref = next(it), next(it)

    i = pl.program_id(1)
    n_ext = tile + 2 * HALO
    shift = mod_ref[0, 0:1, :]
    scale = mod_ref[0, 1:2, :]
    xe = jnp.concatenate([xp_ref[0], x_ref[0], xn_ref[0]], axis=0)
    u = (xe * (1.0 + scale) + shift).astype(BF16)
    proj = _dot(u, win_ref[0])

    row = lax.broadcasted_iota(jnp.int32, (n_ext, 1), 0)
    gpos = i * tile - HALO + row
    valid = (gpos >= 0) & (gpos < seq_len)

    a_ext = jnp.where(valid, proj[:, O_A:O_A + POOL_WIDTH], 0.0)

    def sh(z, s):
        return pltpu.roll(z, s % n_ext, axis=0)

    p2 = a_ext + sh(a_ext, 1)
    p4 = sh(p2, 1) + sh(p2, -1)
    p8 = sh(p4, 2) + sh(p4, -2)
    p16 = sh(p8, 4) + sh(p8, -4)
    t = gpos[HALO:HALO + tile]
    lane = lax.broadcasted_iota(jnp.int32, (1, POOL_WIDTH), 1)
    mean = None
    for g, (w, p) in enumerate(zip(POOL_WINDOWS, (p2, p4, p8, p16))):
        cnt = jnp.minimum(t + w // 2, seq_len) - jnp.maximum(t - w // 2, 0)
        m = p[HALO:HALO + tile] * (1.0 / cnt.astype(F32))
        mean = m if mean is None else jnp.where(lane >= g * POOL_GC, m, mean)
    pooled = mean - a_ext[HALO:HALO + tile]
    y_pool = _dot(pooled.astype(BF16), poolw_ref[0]) * pscale_ref[0]
    yp_ref[0] = y_pool.astype(BF16)

    ca = proj[:, O_C:O_C + CONV_WIDTH]
    cb = proj[:, O_C + CONV_WIDTH:O_C + 2 * CONV_WIDTH]
    h = jnp.where(valid, ca * jax.nn.sigmoid(cb), 0.0)
    acc = None
    for r in range(SUBLANES):
        hr = h if r == 0 else pltpu.roll(h, n_ext - r, axis=0)
        for j in range(CONV_K):
            off = HALO - CONV_K // 2 + j
            if off % SUBLANES != r:
                continue
            term = hr[off - r:off - r + tile] * dw_ref[0, j:j + 1, :]
            acc = term if acc is None else acc + term
    hn = _silu(_layer_norm(acc, cg_ref[0], cb_ref[0]))
    yc_ref[0] = _dot(hn.astype(BF16), pw_ref[0]).astype(BF16)

    qf = proj[HALO:HALO + tile, O_Q:O_Q + QK_WIDTH]
    kf = proj[HALO:HALO + tile, O_K:O_K + QK_WIDTH]
    vf = proj[HALO:HALO + tile, O_V:O_V + ATTN_WIDTH]
    if want_f32_kv:
        kf_ref[0] = kf
        vf_ref[0] = vf
    if has_rope:
        cos = cos_ref[...]
        sin = sin_ref[...]
        l128 = lax.broadcasted_iota(jnp.int32, (1, LANES), 1)
        first_half = (l128 & (AXIS_DIM - 1)) < (AXIS_DIM // 2)

        def rope(z):
            outs = []
            for c in range(QK_WIDTH // LANES):
                zc = z[:, c * LANES:(c + 1) * LANES]
                partner = jnp.where(first_half,
                                    pltpu.roll(zc, LANES - AXIS_DIM // 2, axis=1),
                                    pltpu.roll(zc, AXIS_DIM // 2, axis=1))
                outs.append(zc * cos + partner * sin)
            return jnp.concatenate(outs, axis=1)

        qf = rope(qf)
        kf = rope(kf)
    q_ref[0] = (qf * (LOG2E / math.sqrt(HEAD_DIM))).astype(BF16)
    k_ref[0] = kf.astype(BF16)
    v_ref[0] = vf.astype(BF16)


def _front_call(x, mod, layer, w, rope, want_f32_kv, tile):
    b, seq_len, d = x.shape
    nt = seq_len // tile
    hb = tile // HALO
    n_hblk = seq_len // HALO
    mod_b = (lambda bi, i: (bi, 0, 0)) if mod.shape[0] == b else (lambda bi, i: (0, 0, 0))
    lw = lambda bi, i: (layer, 0, 0)
    in_specs = [
        pl.BlockSpec((1, tile, d), lambda bi, i: (bi, i, 0)),
        pl.BlockSpec((1, HALO, d), lambda bi, i: (bi, jnp.maximum(i * hb - 1, 0), 0)),
        pl.BlockSpec((1, HALO, d), lambda bi, i: (bi, jnp.minimum((i + 1) * hb, n_hblk - 1), 0)),
        pl.BlockSpec((1, 6, d), mod_b),
        pl.BlockSpec((1, d, IN_WIDTH), lw),
        pl.BlockSpec((1, POOL_WIDTH, POOL_WIDTH), lw),
        pl.BlockSpec((1, 1, POOL_WIDTH), lw),
        pl.BlockSpec((1, CONV_K, CONV_WIDTH), lw),
        pl.BlockSpec((1, 1, CONV_WIDTH), lw),
        pl.BlockSpec((1, 1, CONV_WIDTH), lw),
        pl.BlockSpec((1, CONV_WIDTH, CONV_WIDTH), lw),
    ]
    args = [x, x, x, mod, w["w_in"], w["pool_bd"], w["pool_scale"], w["conv_dw"], w["conv_ln_g"],
            w["conv_ln_b"], w["conv_pw"]]
    if rope is not None:
        in_specs += [pl.BlockSpec((tile, LANES), lambda bi, i: (i, 0))] * 2
        args += list(rope)
    tok = lambda width: pl.BlockSpec((1, tile, width), lambda bi, i: (bi, i, 0))
    out_specs = [tok(QK_WIDTH), tok(QK_WIDTH), tok(ATTN_WIDTH)]
    out_shape = [jax.ShapeDtypeStruct((b, seq_len, QK_WIDTH), BF16)] * 3
    if want_f32_kv:
        out_specs += [tok(QK_WIDTH), tok(ATTN_WIDTH)]
        out_shape += [jax.ShapeDtypeStruct((b, seq_len, QK_WIDTH), F32)] * 2
    out_specs += [tok(POOL_WIDTH), tok(CONV_WIDTH)]
    out_shape += [jax.ShapeDtypeStruct((b, seq_len, POOL_WIDTH), BF16)] * 2
    kern = functools.partial(_front_kernel, seq_len=seq_len, tile=tile, has_rope=rope is not None,
                             want_f32_kv=want_f32_kv)
    return pl.pallas_call(
        kern, grid=(b, nt), in_specs=in_specs, out_specs=out_specs, out_shape=out_shape,
        compiler_params=_params(("parallel", "parallel")), name="front",
    )(*args)


def _attn_kernel(*refs, lam_init, has_cache):
    if has_cache:
        lam_ref, g_ref, q_ref, k_ref, v_ref, ck_ref, cv_ref, o_ref = refs
    else:
        lam_ref, g_ref, q_ref, k_ref, v_ref, o_ref = refs
    lp = lam_ref[0]
    lam = (jnp.exp(jnp.sum(lp[0:1] * lp[1:2], axis=-1, keepdims=True))
           - jnp.exp(jnp.sum(lp[2:3] * lp[3:4], axis=-1, keepdims=True)) + lam_init)

    lane = lax.broadcasted_iota(jnp.int32, (1, LANES), 1)
    nt = (((1,), (1,)), ((), ()))
    keys = [k_ref[0]]
    vals = [v_ref[0]]
    if has_cache:
        keys.append(ck_ref[0, 0].astype(BF16))
        vals.append(cv_ref[0, 0].astype(BF16))

    def scores(q):
        zero = jnp.zeros_like(q)
        return [[lax.dot_general(qm, kk, nt, preferred_element_type=F32) for kk in keys]
                for qm in (jnp.where(lane < HEAD_DIM, q, zero), jnp.where(lane >= HEAD_DIM, q, zero))]

    def finish(score_maps):
        parts = []
        for s in score_maps:
            m = functools.reduce(jnp.maximum, [jnp.max(z, axis=-1, keepdims=True) for z in s])
            e = [jnp.exp2(z - m) for z in s]
            den = functools.reduce(jnp.add, [jnp.sum(z, axis=-1, keepdims=True) for z in e])
            parts.append((e, den))
        (e1, den1), (e2, den2) = parts
        c = lam * den1 / den2
        o = None
        for z1, z2, vv in zip(e1, e2, vals):
            part = _dot((z1 - z2 * c).astype(BF16), vv)
            o = part if o is None else o + part
        o = o * (1.0 / den1)
        o = o * lax.rsqrt(jnp.mean(o * o, axis=-1, keepdims=True) + LN_EPS) * g_ref[0] * (1.0 - lam_init)
        return o.astype(BF16)

    tq = q_ref.shape[1]
    n_sub = max(1, tq // ATTN_ROWS)
    step = tq // n_sub
    pending = scores(q_ref[0, 0:step, :])
    for j in range(n_sub):
        current = pending
        if j + 1 < n_sub:
            pending = scores(q_ref[0, (j + 1) * step:(j + 2) * step, :])
        o_ref[0, j * step:(j + 1) * step, :] = finish(current)


def _attn_call(q, k, v, cache, layer, w, lam_init, tq):
    b, seq_len, _ = q.shape
    lw = lambda bi, h, i: (layer, 0, 0)
    in_specs = [
        pl.BlockSpec((1, 4, HEAD_DIM), lw),
        pl.BlockSpec((1, 1, V_DIM), lw),
        pl.BlockSpec((1, tq, LANES), lambda bi, h, i: (bi, i, h)),
        pl.BlockSpec((1, seq_len, LANES), lambda bi, h, i: (bi, 0, h)),
        pl.BlockSpec((1, seq_len, V_DIM), lambda bi, h, i: (bi, 0, h)),
    ]
    args = [w["diff_lambda"], w["subln_g"], q, k, v]
    if cache is not None:
        past = cache[0].shape[2]
        cspec = pl.BlockSpec((1, 1, past, LANES), lambda bi, h, i: (bi, layer, 0, h))
        in_specs += [cspec, cspec]
        args += list(cache)
    kern = functools.partial(_attn_kernel, lam_init=lam_init, has_cache=cache is not None)
    return pl.pallas_call(
        kern, grid=(b, ATTN_HEADS, seq_len // tq), in_specs=in_specs,
        out_specs=pl.BlockSpec((1, tq, V_DIM), lambda bi, h, i: (bi, i, h)),
        out_shape=jax.ShapeDtypeStruct((b, seq_len, ATTN_WIDTH), BF16),
        compiler_params=_params(("parallel", "parallel", "parallel")), name="attn",
    )(*args)


def _post_kernel(yp_ref, ya_ref, yc_ref, x_ref, mod_ref, wout_ref, g_ref, b_ref, wr_ref, br_ref,
                 x1_ref, u2c_ref, sel_ref, *, alpha):
    o1 = POOL_WIDTH
    o2 = o1 + ATTN_WIDTH
    y = (_dot(yp_ref[0], wout_ref[0, 0:o1, :]) + _dot(ya_ref[0], wout_ref[0, o1:o2, :])
         + _dot(yc_ref[0], wout_ref[0, o2:, :]))
    gate1 = mod_ref[0, 2:3, :]
    shift2 = mod_ref[0, 3:4, :]
    scale2 = mod_ref[0, 4:5, :]
    x1 = _layer_norm(alpha * x_ref[0] + gate1 * y, g_ref[0], b_ref[0])
    x1_ref[0] = x1
    u2 = x1 * (1.0 + scale2) + shift2
    tile = u2.shape[0]
    for c in range(ROW_CHUNKS):
        u2c_ref[:, c, :, :] = u2[:, c * LANES:(c + 1) * LANES].reshape(tile // SUBLANES, SUBLANES, LANES)

    u_hi = u2.astype(BF16)
    u_lo = (u2 - u_hi.astype(F32)).astype(BF16)
    both = _dot(u_hi, wr_ref[0]) + _dot(u_lo, wr_ref[0])
    logits = both[:, :ROUTER_LANES] + both[:, ROUTER_LANES:] + br_ref[0]
    lane = lax.broadcasted_iota(jnp.int32, (1, ROUTER_LANES), 1)
    lane_f = lane.astype(F32)
    is_g = (lane >= N_EXPERTS) & (lane < N_EXPERTS + N_GROUPS)
    rmax = lambda z: jnp.max(z, axis=-1, keepdims=True)
    rmin = lambda z: jnp.min(z, axis=-1, keepdims=True)
    gl = jnp.where(is_g, logits, NEG_BIG)
    ge = jnp.where(is_g, jnp.exp(gl - rmax(gl)), 0.0)
    gp = ge / jnp.sum(ge, axis=-1, keepdims=True)
    g_w = rmax(gp)
    g_idx = rmin(jnp.where(is_g & (gp == g_w), lane_f - N_EXPERTS, 1e9))
    in_grp = (lane < N_EXPERTS) & (jnp.right_shift(lane, 3).astype(F32) == g_idx)
    el = jnp.where(in_grp, logits, NEG_BIG)
    v1 = rmax(el)
    i1 = rmin(jnp.where(in_grp & (el == v1), lane_f, 1e9))
    rest = in_grp & (lane_f != i1)
    el2 = jnp.where(rest, logits, NEG_BIG)
    v2 = rmax(el2)
    i2 = rmin(jnp.where(rest & (el2 == v2), lane_f, 1e9))
    tt = jnp.exp(v2 - v1)
    w1 = 1.0 / (1.0 + tt)
    w2 = tt / (1.0 + tt)
    sel = jnp.zeros((tile, ROUTER_LANES), F32)
    for j, col in enumerate((i1, i2, g_w * w1, g_w * w2)):
        sel = jnp.where(lane == j, col, sel)
    sel_ref[0] = sel


def _post_call(y_pool, y_attn, y_conv, x, mod, layer, w, alpha, tile):
    b, seq_len, d = x.shape
    nt = seq_len // tile
    mod_b = (lambda bi, i: (bi, 0, 0)) if mod.shape[0] == b else (lambda bi, i: (0, 0, 0))
    lw = lambda bi, i: (layer, 0, 0)
    tok = lambda width: pl.BlockSpec((1, tile, width), lambda bi, i: (bi, i, 0))
    in_specs = [tok(POOL_WIDTH), tok(ATTN_WIDTH), tok(CONV_WIDTH), tok(d),
                pl.BlockSpec((1, 6, d), mod_b),
                pl.BlockSpec((1, d, d), lw),
                pl.BlockSpec((1, 1, d), lw), pl.BlockSpec((1, 1, d), lw),
                pl.BlockSpec((1, d, 2 * ROUTER_LANES), lw), pl.BlockSpec((1, 1, ROUTER_LANES), lw)]
    out_specs = [tok(d),
                 pl.BlockSpec((tile // SUBLANES, ROW_CHUNKS, SUBLANES, LANES), lambda bi, i: (bi * nt + i, 0, 0, 0)),
                 tok(ROUTER_LANES)]
    out_shape = [jax.ShapeDtypeStruct((b, seq_len, d), F32),
                 jax.ShapeDtypeStruct((b * seq_len // SUBLANES, ROW_CHUNKS, SUBLANES, LANES), F32),
                 jax.ShapeDtypeStruct((b, seq_len, ROUTER_LANES), F32)]
    return pl.pallas_call(
        functools.partial(_post_kernel, alpha=alpha),
        grid=(b, nt), in_specs=in_specs, out_specs=out_specs, out_shape=out_shape,
        compiler_params=_params(("parallel", "parallel")), name="post",
    )(y_pool, y_attn, y_conv, x, mod, w["w_out"], w["ln1_g"], w["ln1_b"], w["router_w"], w["router_b"])


def _route_kernel(sel_ref, ri_ref, rw_ref, seg_ref, *, tile):
    lane = lax.broadcasted_iota(jnp.int32, (1, ROUTER_LANES), 1)
    lane_f = lane.astype(F32)
    blk = ROUTE_BLOCK
    r_io = lax.broadcasted_iota(jnp.int32, (blk, blk), 0)
    c_io = lax.broadcasted_iota(jnp.int32, (blk, blk), 1)
    before = jnp.where(r_io > c_io, 1.0, 0.0).astype(BF16)

    def picks(j):
        s = sel_ref[j * blk:(j + 1) * blk, :]
        return s, lane_f == s[:, 0:1], lane_f == s[:, 1:2]

    carry = jnp.zeros((1, ROUTER_LANES), F32)
    prior = []
    for j in range(tile // blk):
        _, oh1, oh2 = picks(j)
        member = jnp.where(oh1 | oh2, 1.0, 0.0)
        prior.append(_dot(before, member.astype(BF16)) + carry)
        carry = carry + jnp.sum(member, axis=0, keepdims=True)

    cnt = jnp.broadcast_to(carry, (SUBLANES, ROUTER_LANES)).astype(jnp.int32)
    padded = jnp.where(lane < N_EXPERTS, jnp.right_shift(cnt + (SUBLANES - 1), 3) * SUBLANES, 0)
    incl = padded
    step = 1
    while step < N_EXPERTS:
        incl = incl + jnp.where(lane >= step, pltpu.roll(incl, step, axis=1), 0)
        step *= 2
    start = incl - padded
    seg_ref[0] = jnp.where(lane < N_EXPERTS, cnt, pltpu.roll(start, N_EXPERTS, axis=1))
    start_f = start[0:1, :].astype(F32)

    for j in range(tile // blk):
        s, oh1, oh2 = picks(j)
        cols = []
        for oh in (oh1, oh2):
            p = jnp.sum(jnp.where(oh, start_f + prior[j], 0.0), axis=-1, keepdims=True).astype(jnp.int32)
            addr = jnp.right_shift(p, 3) * (SUBLANES * ROW_CHUNKS) + jnp.bitwise_and(p, SUBLANES - 1)
            cols.append(addr.astype(F32))
        z = jnp.where(lane == 0, cols[0], jnp.where(lane == 1, cols[1], s))
        zt = z.T
        ri_ref[0, :, j * blk:(j + 1) * blk] = zt[0:2, :].astype(jnp.int32)
        rw_ref[0, :, j * blk:(j + 1) * blk] = zt[2:4, :]


def _route_call(sel, tile):
    n_tok = sel.shape[0]
    n_tiles = n_tok // tile
    return pl.pallas_call(
        functools.partial(_route_kernel, tile=tile),
        grid=(n_tiles,),
        in_specs=[pl.BlockSpec((tile, ROUTER_LANES), lambda t: (t, 0))],
        out_specs=[pl.BlockSpec((1, TOP_K, tile), lambda t: (t, 0, 0)),
                   pl.BlockSpec((1, TOP_K, tile), lambda t: (t, 0, 0)),
                   pl.BlockSpec((1, SUBLANES, ROUTER_LANES), lambda t: (t, 0, 0))],
        out_shape=[jax.ShapeDtypeStruct((n_tiles, TOP_K, tile), jnp.int32),
                   jax.ShapeDtypeStruct((n_tiles, TOP_K, tile), F32),
                   jax.ShapeDtypeStruct((n_tiles, SUBLANES, ROUTER_LANES), jnp.int32)],
        compiler_params=_params(("parallel",)), name="route",
    )(sel)


def _row_at(start):
    return pl.ds(start, ROW_CHUNKS, stride=SUBLANES)


def _moe_kernel(ri_ref, rw_ref, seg_ref, u2c_ref, wg_ref, wu_ref, wd_ref, o_ref, sorted_ref, *, tile):
    step = pl.program_id(1)
    groups = tile // SUBLANES
    group_rows = SUBLANES * ROW_CHUNKS

    @pl.when((pl.program_id(0) == 0) & (step == 0))
    def _():
        sorted_ref[...] = jnp.zeros_like(sorted_ref)

    @pl.when(step == 0)
    def _():
        def disp_body(a, carry):
            for s in range(SUBLANES):
                row = u2c_ref[_row_at(a * group_rows + s), :]
                for k in range(TOP_K):
                    sorted_ref[_row_at(ri_ref[k * tile + a * SUBLANES + s]), :] = row
            return carry
        lax.fori_loop(0, groups, disp_body, 0)

    blk_groups = EXPERT_BLOCK // SUBLANES
    row_id = lax.broadcasted_iota(jnp.int32, (SUBLANES, 1), 0)

    def tile_at(start, g, c):
        return pl.ds(pl.multiple_of(start + (g * ROW_CHUNKS + c) * SUBLANES, SUBLANES), SUBLANES)

    experts = [step * EXPERTS_PER_STEP + j for j in range(EXPERTS_PER_STEP)]
    cnts = [seg_ref[0, 0, e] for e in experts]
    bases = [seg_ref[0, 0, N_EXPERTS + e] for e in experts]
    trip_rows = EXPERT_BLOCK * BLOCKS_PER_TRIP

    def blk_body(bi, carry):
        chains = [(j, (bi * BLOCKS_PER_TRIP + u) * EXPERT_BLOCK)
                  for j in range(EXPERTS_PER_STEP) for u in range(BLOCKS_PER_TRIP)]
        starts = [jnp.where(first < cnts[j], (bases[j] + first) * ROW_CHUNKS, 0) for j, first in chains]
        xs = [jnp.concatenate(
            [jnp.concatenate([sorted_ref[tile_at(start, g, c), :] for g in range(blk_groups)], axis=0)
             for c in range(ROW_CHUNKS)], axis=1).astype(BF16) for start in starts]
        hs = [_silu(_dot(x, wg_ref[0, j])) * _dot(x, wu_ref[0, j]) for x, (j, _) in zip(xs, chains)]
        ys = [_dot(h.astype(BF16), wd_ref[0, j]) for h, (j, _) in zip(hs, chains)]
        for (j, first), start, y in zip(chains, starts, ys):
            n_valid = cnts[j] - first
            for g in range(blk_groups):
                keep = (row_id + g * SUBLANES) < n_valid
                for c in range(ROW_CHUNKS):
                    sl = tile_at(start, g, c)
                    piece = y[g * SUBLANES:(g + 1) * SUBLANES, c * LANES:(c + 1) * LANES]
                    sorted_ref[sl, :] = jnp.where(keep, piece, sorted_ref[sl, :])
        return carry
    trips = functools.reduce(jnp.maximum, [(cnt + trip_rows - 1) // trip_rows for cnt in cnts])
    lax.fori_loop(0, trips, blk_body, 0)

    @pl.when(step == pl.num_programs(1) - 1)
    def _():
        def comb_body(a, carry):
            for s in range(SUBLANES):
                n = a * SUBLANES + s
                acc = None
                for k in range(TOP_K):
                    term = sorted_ref[_row_at(ri_ref[k * tile + n]), :] * rw_ref[k * tile + n]
                    acc = term if acc is None else acc + term
                o_ref[_row_at(a * group_rows + s), :] = acc
            return carry
        lax.fori_loop(0, groups, comb_body, 0)


def _moe_call(u2c, ri, rw, seg, layer, w, tile):
    n_tok = u2c.shape[0] // ROW_CHUNKS
    n_tiles = n_tok // tile
    d = D_MODEL
    sorted_rows = tile * TOP_K + N_EXPERTS * (SUBLANES - 1) + EXPERT_BLOCK * BLOCKS_PER_TRIP
    sorted_rows = -(-sorted_rows // SUBLANES) * SUBLANES
    eps = EXPERTS_PER_STEP
    smem = lambda shape, imap: pl.BlockSpec(shape, imap, memory_space=pltpu.SMEM)
    in_specs = [smem((TOP_K * tile,), lambda t, e: (t,)),
                smem((TOP_K * tile,), lambda t, e: (t,)),
                smem((1, SUBLANES, ROUTER_LANES), lambda t, e: (t, 0, 0)),
                pl.BlockSpec((tile * ROW_CHUNKS, LANES), lambda t, e: (t, 0), pipeline_mode=pl.Buffered(1)),
                pl.BlockSpec((1, eps, d, EXPERT_FF), lambda t, e: (layer, e, 0, 0)),
                pl.BlockSpec((1, eps, d, EXPERT_FF), lambda t, e: (layer, e, 0, 0)),
                pl.BlockSpec((1, eps, EXPERT_FF, d), lambda t, e: (layer, e, 0, 0))]
    return pl.pallas_call(
        functools.partial(_moe_kernel, tile=tile),
        grid=(n_tiles, N_EXPERTS // eps), in_specs=in_specs,
        out_specs=pl.BlockSpec((tile * ROW_CHUNKS, LANES), lambda t, e: (t, 0)),
        out_shape=jax.ShapeDtypeStruct((n_tok * ROW_CHUNKS, LANES), F32),
        scratch_shapes=[pltpu.VMEM((sorted_rows * ROW_CHUNKS, LANES), F32)],
        compiler_params=_params(("arbitrary", "arbitrary")), name="moe",
    )(ri.reshape(-1), rw.reshape(-1), seg, u2c, w["w_gate"], w["w_up"], w["w_down"])


def _ln2_kernel(m_ref, x1_ref, mod_ref, g_ref, b_ref, o_ref, *, alpha):
    tile = x1_ref.shape[0]
    moe = jnp.concatenate([m_ref[:, c, :, :].reshape(tile, LANES) for c in range(ROW_CHUNKS)], axis=1)
    gate2 = mod_ref[0, 5:6, :]
    o_ref[...] = _layer_norm(alpha * x1_ref[...] + gate2 * moe, g_ref[0], b_ref[0])


def _ln2_call(moe_c, x1, mod, layer, w, alpha, tile):
    b, seq_len, d = x1.shape
    n_tok = b * seq_len
    per_seq = seq_len // tile
    mod_b = (lambda t: (t // per_seq, 0, 0)) if mod.shape[0] == b else (lambda t: (0, 0, 0))
    lw = lambda t: (layer, 0, 0)
    out = pl.pallas_call(
        functools.partial(_ln2_kernel, alpha=alpha),
        grid=(n_tok // tile,),
        in_specs=[pl.BlockSpec((tile // SUBLANES, ROW_CHUNKS, SUBLANES, LANES), lambda t: (t, 0, 0, 0)),
                  pl.BlockSpec((tile, d), lambda t: (t, 0)),
                  pl.BlockSpec((1, 6, d), mod_b),
                  pl.BlockSpec((1, 1, d), lw), pl.BlockSpec((1, 1, d), lw)],
        out_specs=pl.BlockSpec((tile, d), lambda t: (t, 0)),
        out_shape=jax.ShapeDtypeStruct((n_tok, d), F32),
        compiler_params=_params(("parallel",)), name="ln2",
    )(moe_c.reshape(n_tok // SUBLANES, ROW_CHUNKS, SUBLANES, LANES), x1.reshape(n_tok, d), mod,
      w["ln2_g"], w["ln2_b"])
    return out.reshape(b, seq_len, d)


def _rope_tables(seq_len):
    rows = seq_len // GRID_W
    row_id = jnp.repeat(jnp.arange(rows), GRID_W).astype(F32)
    col_id = jnp.tile(jnp.arange(GRID_W), rows).astype(F32)
    inv = ROPE_THETA ** (-jnp.arange(0, AXIS_DIM, 2, dtype=F32) / AXIS_DIM)
    ang = jnp.stack([row_id[:, None] * inv, col_id[:, None] * inv], axis=1)
    cos = jnp.cos(ang)[:, :, None, :]
    sin = jnp.sin(ang)[:, :, None, :]
    shape = (seq_len, 2, 2, AXIS_DIM // 2)
    cos = jnp.broadcast_to(cos, shape).reshape(seq_len, HEAD_DIM)
    sin = (jnp.broadcast_to(sin, shape) * jnp.array([-1.0, 1.0], F32)[None, None, :, None]).reshape(seq_len, HEAD_DIM)
    return jnp.tile(cos, (1, LANES // HEAD_DIM)), jnp.tile(sin, (1, LANES // HEAD_DIM))


def _trunk_layer(x, mod, layer, w, lam_init, alpha, rope, cache, want_f32_kv, tiles):
    outs = _front_call(x, mod, layer, w, rope, want_f32_kv, tiles["front"])
    if want_f32_kv:
        q, k, v, kf, vf, y_pool, y_conv = outs
    else:
        q, k, v, y_pool, y_conv = outs
        kf = vf = None
    y_attn = _attn_call(q, k, v, cache, layer, w, lam_init, tiles["attn"])
    x1, u2c, sel = _post_call(y_pool, y_attn, y_conv, x, mod, layer, w, alpha, tiles["post"])
    ri, rw, seg = _route_call(sel.reshape(-1, ROUTER_LANES), tiles["moe"])
    moe_c = _moe_call(u2c.reshape(-1, LANES), ri, rw, seg, layer, w, tiles["moe"])
    x2 = _ln2_call(moe_c, x1, mod, layer, w, alpha, tiles["ln2"])
    return x2, kf, vf


def kernel(x_prompt, x_sample, cache_k, cache_v, c, c_ctx, w_mod, b_mod, w_in, w_out, pool_w, pool_scale, diff_lambda, subln_g, conv_dw, conv_ln_g, conv_ln_b, conv_pw, ln1_g, ln1_b, ln2_g, ln2_b, router_grp, router_grp_b, router_exp, router_exp_b, w_gate, w_up, w_down):
    depth = w_in.shape[0]
    batch, seq, d = x_prompt.shape
    dec_batch, dec_seq, _ = x_sample.shape
    past = cache_k.shape[2]
    alpha = (2 * depth) ** 0.25

    mod_rows = 16
    cvec = jnp.concatenate([c, c_ctx[None], jnp.zeros((mod_rows - dec_batch - 1, d), F32)], axis=0)
    mod_all = _mod_call(cvec, w_mod, b_mod)

    eye = jnp.eye(len(POOL_WINDOWS), dtype=F32)
    pool_bd = (pool_w[:, :, :, None, :] * eye[None, :, None, :, None]).reshape(depth, POOL_WIDTH, POOL_WIDTH)
    router_w = jnp.concatenate(
        [router_exp, router_grp, jnp.zeros((depth, d, ROUTER_LANES - N_EXPERTS - N_GROUPS), F32)], axis=-1)
    router_hi = router_w.astype(BF16)
    router_lo = (router_w - router_hi.astype(F32)).astype(BF16)
    router_b = jnp.concatenate(
        [router_exp_b, router_grp_b, jnp.zeros((depth, ROUTER_LANES - N_EXPERTS - N_GROUPS), F32)], axis=-1)
    r3 = lambda z: z.reshape(depth, 1, z.shape[-1])
    w = {
        "w_in": w_in.astype(BF16), "w_out": w_out.astype(BF16), "pool_bd": pool_bd.astype(BF16),
        "pool_scale": r3(pool_scale), "diff_lambda": diff_lambda, "subln_g": r3(subln_g),
        "conv_dw": conv_dw, "conv_ln_g": r3(conv_ln_g), "conv_ln_b": r3(conv_ln_b),
        "conv_pw": conv_pw.astype(BF16), "ln1_g": r3(ln1_g), "ln1_b": r3(ln1_b), "ln2_g": r3(ln2_g),
        "ln2_b": r3(ln2_b), "router_w": jnp.concatenate([router_hi, router_lo], axis=-1), "router_b": r3(router_b),
        "w_gate": w_gate.astype(BF16), "w_up": w_up.astype(BF16), "w_down": w_down.astype(BF16),
    }
    rope = _rope_tables(dec_seq)
    cache = (cache_k.reshape(dec_batch, depth, past, QK_WIDTH), cache_v.reshape(dec_batch, depth, past, ATTN_WIDTH))
    ctx_tiles = {"front": seq, "attn": seq, "post": seq, "moe": 2048, "ln2": seq}
    lat_tiles = {"front": 512, "attn": 2048, "post": 512, "moe": 2048, "ln2": 512}

    xp, xs = x_prompt, x_sample
    new_k, new_v = [], []
    for l in range(depth):
        lam_init = 0.8 - 0.6 * math.exp(-0.3 * l)
        mod_ctx = mod_all[l, dec_batch:dec_batch + 1].reshape(1, 6, d)
        mod_lat = mod_all[l, :dec_batch].reshape(dec_batch, 6, d)
        xp, kf, vf = _trunk_layer(xp, mod_ctx, l, w, lam_init, alpha, None, None, True, ctx_tiles)
        new_k.append(kf.reshape(batch, seq, ATTN_HEADS, 2, HEAD_DIM))
        new_v.append(vf.reshape(batch, seq, ATTN_HEADS, V_DIM))
        xs, _, _ = _trunk_layer(xs, mod_lat, l, w, lam_init, alpha, rope, cache, False, lat_tiles)
    return (xp, xs, jnp.stack(new_k, axis=1), jnp.stack(new_v, axis=1))
```

```python
import functools
import math

import jax
import jax.numpy as jnp
from jax import lax
from jax.experimental import pallas as pl
from jax.experimental.pallas import tpu as pltpu

F32 = jnp.float32
BF16 = jnp.bfloat16

D_MODEL = 1024
GRID_W = 64
POOL_WIDTH = 256
POOL_WINDOWS = (2, 4, 8, 16)
POOL_GC = 64
ATTN_HEADS = 4
HEAD_DIM = 64
V_DIM = 128
ATTN_WIDTH = 512
QK_WIDTH = 512
CONV_WIDTH = 256
CONV_K = 31
IN_WIDTH = 2304
AXIS_DIM = 32
ROPE_THETA = 10000.0
N_GROUPS = 4
EXPERTS_PER_GROUP = 8
N_EXPERTS = 32
EXPERT_FF = 256
LN_EPS = 1e-5

O_A = 0
O_Q = POOL_WIDTH
O_K = O_Q + QK_WIDTH
O_V = O_K + QK_WIDTH
O_C = O_V + ATTN_WIDTH

TOP_K = 2
HALO = 16
LANES = 128
SUBLANES = 8
ROW_CHUNKS = D_MODEL // LANES
ROUTER_LANES = 128
ROUTE_BLOCK = 512
EXPERT_BLOCK = 128
BLOCKS_PER_TRIP = 2
EXPERTS_PER_STEP = 2
ATTN_ROWS = 128
LOG2E = 1.4426950408889634
NEG_BIG = -1e30
VMEM_LIMIT = 56 * 1024 * 1024


def _silu(x):
    return x * jax.nn.sigmoid(x)


def _layer_norm(x, g, b):
    mu = jnp.mean(x, axis=-1, keepdims=True)
    xc = x - mu
    var = jnp.mean(xc * xc, axis=-1, keepdims=True)
    return xc * lax.rsqrt(var + LN_EPS) * g + b


def _dot(a, b):
    return jnp.dot(a, b, preferred_element_type=F32)


def _params(sem):
    return pltpu.CompilerParams(dimension_semantics=sem, vmem_limit_bytes=VMEM_LIMIT)


def _mod_kernel(c_ref, w_ref, b_ref, o_ref):
    s = _silu(c_ref[...])
    o_ref[0] = _dot(s.astype(BF16), w_ref[0].astype(BF16)) + b_ref[0]


def _mod_call(cvec, w_mod, b_mod):
    depth, d, n6 = w_mod.shape
    rows = cvec.shape[0]
    nb = 1536
    return pl.pallas_call(
        _mod_kernel,
        grid=(depth, n6 // nb),
        in_specs=[
            pl.BlockSpec((rows, d), lambda l, j: (0, 0)),
            pl.BlockSpec((1, d, nb), lambda l, j: (l, 0, j)),
            pl.BlockSpec((1, 1, nb), lambda l, j: (l, 0, j)),
        ],
        out_specs=pl.BlockSpec((1, rows, nb), lambda l, j: (l, 0, j)),
        out_shape=jax.ShapeDtypeStruct((depth, rows, n6), F32),
        compiler_params=_params(("parallel", "parallel")),
        name="mod",
    )(cvec, w_mod, b_mod.reshape(depth, 1, n6))


def _front_kernel(*refs, seq_len, tile, has_rope, want_f32_kv, has_prev_kv):
    it = iter(refs)
    x_ref, xp_ref, xn_ref, mod_ref, win_ref = next(it), next(it), next(it), next(it), next(it)
    poolw_ref, pscale_ref, dw_ref, cg_ref, cb_ref, pw_ref = (next(it) for _ in range(6))
    if has_rope:
        cos_ref, sin_ref = next(it), next(it)
    if has_prev_kv:
        next(it), next(it)
    q_ref, k_ref, v_ref = next(it), next(it), next(it)
    if want_f32_kv:
        kf_ref, vf_ref = next(it), next(it)
    yp_ref, yc_ref = next(it), next(it)

    i = pl.program_id(1)
    n_ext = tile + 2 * HALO
    shift = mod_ref[0, 0:1, :]
    scale = mod_ref[0, 1:2, :]
    xe = jnp.concatenate([xp_ref[0], x_ref[0], xn_ref[0]], axis=0)
    u = (xe * (1.0 + scale) + shift).astype(BF16)
    proj = _dot(u, win_ref[0])

    row = lax.broadcasted_iota(jnp.int32, (n_ext, 1), 0)
    gpos = i * tile - HALO + row
    valid = (gpos >= 0) & (gpos < seq_len)

    a_ext = jnp.where(valid, proj[:, O_A:O_A + POOL_WIDTH], 0.0)

    def sh(z, s):
        return pltpu.roll(z, s % n_ext, axis=0)

    p2 = a_ext + sh(a_ext, 1)
    p4 = sh(p2, 1) + sh(p2, -1)
    p8 = sh(p4, 2) + sh(p4, -2)
    p16 = sh(p8, 4) + sh(p8, -4)
    t = gpos[HALO:HALO + tile]
    lane = lax.broadcasted_iota(jnp.int32, (1, POOL_WIDTH), 1)
    mean = None
    for g, (w, p) in enumerate(zip(POOL_WINDOWS, (p2, p4, p8, p16))):
        cnt = jnp.minimum(t + w // 2, seq_len) - jnp.maximum(t - w // 2, 0)
        m = p[HALO:HALO + tile] * (1.0 / cnt.astype(F32))
        mean = m if mean is None else jnp.where(lane >= g * POOL_GC, m, mean)
    pooled = mean - a_ext[HALO:HALO + tile]
    y_pool = _dot(pooled.astype(BF16), poolw_ref[0]) * pscale_ref[0]
    yp_ref[0] = y_pool.astype(BF16)

    ca = proj[:, O_C:O_C + CONV_WIDTH]
    cb = proj[:, O_C + CONV_WIDTH:O_C + 2 * CONV_WIDTH]
    h = jnp.where(valid, ca * jax.nn.sigmoid(cb), 0.0)
    acc = None
    for r in range(SUBLANES):
        hr = h if r == 0 else pltpu.roll(h, n_ext - r, axis=0)
        for j in range(CONV_K):
            off = HALO - CONV_K // 2 + j
            if off % SUBLANES != r:
                continue
            term = hr[off - r:off - r + tile] * dw_ref[0, j:j + 1, :]
            acc = term if acc is None else acc + term
    hn = _silu(_layer_norm(acc, cg_ref[0], cb_ref[0]))
    yc_ref[0] = _dot(hn.astype(BF16), pw_ref[0]).astype(BF16)

    qf = proj[HALO:HALO + tile, O_Q:O_Q + QK_WIDTH]
    kf = proj[HALO:HALO + tile, O_K:O_K + QK_WIDTH]
    vf = proj[HALO:HALO + tile, O_V:O_V + ATTN_WIDTH]
    if want_f32_kv:
        kf_ref[0, 0] = kf
        vf_ref[0, 0] = vf
    if has_rope:
        cos = cos_ref[...]
        sin = sin_ref[...]
        l128 = lax.broadcasted_iota(jnp.int32, (1, LANES), 1)
        first_half = (l128 & (AXIS_DIM - 1)) < (AXIS_DIM // 2)

        def rope(z):
            outs = []
            for c in range(QK_WIDTH // LANES):
                zc = z[:, c * LANES:(c + 1) * LANES]
                partner = jnp.where(first_half,
                                    pltpu.roll(zc, LANES - AXIS_DIM // 2, axis=1),
                                    pltpu.roll(zc, AXIS_DIM // 2, axis=1))
                outs.append(zc * cos + partner * sin)
            return jnp.concatenate(outs, axis=1)

        qf = rope(qf)
        kf = rope(kf)
    q_ref[0] = (qf * (LOG2E / math.sqrt(HEAD_DIM))).astype(BF16)
    k_ref[0] = kf.astype(BF16)
    v_ref[0] = vf.astype(BF16)


def _front_call(x, mod, layer, w, rope, new_kv, tile):
    want_f32_kv = new_kv is not None
    depth, prev_kv = new_kv if want_f32_kv else (None, None)
    b, seq_len, d = x.shape
    nt = seq_len // tile
    hb = tile // HALO
    n_hblk = seq_len // HALO
    mod_b = (lambda bi, i: (bi, 0, 0)) if mod.shape[0] == b else (lambda bi, i: (0, 0, 0))
    lw = lambda bi, i: (layer, 0, 0)
    in_specs = [
        pl.BlockSpec((1, tile, d), lambda bi, i: (bi, i, 0)),
        pl.BlockSpec((1, HALO, d), lambda bi, i: (bi, jnp.maximum(i * hb - 1, 0), 0)),
        pl.BlockSpec((1, HALO, d), lambda bi, i: (bi, jnp.minimum((i + 1) * hb, n_hblk - 1), 0)),
        pl.BlockSpec((1, 6, d), mod_b),
        pl.BlockSpec((1, d, IN_WIDTH), lw),
        pl.BlockSpec((1, POOL_WIDTH, POOL_WIDTH), lw),
        pl.BlockSpec((1, 1, POOL_WIDTH), lw),
        pl.BlockSpec((1, CONV_K, CONV_WIDTH), lw),
        pl.BlockSpec((1, 1, CONV_WIDTH), lw),
        pl.BlockSpec((1, 1, CONV_WIDTH), lw),
        pl.BlockSpec((1, CONV_WIDTH, CONV_WIDTH), lw),
    ]
    args = [x, x, x, mod, w["w_in"], w["pool_bd"], w["pool_scale"], w["conv_dw"], w["conv_ln_g"],
            w["conv_ln_b"], w["conv_pw"]]
    if rope is not None:
        in_specs += [pl.BlockSpec((tile, LANES), lambda bi, i: (i, 0))] * 2
        args += list(rope)
    tok = lambda width: pl.BlockSpec((1, tile, width), lambda bi, i: (bi, i, 0))
    out_specs = [tok(QK_WIDTH), tok(QK_WIDTH), tok(ATTN_WIDTH)]
    out_shape = [jax.ShapeDtypeStruct((b, seq_len, QK_WIDTH), BF16)] * 3
    aliases = {}
    if want_f32_kv:
        if prev_kv is not None:
            aliases = {len(args): len(out_specs), len(args) + 1: len(out_specs) + 1}
            in_specs += [pl.BlockSpec(memory_space=pl.ANY)] * 2
            args += list(prev_kv)
        layer_tok = lambda width: pl.BlockSpec((1, 1, tile, width), lambda bi, i: (bi, layer, i, 0))
        out_specs += [layer_tok(QK_WIDTH), layer_tok(ATTN_WIDTH)]
        out_shape += [jax.ShapeDtypeStruct((b, depth, seq_len, QK_WIDTH), F32)] * 2
    out_specs += [tok(POOL_WIDTH), tok(CONV_WIDTH)]
    out_shape += [jax.ShapeDtypeStruct((b, seq_len, POOL_WIDTH), BF16)] * 2
    kern = functools.partial(_front_kernel, seq_len=seq_len, tile=tile, has_rope=rope is not None,
                             want_f32_kv=want_f32_kv, has_prev_kv=prev_kv is not None)
    return pl.pallas_call(
        kern, grid=(b, nt), in_specs=in_specs, out_specs=out_specs, out_shape=out_shape,
        input_output_aliases=aliases,
        compiler_params=_params(("parallel", "parallel")), name="front",
    )(*args)


def _attn_kernel(*refs, lam_init, has_cache):
    if has_cache:
        lam_ref, g_ref, q_ref, k_ref, v_ref, ck_ref, cv_ref, o_ref = refs
    else:
        lam_ref, g_ref, q_ref, k_ref, v_ref, o_ref = refs
    lp = lam_ref[0]
    lam = (jnp.exp(jnp.sum(lp[0:1] * lp[1:2], axis=-1, keepdims=True))
           - jnp.exp(jnp.sum(lp[2:3] * lp[3:4], axis=-1, keepdims=True)) + lam_init)

    lane = lax.broadcasted_iota(jnp.int32, (1, LANES), 1)
    nt = (((1,), (1,)), ((), ()))
    heads = q_ref.shape[2] // LANES
    kv = []
    for hh in range(heads):
        cols = slice(hh * LANES, (hh + 1) * LANES)
        keys = [k_ref[0, :, cols]]
        vals = [v_ref[0, :, cols]]
        if has_cache:
            keys.append(ck_ref[0, 0, :, cols].astype(BF16))
            vals.append(cv_ref[0, 0, :, cols].astype(BF16))
        kv.append((keys, vals))

    def scores(q, keys):
        zero = jnp.zeros_like(q)
        return [[lax.dot_general(qm, kk, nt, preferred_element_type=F32) for kk in keys]
                for qm in (jnp.where(lane < HEAD_DIM, q, zero), jnp.where(lane >= HEAD_DIM, q, zero))]

    def finish(score_maps, vals):
        parts = []
        for s in score_maps:
            m = functools.reduce(jnp.maximum, [jnp.max(z, axis=-1, keepdims=True) for z in s])
            e = [jnp.exp2(z - m) for z in s]
            den = functools.reduce(jnp.add, [jnp.sum(z, axis=-1, keepdims=True) for z in e])
            parts.append((e, den))
        (e1, den1), (e2, den2) = parts
        c = lam * den1 / den2
        o = None
        for z1, z2, vv in zip(e1, e2, vals):
            part = _dot((z1 - z2 * c).astype(BF16), vv)
            o = part if o is None else o + part
        o = o * (1.0 / den1)
        o = o * lax.rsqrt(jnp.mean(o * o, axis=-1, keepdims=True) + LN_EPS) * g_ref[0] * (1.0 - lam_init)
        return o.astype(BF16)

    tq = q_ref.shape[1]
    n_sub = max(1, tq // ATTN_ROWS)
    step = tq // n_sub
    items = [(hh, j) for hh in range(heads) for j in range(n_sub)]

    def item_scores(item):
        hh, j = item
        return scores(q_ref[0, j * step:(j + 1) * step, hh * LANES:(hh + 1) * LANES], kv[hh][0])

    pending = item_scores(items[0])
    for idx, (hh, j) in enumerate(items):
        current = pending
        if idx + 1 < len(items):
            pending = item_scores(items[idx + 1])
        o_ref[0, j * step:(j + 1) * step, hh * LANES:(hh + 1) * LANES] = finish(current, kv[hh][1])


def _attn_call(q, k, v, cache, layer, w, lam_init, tq, heads):
    b, seq_len, _ = q.shape
    width = heads * LANES
    lw = lambda bi, h, i: (layer, 0, 0)
    in_specs = [
        pl.BlockSpec((1, 4, HEAD_DIM), lw),
        pl.BlockSpec((1, 1, V_DIM), lw),
        pl.BlockSpec((1, tq, width), lambda bi, h, i: (bi, i, h)),
        pl.BlockSpec((1, seq_len, width), lambda bi, h, i: (bi, 0, h)),
        pl.BlockSpec((1, seq_len, width), lambda bi, h, i: (bi, 0, h)),
    ]
    args = [w["diff_lambda"], w["subln_g"], q, k, v]
    if cache is not None:
        past = cache[0].shape[2]
        cspec = pl.BlockSpec((1, 1, past, width), lambda bi, h, i: (bi, layer, 0, h))
        in_specs += [cspec, cspec]
        args += list(cache)
    kern = functools.partial(_attn_kernel, lam_init=lam_init, has_cache=cache is not None)
    return pl.pallas_call(
        kern, grid=(b, ATTN_HEADS // heads, seq_len // tq), in_specs=in_specs,
        out_specs=pl.BlockSpec((1, tq, width), lambda bi, h, i: (bi, i, h)),
        out_shape=jax.ShapeDtypeStruct((b, seq_len, ATTN_WIDTH), BF16),
        compiler_params=_params(("parallel", "parallel", "parallel")), name="attn",
    )(*args)


def _post_kernel(yp_ref, ya_ref, yc_ref, x_ref, mod_ref, wout_ref, g_ref, b_ref, wr_ref, br_ref,
                 x1_ref, u2c_ref, sel_ref, *, alpha):
    o1 = POOL_WIDTH
    o2 = o1 + ATTN_WIDTH
    y = (_dot(yp_ref[0], wout_ref[0, 0:o1, :]) + _dot(ya_ref[0], wout_ref[0, o1:o2, :])
         + _dot(yc_ref[0], wout_ref[0, o2:, :]))
    gate1 = mod_ref[0, 2:3, :]
    shift2 = mod_ref[0, 3:4, :]
    scale2 = mod_ref[0, 4:5, :]
    x1 = _layer_norm(alpha * x_ref[0] + gate1 * y, g_ref[0], b_ref[0])
    x1_ref[0] = x1
    u2 = x1 * (1.0 + scale2) + shift2
    tile = u2.shape[0]
    for c in range(ROW_CHUNKS):
        u2c_ref[:, c, :, :] = u2[:, c * LANES:(c + 1) * LANES].reshape(tile // SUBLANES, SUBLANES, LANES)

    u_hi = u2.astype(BF16)
    u_lo = (u2 - u_hi.astype(F32)).astype(BF16)
    both = _dot(u_hi, wr_ref[0]) + _dot(u_lo, wr_ref[0])
    logits = both[:, :ROUTER_LANES] + both[:, ROUTER_LANES:] + br_ref[0]
    lane = lax.broadcasted_iota(jnp.int32, (1, ROUTER_LANES), 1)
    lane_f = lane.astype(F32)
    is_g = (lane >= N_EXPERTS) & (lane < N_EXPERTS + N_GROUPS)
    rmax = lambda z: jnp.max(z, axis=-1, keepdims=True)
    rmin = lambda z: jnp.min(z, axis=-1, keepdims=True)
    gl = jnp.where(is_g, logits, NEG_BIG)
    ge = jnp.where(is_g, jnp.exp(gl - rmax(gl)), 0.0)
    gp = ge / jnp.sum(ge, axis=-1, keepdims=True)
    g_w = rmax(gp)
    g_idx = rmin(jnp.where(is_g & (gp == g_w), lane_f - N_EXPERTS, 1e9))
    in_grp = (lane < N_EXPERTS) & (jnp.right_shift(lane, 3).astype(F32) == g_idx)
    el = jnp.where(in_grp, logits, NEG_BIG)
    v1 = rmax(el)
    i1 = rmin(jnp.where(in_grp & (el == v1), lane_f, 1e9))
    rest = in_grp & (lane_f != i1)
    el2 = jnp.where(rest, logits, NEG_BIG)
    v2 = rmax(el2)
    i2 = rmin(jnp.where(rest & (el2 == v2), lane_f, 1e9))
    tt = jnp.exp(v2 - v1)
    w1 = 1.0 / (1.0 + tt)
    w2 = tt / (1.0 + tt)
    sel = jnp.zeros((tile, ROUTER_LANES), F32)
    for j, col in enumerate((i1, i2, g_w * w1, g_w * w2)):
        sel = jnp.where(lane == j, col, sel)
    sel_ref[0] = sel


def _post_call(y_pool, y_attn, y_conv, x, mod, layer, w, alpha, tile):
    b, seq_len, d = x.shape
    nt = seq_len // tile
    mod_b = (lambda bi, i: (bi, 0, 0)) if mod.shape[0] == b else (lambda bi, i: (0, 0, 0))
    lw = lambda bi, i: (layer, 0, 0)
    tok = lambda width: pl.BlockSpec((1, tile, width), lambda bi, i: (bi, i, 0))
    in_specs = [tok(POOL_WIDTH), tok(ATTN_WIDTH), tok(CONV_WIDTH), tok(d),
                pl.BlockSpec((1, 6, d), mod_b),
                pl.BlockSpec((1, d, d), lw),
                pl.BlockSpec((1, 1, d), lw), pl.BlockSpec((1, 1, d), lw),
                pl.BlockSpec((1, d, 2 * ROUTER_LANES), lw), pl.BlockSpec((1, 1, ROUTER_LANES), lw)]
    out_specs = [tok(d),
                 pl.BlockSpec((tile // SUBLANES, ROW_CHUNKS, SUBLANES, LANES), lambda bi, i: (bi * nt + i, 0, 0, 0)),
                 tok(ROUTER_LANES)]
    out_shape = [jax.ShapeDtypeStruct((b, seq_len, d), F32),
                 jax.ShapeDtypeStruct((b * seq_len // SUBLANES, ROW_CHUNKS, SUBLANES, LANES), F32),
                 jax.ShapeDtypeStruct((b, seq_len, ROUTER_LANES), F32)]
    return pl.pallas_call(
        functools.partial(_post_kernel, alpha=alpha),
        grid=(b, nt), in_specs=in_specs, out_specs=out_specs, out_shape=out_shape,
        compiler_params=_params(("parallel", "parallel")), name="post",
    )(y_pool, y_attn, y_conv, x, mod, w["w_out"], w["ln1_g"], w["ln1_b"], w["router_w"], w["router_b"])


def _route_kernel(sel_ref, ri_ref, rw_ref, seg_ref, *, tile):
    lane = lax.broadcasted_iota(jnp.int32, (1, ROUTER_LANES), 1)
    lane_f = lane.astype(F32)
    blk = ROUTE_BLOCK
    r_io = lax.broadcasted_iota(jnp.int32, (blk, blk), 0)
    c_io = lax.broadcasted_iota(jnp.int32, (blk, blk), 1)
    before = jnp.where(r_io > c_io, 1.0, 0.0).astype(BF16)

    def picks(j):
        s = sel_ref[j * blk:(j + 1) * blk, :]
        return s, lane_f == s[:, 0:1], lane_f == s[:, 1:2]

    carry = jnp.zeros((1, ROUTER_LANES), F32)
    prior = []
    for j in range(tile // blk):
        _, oh1, oh2 = picks(j)
        member = jnp.where(oh1 | oh2, 1.0, 0.0)
        prior.append(_dot(before, member.astype(BF16)) + carry)
        carry = carry + jnp.sum(member, axis=0, keepdims=True)

    cnt = jnp.broadcast_to(carry, (SUBLANES, ROUTER_LANES)).astype(jnp.int32)
    padded = jnp.where(lane < N_EXPERTS, jnp.right_shift(cnt + (SUBLANES - 1), 3) * SUBLANES, 0)
    incl = padded
    step = 1
    while step < N_EXPERTS:
        incl = incl + jnp.where(lane >= step, pltpu.roll(incl, step, axis=1), 0)
        step *= 2
    start = incl - padded
    seg_ref[0] = jnp.where(lane < N_EXPERTS, cnt, pltpu.roll(start, N_EXPERTS, axis=1))
    start_f = start[0:1, :].astype(F32)

    for j in range(tile // blk):
        s, oh1, oh2 = picks(j)
        cols = []
        for oh in (oh1, oh2):
            p = jnp.sum(jnp.where(oh, start_f + prior[j], 0.0), axis=-1, keepdims=True).astype(jnp.int32)
            addr = jnp.right_shift(p, 3) * (SUBLANES * ROW_CHUNKS) + jnp.bitwise_and(p, SUBLANES - 1)
            cols.append(addr.astype(F32))
        z = jnp.where(lane == 0, cols[0], jnp.where(lane == 1, cols[1], s))
        zt = z.T
        ri_ref[0, :, j * blk:(j + 1) * blk] = zt[0:2, :].astype(jnp.int32)
        rw_ref[0, :, j * blk:(j + 1) * blk] = zt[2:4, :]


def _route_call(sel, tile):
    n_tok = sel.shape[0]
    n_tiles = n_tok // tile
    return pl.pallas_call(
        functools.partial(_route_kernel, tile=tile),
        grid=(n_tiles,),
        in_specs=[pl.BlockSpec((tile, ROUTER_LANES), lambda t: (t, 0))],
        out_specs=[pl.BlockSpec((1, TOP_K, tile), lambda t: (t, 0, 0)),
                   pl.BlockSpec((1, TOP_K, tile), lambda t: (t, 0, 0)),
                   pl.BlockSpec((1, SUBLANES, ROUTER_LANES), lambda t: (t, 0, 0))],
        out_shape=[jax.ShapeDtypeStruct((n_tiles, TOP_K, tile), jnp.int32),
                   jax.ShapeDtypeStruct((n_tiles, TOP_K, tile), F32),
                   jax.ShapeDtypeStruct((n_tiles, SUBLANES, ROUTER_LANES), jnp.int32)],
        compiler_params=_params(("parallel",)), name="route",
    )(sel)


def _row_at(start):
    return pl.ds(start, ROW_CHUNKS, stride=SUBLANES)


def _moe_kernel(ri_ref, rw_ref, seg_ref, u2c_ref, wg_ref, wu_ref, wd_ref, o_ref, sorted_ref, *, tile):
    step = pl.program_id(1)
    groups = tile // SUBLANES
    group_rows = SUBLANES * ROW_CHUNKS

    @pl.when((pl.program_id(0) == 0) & (step == 0))
    def _():
        sorted_ref[...] = jnp.zeros_like(sorted_ref)

    @pl.when(step == 0)
    def _():
        def disp_body(a, carry):
            for s in range(SUBLANES):
                row = u2c_ref[_row_at(a * group_rows + s), :]
                for k in range(TOP_K):
                    sorted_ref[_row_at(ri_ref[k * tile + a * SUBLANES + s]), :] = row
            return carry
        lax.fori_loop(0, groups, disp_body, 0)

    blk_groups = EXPERT_BLOCK // SUBLANES
    row_id = lax.broadcasted_iota(jnp.int32, (SUBLANES, 1), 0)

    def tile_at(start, g, c):
        return pl.ds(pl.multiple_of(start + (g * ROW_CHUNKS + c) * SUBLANES, SUBLANES), SUBLANES)

    experts = [step * EXPERTS_PER_STEP + j for j in range(EXPERTS_PER_STEP)]
    cnts = [seg_ref[0, 0, e] for e in experts]
    bases = [seg_ref[0, 0, N_EXPERTS + e] for e in experts]
    trip_rows = EXPERT_BLOCK * BLOCKS_PER_TRIP

    def blk_body(bi, carry):
        chains = [(j, (bi * BLOCKS_PER_TRIP + u) * EXPERT_BLOCK)
                  for j in range(EXPERTS_PER_STEP) for u in range(BLOCKS_PER_TRIP)]
        starts = [jnp.where(first < cnts[j], (bases[j] + first) * ROW_CHUNKS, 0) for j, first in chains]
        xs = [jnp.concatenate(
            [jnp.concatenate([sorted_ref[tile_at(start, g, c), :] for g in range(blk_groups)], axis=0)
             for c in range(ROW_CHUNKS)], axis=1).astype(BF16) for start in starts]
        hs = [_silu(_dot(x, wg_ref[0, j])) * _dot(x, wu_ref[0, j]) for x, (j, _) in zip(xs, chains)]
        ys = [_dot(h.astype(BF16), wd_ref[0, j]) for h, (j, _) in zip(hs, chains)]
        for (j, first), start, y in zip(chains, starts, ys):
            n_valid = cnts[j] - first
            for g in range(blk_groups):
                keep = (row_id + g * SUBLANES) < n_valid
                for c in range(ROW_CHUNKS):
                    sl = tile_at(start, g, c)
                    piece = y[g * SUBLANES:(g + 1) * SUBLANES, c * LANES:(c + 1) * LANES]
                    sorted_ref[sl, :] = jnp.where(keep, piece, sorted_ref[sl, :])
        return carry
    trips = functools.reduce(jnp.maximum, [(cnt + trip_rows - 1) // trip_rows for cnt in cnts])
    lax.fori_loop(0, trips, blk_body, 0)

    @pl.when(step == pl.num_programs(1) - 1)
    def _():
        def comb_body(a, carry):
            for s in range(SUBLANES):
                n = a * SUBLANES + s
                acc = None
                for k in range(TOP_K):
                    term = sorted_ref[_row_at(ri_ref[k * tile + n]), :] * rw_ref[k * tile + n]
                    acc = term if acc is None else acc + term
                o_ref[_row_at(a * group_rows + s), :] = acc
            return carry
        lax.fori_loop(0, groups, comb_body, 0)


def _moe_call(u2c, ri, rw, seg, layer, w, tile):
    n_tok = u2c.shape[0] // ROW_CHUNKS
    n_tiles = n_tok // tile
    d = D_MODEL
    sorted_rows = tile * TOP_K + N_EXPERTS * (SUBLANES - 1) + EXPERT_BLOCK * BLOCKS_PER_TRIP
    sorted_rows = -(-sorted_rows // SUBLANES) * SUBLANES
    eps = EXPERTS_PER_STEP
    smem = lambda shape, imap: pl.BlockSpec(shape, imap, memory_space=pltpu.SMEM)
    in_specs = [smem((TOP_K * tile,), lambda t, e: (t,)),
                smem((TOP_K * tile,), lambda t, e: (t,)),
                smem((1, SUBLANES, ROUTER_LANES), lambda t, e: (t, 0, 0)),
                pl.BlockSpec((tile * ROW_CHUNKS, LANES), lambda t, e: (t, 0), pipeline_mode=pl.Buffered(1)),
                pl.BlockSpec((1, eps, d, EXPERT_FF), lambda t, e: (layer, e, 0, 0)),
                pl.BlockSpec((1, eps, d, EXPERT_FF), lambda t, e: (layer, e, 0, 0)),
                pl.BlockSpec((1, eps, EXPERT_FF, d), lambda t, e: (layer, e, 0, 0))]
    return pl.pallas_call(
        functools.partial(_moe_kernel, tile=tile),
        grid=(n_tiles, N_EXPERTS // eps), in_specs=in_specs,
        out_specs=pl.BlockSpec((tile * ROW_CHUNKS, LANES), lambda t, e: (t, 0)),
        out_shape=jax.ShapeDtypeStruct((n_tok * ROW_CHUNKS, LANES), F32),
        scratch_shapes=[pltpu.VMEM((sorted_rows * ROW_CHUNKS, LANES), F32)],
        compiler_params=_params(("arbitrary", "arbitrary")), name="moe",
    )(ri.reshape(-1), rw.reshape(-1), seg, u2c, w["w_gate"], w["w_up"], w["w_down"])


def _ln2_kernel(m_ref, x1_ref, mod_ref, g_ref, b_ref, o_ref, *, alpha):
    tile = x1_ref.shape[0]
    moe = jnp.concatenate([m_ref[:, c, :, :].reshape(tile, LANES) for c in range(ROW_CHUNKS)], axis=1)
    gate2 = mod_ref[0, 5:6, :]
    o_ref[...] = _layer_norm(alpha * x1_ref[...] + gate2 * moe, g_ref[0], b_ref[0])


def _ln2_call(moe_c, x1, mod, layer, w, alpha, tile):
    b, seq_len, d = x1.shape
    n_tok = b * seq_len
    per_seq = seq_len // tile
    mod_b = (lambda t: (t // per_seq, 0, 0)) if mod.shape[0] == b else (lambda t: (0, 0, 0))
    lw = lambda t: (layer, 0, 0)
    out = pl.pallas_call(
        functools.partial(_ln2_kernel, alpha=alpha),
        grid=(n_tok // tile,),
        in_specs=[pl.BlockSpec((tile // SUBLANES, ROW_CHUNKS, SUBLANES, LANES), lambda t: (t, 0, 0, 0)),
                  pl.BlockSpec((tile, d), lambda t: (t, 0)),
                  pl.BlockSpec((1, 6, d), mod_b),
                  pl.BlockSpec((1, 1, d), lw), pl.BlockSpec((1, 1, d), lw)],
        out_specs=pl.BlockSpec((tile, d), lambda t: (t, 0)),
        out_shape=jax.ShapeDtypeStruct((n_tok, d), F32),
        compiler_params=_params(("parallel",)), name="ln2",
    )(moe_c.reshape(n_tok // SUBLANES, ROW_CHUNKS, SUBLANES, LANES), x1.reshape(n_tok, d), mod,
      w["ln2_g"], w["ln2_b"])
    return out.reshape(b, seq_len, d)


def _rope_tables(seq_len):
    rows = seq_len // GRID_W
    row_id = jnp.repeat(jnp.arange(rows), GRID_W).astype(F32)
    col_id = jnp.tile(jnp.arange(GRID_W), rows).astype(F32)
    inv = ROPE_THETA ** (-jnp.arange(0, AXIS_DIM, 2, dtype=F32) / AXIS_DIM)
    ang = jnp.stack([row_id[:, None] * inv, col_id[:, None] * inv], axis=1)
    cos = jnp.cos(ang)[:, :, None, :]
    sin = jnp.sin(ang)[:, :, None, :]
    shape = (seq_len, 2, 2, AXIS_DIM // 2)
    cos = jnp.broadcast_to(cos, shape).reshape(seq_len, HEAD_DIM)
    sin = (jnp.broadcast_to(sin, shape) * jnp.array([-1.0, 1.0], F32)[None, None, :, None]).reshape(seq_len, HEAD_DIM)
    return jnp.tile(cos, (1, LANES // HEAD_DIM)), jnp.tile(sin, (1, LANES // HEAD_DIM))


def _trunk_layer(x, mod, layer, w, lam_init, alpha, rope, cache, new_kv, tiles):
    outs = _front_call(x, mod, layer, w, rope, new_kv, tiles["front"])
    if new_kv is not None:
        q, k, v, kf, vf, y_pool, y_conv = outs
    else:
        q, k, v, y_pool, y_conv = outs
        kf = vf = None
    y_attn = _attn_call(q, k, v, cache, layer, w, lam_init, tiles["attn"], tiles["attn_heads"])
    x1, u2c, sel = _post_call(y_pool, y_attn, y_conv, x, mod, layer, w, alpha, tiles["post"])
    ri, rw, seg = _route_call(sel.reshape(-1, ROUTER_LANES), tiles["moe"])
    moe_c = _moe_call(u2c.reshape(-1, LANES), ri, rw, seg, layer, w, tiles["moe"])
    x2 = _ln2_call(moe_c, x1, mod, layer, w, alpha, tiles["ln2"])
    return x2, kf, vf


def kernel(x_prompt, x_sample, cache_k, cache_v, c, c_ctx, w_mod, b_mod, w_in, w_out, pool_w, pool_scale, diff_lambda, subln_g, conv_dw, conv_ln_g, conv_ln_b, conv_pw, ln1_g, ln1_b, ln2_g, ln2_b, router_grp, router_grp_b, router_exp, router_exp_b, w_gate, w_up, w_down):
    depth = w_in.shape[0]
    batch, seq, d = x_prompt.shape
    dec_batch, dec_seq, _ = x_sample.shape
    past = cache_k.shape[2]
    alpha = (2 * depth) ** 0.25

    mod_rows = 16
    cvec = jnp.concatenate([c, c_ctx[None], jnp.zeros((mod_rows - dec_batch - 1, d), F32)], axis=0)
    mod_all = _mod_call(cvec, w_mod, b_mod)

    eye = jnp.eye(len(POOL_WINDOWS), dtype=F32)
    pool_bd = (pool_w[:, :, :, None, :] * eye[None, :, None, :, None]).reshape(depth, POOL_WIDTH, POOL_WIDTH)
    router_w = jnp.concatenate(
        [router_exp, router_grp, jnp.zeros((depth, d, ROUTER_LANES - N_EXPERTS - N_GROUPS), F32)], axis=-1)
    router_hi = router_w.astype(BF16)
    router_lo = (router_w - router_hi.astype(F32)).astype(BF16)
    router_b = jnp.concatenate(
        [router_exp_b, router_grp_b, jnp.zeros((depth, ROUTER_LANES - N_EXPERTS - N_GROUPS), F32)], axis=-1)
    r3 = lambda z: z.reshape(depth, 1, z.shape[-1])
    w = {
        "w_in": w_in.astype(BF16), "w_out": w_out.astype(BF16), "pool_bd": pool_bd.astype(BF16),
        "pool_scale": r3(pool_scale), "diff_lambda": diff_lambda, "subln_g": r3(subln_g),
        "conv_dw": conv_dw, "conv_ln_g": r3(conv_ln_g), "conv_ln_b": r3(conv_ln_b),
        "conv_pw": conv_pw.astype(BF16), "ln1_g": r3(ln1_g), "ln1_b": r3(ln1_b), "ln2_g": r3(ln2_g),
        "ln2_b": r3(ln2_b), "router_w": jnp.concatenate([router_hi, router_lo], axis=-1), "router_b": r3(router_b),
        "w_gate": w_gate.astype(BF16), "w_up": w_up.astype(BF16), "w_down": w_down.astype(BF16),
    }
    rope = _rope_tables(dec_seq)
    cache = (cache_k.reshape(dec_batch, depth, past, QK_WIDTH), cache_v.reshape(dec_batch, depth, past, ATTN_WIDTH))
    ctx_tiles = {"front": seq, "attn": seq, "attn_heads": ATTN_HEADS, "post": seq, "moe": 2048, "ln2": seq}
    lat_tiles = {"front": 512, "attn": 2048, "attn_heads": 1, "post": 512, "moe": 2048, "ln2": 512}

    xp, xs = x_prompt, x_sample
    new_kv = None
    for l in range(depth):
        lam_init = 0.8 - 0.6 * math.exp(-0.3 * l)
        mod_ctx = mod_all[l, dec_batch:dec_batch + 1].reshape(1, 6, d)
        mod_lat = mod_all[l, :dec_batch].reshape(dec_batch, 6, d)
        xp, kf, vf = _trunk_layer(xp, mod_ctx, l, w, lam_init, alpha, None, None, (depth, new_kv), ctx_tiles)
        new_kv = (kf, vf)
        xs, _, _ = _trunk_layer(xs, mod_lat, l, w, lam_init, alpha, rope, cache, None, lat_tiles)
    return (xp, xs, new_kv[0].reshape(batch, depth, seq, ATTN_HEADS, 2, HEAD_DIM),
            new_kv[1].reshape(batch, depth, seq, ATTN_HEADS, V_DIM))
```

```python
import functools
import math

import jax
import jax.numpy as jnp
from jax import lax
from jax.experimental import pallas as pl
from jax.experimental.pallas import tpu as pltpu

F32 = jnp.float32
BF16 = jnp.bfloat16

D_MODEL = 1024
GRID_W = 64
POOL_WIDTH = 256
POOL_WINDOWS = (2, 4, 8, 16)
POOL_GC = 64
ATTN_HEADS = 4
HEAD_DIM = 64
V_DIM = 128
ATTN_WIDTH = 512
QK_WIDTH = 512
CONV_WIDTH = 256
CONV_K = 31
IN_WIDTH = 2304
AXIS_DIM = 32
ROPE_THETA = 10000.0
N_GROUPS = 4
EXPERTS_PER_GROUP = 8
N_EXPERTS = 32
EXPERT_FF = 256
LN_EPS = 1e-5

O_A = 0
O_Q = POOL_WIDTH
O_K = O_Q + QK_WIDTH
O_V = O_K + QK_WIDTH
O_C = O_V + ATTN_WIDTH

TOP_K = 2
HALO = 16
LANES = 128
SUBLANES = 8
ROW_CHUNKS = D_MODEL // LANES
ROUTER_LANES = 128
ROUTE_BLOCK = 512
EXPERT_BLOCK = 128
BLOCKS_PER_TRIP = 1
EXPERTS_PER_STEP = 4
ATTN_ROWS = 128
LOG2E = 1.4426950408889634
NEG_BIG = -1e30
VMEM_LIMIT = 56 * 1024 * 1024


def _silu(x):
    return x * jax.nn.sigmoid(x)


def _layer_norm(x, g, b):
    mu = jnp.mean(x, axis=-1, keepdims=True)
    xc = x - mu
    var = jnp.mean(xc * xc, axis=-1, keepdims=True)
    return xc * lax.rsqrt(var + LN_EPS) * g + b


def _dot(a, b):
    return jnp.dot(a, b, preferred_element_type=F32)


def _params(sem):
    return pltpu.CompilerParams(dimension_semantics=sem, vmem_limit_bytes=VMEM_LIMIT)


def _mod_kernel(c_ref, w_ref, b_ref, o_ref):
    s = _silu(c_ref[...])
    o_ref[0] = _dot(s.astype(BF16), w_ref[0].astype(BF16)) + b_ref[0]


def _mod_call(cvec, w_mod, b_mod):
    depth, d, n6 = w_mod.shape
    rows = cvec.shape[0]
    nb = 1536
    return pl.pallas_call(
        _mod_kernel,
        grid=(depth, n6 // nb),
        in_specs=[
            pl.BlockSpec((rows, d), lambda l, j: (0, 0)),
            pl.BlockSpec((1, d, nb), lambda l, j: (l, 0, j)),
            pl.BlockSpec((1, 1, nb), lambda l, j: (l, 0, j)),
        ],
        out_specs=pl.BlockSpec((1, rows, nb), lambda l, j: (l, 0, j)),
        out_shape=jax.ShapeDtypeStruct((depth, rows, n6), F32),
        compiler_params=_params(("parallel", "parallel")),
        name="mod",
    )(cvec, w_mod, b_mod.reshape(depth, 1, n6))


def _front_kernel(*refs, seq_len, tile, has_rope, want_f32_kv, has_prev_kv):
    it = iter(refs)
    x_ref, xp_ref, xn_ref, mod_ref, win_ref = next(it), next(it), next(it), next(it), next(it)
    poolw_ref, pscale_ref, dw_ref, cg_ref, cb_ref, pw_ref = (next(it) for _ in range(6))
    if has_rope:
        cos_ref, sin_ref = next(it), next(it)
    if has_prev_kv:
        next(it), next(it)
    q_ref, k_ref, v_ref = next(it), next(it), next(it)
    if want_f32_kv:
        kf_ref, vf_ref = next(it), next(it)
    yp_ref, yc_ref = next(it), next(it)

    i = pl.program_id(1)
    n_ext = tile + 2 * HALO
    shift = mod_ref[0, 0:1, :]
    scale = mod_ref[0, 1:2, :]
    xe = jnp.concatenate([xp_ref[0], x_ref[0], xn_ref[0]], axis=0)
    u = (xe * (1.0 + scale) + shift).astype(BF16)
    proj = _dot(u, win_ref[0])

    row = lax.broadcasted_iota(jnp.int32, (n_ext, 1), 0)
    gpos = i * tile - HALO + row
    valid = (gpos >= 0) & (gpos < seq_len)

    a_ext = jnp.where(valid, proj[:, O_A:O_A + POOL_WIDTH], 0.0)

    def sh(z, s):
        return pltpu.roll(z, s % n_ext, axis=0)

    p2 = a_ext + sh(a_ext, 1)
    p4 = sh(p2, 1) + sh(p2, -1)
    p8 = sh(p4, 2) + sh(p4, -2)
    p16 = sh(p8, 4) + sh(p8, -4)
    t = gpos[HALO:HALO + tile]
    lane = lax.broadcasted_iota(jnp.int32, (1, POOL_WIDTH), 1)
    mean = None
    for g, (w, p) in enumerate(zip(POOL_WINDOWS, (p2, p4, p8, p16))):
        cnt = jnp.minimum(t + w // 2, seq_len) - jnp.maximum(t - w // 2, 0)
        m = p[HALO:HALO + tile] * (1.0 / cnt.astype(F32))
        mean = m if mean is None else jnp.where(lane >= g * POOL_GC, m, mean)
    pooled = mean - a_ext[HALO:HALO + tile]
    y_pool = _dot(pooled.astype(BF16), poolw_ref[0]) * pscale_ref[0]
    yp_ref[0] = y_pool.astype(BF16)

    ca = proj[:, O_C:O_C + CONV_WIDTH]
    cb = proj[:, O_C + CONV_WIDTH:O_C + 2 * CONV_WIDTH]
    h = jnp.where(valid, ca * jax.nn.sigmoid(cb), 0.0)
    acc = None
    for r in range(SUBLANES):
        hr = h if r == 0 else pltpu.roll(h, n_ext - r, axis=0)
        for j in range(CONV_K):
            off = HALO - CONV_K // 2 + j
            if off % SUBLANES != r:
                continue
            term = hr[off - r:off - r + tile] * dw_ref[0, j:j + 1, :]
            acc = term if acc is None else acc + term
    hn = _silu(_layer_norm(acc, cg_ref[0], cb_ref[0]))
    yc_ref[0] = _dot(hn.astype(BF16), pw_ref[0]).astype(BF16)

    qf = proj[HALO:HALO + tile, O_Q:O_Q + QK_WIDTH]
    kf = proj[HALO:HALO + tile, O_K:O_K + QK_WIDTH]
    vf = proj[HALO:HALO + tile, O_V:O_V + ATTN_WIDTH]
    if want_f32_kv:
        kf_ref[0, 0] = kf
        vf_ref[0, 0] = vf
    if has_rope:
        cos = cos_ref[...]
        sin = sin_ref[...]
        l128 = lax.broadcasted_iota(jnp.int32, (1, LANES), 1)
        first_half = (l128 & (AXIS_DIM - 1)) < (AXIS_DIM // 2)

        def rope(z):
            outs = []
            for c in range(QK_WIDTH // LANES):
                zc = z[:, c * LANES:(c + 1) * LANES]
                partner = jnp.where(first_half,
                                    pltpu.roll(zc, LANES - AXIS_DIM // 2, axis=1),
                                    pltpu.roll(zc, AXIS_DIM // 2, axis=1))
                outs.append(zc * cos + partner * sin)
            return jnp.concatenate(outs, axis=1)

        qf = rope(qf)
        kf = rope(kf)
    q_ref[0] = (qf * (LOG2E / math.sqrt(HEAD_DIM))).astype(BF16)
    k_ref[0] = kf.astype(BF16)
    v_ref[0] = vf.astype(BF16)


def _front_call(x, mod, layer, w, rope, new_kv, tile):
    want_f32_kv = new_kv is not None
    depth, prev_kv = new_kv if want_f32_kv else (None, None)
    b, seq_len, d = x.shape
    nt = seq_len // tile
    hb = tile // HALO
    n_hblk = seq_len // HALO
    mod_b = (lambda bi, i: (bi, 0, 0)) if mod.shape[0] == b else (lambda bi, i: (0, 0, 0))
    lw = lambda bi, i: (layer, 0, 0)
    in_specs = [
        pl.BlockSpec((1, tile, d), lambda bi, i: (bi, i, 0)),
        pl.BlockSpec((1, HALO, d), lambda bi, i: (bi, jnp.maximum(i * hb - 1, 0), 0)),
        pl.BlockSpec((1, HALO, d), lambda bi, i: (bi, jnp.minimum((i + 1) * hb, n_hblk - 1), 0)),
        pl.BlockSpec((1, 6, d), mod_b),
        pl.BlockSpec((1, d, IN_WIDTH), lw),
        pl.BlockSpec((1, POOL_WIDTH, POOL_WIDTH), lw),
        pl.BlockSpec((1, 1, POOL_WIDTH), lw),
        pl.BlockSpec((1, CONV_K, CONV_WIDTH), lw),
        pl.BlockSpec((1, 1, CONV_WIDTH), lw),
        pl.BlockSpec((1, 1, CONV_WIDTH), lw),
        pl.BlockSpec((1, CONV_WIDTH, CONV_WIDTH), lw),
    ]
    args = [x, x, x, mod, w["w_in"], w["pool_bd"], w["pool_scale"], w["conv_dw"], w["conv_ln_g"],
            w["conv_ln_b"], w["conv_pw"]]
    if rope is not None:
        in_specs += [pl.BlockSpec((tile, LANES), lambda bi, i: (i, 0))] * 2
        args += list(rope)
    tok = lambda width: pl.BlockSpec((1, tile, width), lambda bi, i: (bi, i, 0))
    out_specs = [tok(QK_WIDTH), tok(QK_WIDTH), tok(ATTN_WIDTH)]
    out_shape = [jax.ShapeDtypeStruct((b, seq_len, QK_WIDTH), BF16)] * 3
    aliases = {}
    if want_f32_kv:
        if prev_kv is not None:
            aliases = {len(args): len(out_specs), len(args) + 1: len(out_specs) + 1}
            in_specs += [pl.BlockSpec(memory_space=pl.ANY)] * 2
            args += list(prev_kv)
        layer_tok = lambda width: pl.BlockSpec((1, 1, tile, width), lambda bi, i: (bi, layer, i, 0))
        out_specs += [layer_tok(QK_WIDTH), layer_tok(ATTN_WIDTH)]
        out_shape += [jax.ShapeDtypeStruct((b, depth, seq_len, QK_WIDTH), F32)] * 2
    out_specs += [tok(POOL_WIDTH), tok(CONV_WIDTH)]
    out_shape += [jax.ShapeDtypeStruct((b, seq_len, POOL_WIDTH), BF16)] * 2
    kern = functools.partial(_front_kernel, seq_len=seq_len, tile=tile, has_rope=rope is not None,
                             want_f32_kv=want_f32_kv, has_prev_kv=prev_kv is not None)
    return pl.pallas_call(
        kern, grid=(b, nt), in_specs=in_specs, out_specs=out_specs, out_shape=out_shape,
        input_output_aliases=aliases,
        compiler_params=_params(("parallel", "parallel")), name="front",
    )(*args)


def _attn_kernel(*refs, lam_init, has_cache):
    if has_cache:
        lam_ref, g_ref, q_ref, k_ref, v_ref, ck_ref, cv_ref, o_ref = refs
    else:
        lam_ref, g_ref, q_ref, k_ref, v_ref, o_ref = refs
    lp = lam_ref[0]
    lam = (jnp.exp(jnp.sum(lp[0:1] * lp[1:2], axis=-1, keepdims=True))
           - jnp.exp(jnp.sum(lp[2:3] * lp[3:4], axis=-1, keepdims=True)) + lam_init)

    lane = lax.broadcasted_iota(jnp.int32, (1, LANES), 1)
    nt = (((1,), (1,)), ((), ()))
    heads = q_ref.shape[2] // LANES
    kv = []
    for hh in range(heads):
        cols = slice(hh * LANES, (hh + 1) * LANES)
        keys = [k_ref[0, :, cols]]
        vals = [v_ref[0, :, cols]]
        if has_cache:
            keys.append(ck_ref[0, 0, :, cols].astype(BF16))
            vals.append(cv_ref[0, 0, :, cols].astype(BF16))
        kv.append((keys, vals))

    def scores(q, keys):
        zero = jnp.zeros_like(q)
        return [[lax.dot_general(qm, kk, nt, preferred_element_type=F32) for kk in keys]
                for qm in (jnp.where(lane < HEAD_DIM, q, zero), jnp.where(lane >= HEAD_DIM, q, zero))]

    def weights(score_maps):
        parts = []
        for s in score_maps:
            m = functools.reduce(jnp.maximum, [jnp.max(z, axis=-1, keepdims=True) for z in s])
            e = [jnp.exp2(z - m) for z in s]
            den = functools.reduce(jnp.add, [jnp.sum(z, axis=-1, keepdims=True) for z in e])
            parts.append((e, den))
        (e1, den1), (e2, den2) = parts
        c = lam * den1 / den2
        return [(z1 - z2 * c).astype(BF16) for z1, z2 in zip(e1, e2)], den1

    def values(a_parts, den1, vals):
        o = functools.reduce(jnp.add, [_dot(a, vv) for a, vv in zip(a_parts, vals)])
        o = o * (1.0 / den1)
        o = o * lax.rsqrt(jnp.mean(o * o, axis=-1, keepdims=True) + LN_EPS) * g_ref[0] * (1.0 - lam_init)
        return o.astype(BF16)

    tq = q_ref.shape[1]
    n_sub = max(1, tq // ATTN_ROWS)
    step = tq // n_sub
    items = [(hh, j) for hh in range(heads) for j in range(n_sub)]

    def item_scores(item):
        hh, j = item
        return scores(q_ref[0, j * step:(j + 1) * step, hh * LANES:(hh + 1) * LANES], kv[hh][0])

    def item_store(item, a_parts, den1):
        hh, j = item
        o_ref[0, j * step:(j + 1) * step, hh * LANES:(hh + 1) * LANES] = values(a_parts, den1, kv[hh][1])

    pending = item_scores(items[0])
    softmaxed = None
    for idx, item in enumerate(items):
        current = pending
        if idx + 1 < len(items):
            pending = item_scores(items[idx + 1])
        done = weights(current)
        if softmaxed is not None:
            item_store(items[idx - 1], *softmaxed)
        softmaxed = done
    item_store(items[-1], *softmaxed)


def _attn_call(q, k, v, cache, layer, w, lam_init, tq, heads):
    b, seq_len, _ = q.shape
    width = heads * LANES
    lw = lambda bi, h, i: (layer, 0, 0)
    in_specs = [
        pl.BlockSpec((1, 4, HEAD_DIM), lw),
        pl.BlockSpec((1, 1, V_DIM), lw),
        pl.BlockSpec((1, tq, width), lambda bi, h, i: (bi, i, h)),
        pl.BlockSpec((1, seq_len, width), lambda bi, h, i: (bi, 0, h)),
        pl.BlockSpec((1, seq_len, width), lambda bi, h, i: (bi, 0, h)),
    ]
    args = [w["diff_lambda"], w["subln_g"], q, k, v]
    if cache is not None:
        past = cache[0].shape[2]
        cspec = pl.BlockSpec((1, 1, past, width), lambda bi, h, i: (bi, layer, 0, h))
        in_specs += [cspec, cspec]
        args += list(cache)
    kern = functools.partial(_attn_kernel, lam_init=lam_init, has_cache=cache is not None)
    return pl.pallas_call(
        kern, grid=(b, ATTN_HEADS // heads, seq_len // tq), in_specs=in_specs,
        out_specs=pl.BlockSpec((1, tq, width), lambda bi, h, i: (bi, i, h)),
        out_shape=jax.ShapeDtypeStruct((b, seq_len, ATTN_WIDTH), BF16),
        compiler_params=_params(("parallel", "parallel", "parallel")), name="attn",
    )(*args)


def _post_kernel(yp_ref, ya_ref, yc_ref, x_ref, mod_ref, wout_ref, g_ref, b_ref, wr_ref, br_ref,
                 x1_ref, u2c_ref, sel_ref, *, alpha):
    o1 = POOL_WIDTH
    o2 = o1 + ATTN_WIDTH
    y = (_dot(yp_ref[0], wout_ref[0, 0:o1, :]) + _dot(ya_ref[0], wout_ref[0, o1:o2, :])
         + _dot(yc_ref[0], wout_ref[0, o2:, :]))
    gate1 = mod_ref[0, 2:3, :]
    shift2 = mod_ref[0, 3:4, :]
    scale2 = mod_ref[0, 4:5, :]
    x1 = _layer_norm(alpha * x_ref[0] + gate1 * y, g_ref[0], b_ref[0])
    x1_ref[0] = x1
    u2 = x1 * (1.0 + scale2) + shift2
    tile = u2.shape[0]
    for c in range(ROW_CHUNKS):
        u2c_ref[:, c, :, :] = u2[:, c * LANES:(c + 1) * LANES].reshape(tile // SUBLANES, SUBLANES, LANES)

    u_hi = u2.astype(BF16)
    u_lo = (u2 - u_hi.astype(F32)).astype(BF16)
    both = _dot(u_hi, wr_ref[0]) + _dot(u_lo, wr_ref[0])
    logits = both[:, :ROUTER_LANES] + both[:, ROUTER_LANES:] + br_ref[0]
    lane = lax.broadcasted_iota(jnp.int32, (1, ROUTER_LANES), 1)
    lane_f = lane.astype(F32)
    is_g = (lane >= N_EXPERTS) & (lane < N_EXPERTS + N_GROUPS)
    rmax = lambda z: jnp.max(z, axis=-1, keepdims=True)
    rmin = lambda z: jnp.min(z, axis=-1, keepdims=True)
    gl = jnp.where(is_g, logits, NEG_BIG)
    ge = jnp.where(is_g, jnp.exp(gl - rmax(gl)), 0.0)
    gp = ge / jnp.sum(ge, axis=-1, keepdims=True)
    g_w = rmax(gp)
    g_idx = rmin(jnp.where(is_g & (gp == g_w), lane_f - N_EXPERTS, 1e9))
    in_grp = (lane < N_EXPERTS) & (jnp.right_shift(lane, 3).astype(F32) == g_idx)
    el = jnp.where(in_grp, logits, NEG_BIG)
    v1 = rmax(el)
    i1 = rmin(jnp.where(in_grp & (el == v1), lane_f, 1e9))
    rest = in_grp & (lane_f != i1)
    el2 = jnp.where(rest, logits, NEG_BIG)
    v2 = rmax(el2)
    i2 = rmin(jnp.where(rest & (el2 == v2), lane_f, 1e9))
    tt = jnp.exp(v2 - v1)
    w1 = 1.0 / (1.0 + tt)
    w2 = tt / (1.0 + tt)
    sel = jnp.zeros((tile, ROUTER_LANES), F32)
    for j, col in enumerate((i1, i2, g_w * w1, g_w * w2)):
        sel = jnp.where(lane == j, col, sel)
    sel_ref[0] = sel


def _post_call(y_pool, y_attn, y_conv, x, mod, layer, w, alpha, tile):
    b, seq_len, d = x.shape
    nt = seq_len // tile
    mod_b = (lambda bi, i: (bi, 0, 0)) if mod.shape[0] == b else (lambda bi, i: (0, 0, 0))
    lw = lambda bi, i: (layer, 0, 0)
    tok = lambda width: pl.BlockSpec((1, tile, width), lambda bi, i: (bi, i, 0))
    in_specs = [tok(POOL_WIDTH), tok(ATTN_WIDTH), tok(CONV_WIDTH), tok(d),
                pl.BlockSpec((1, 6, d), mod_b),
                pl.BlockSpec((1, d, d), lw),
                pl.BlockSpec((1, 1, d), lw), pl.BlockSpec((1, 1, d), lw),
                pl.BlockSpec((1, d, 2 * ROUTER_LANES), lw), pl.BlockSpec((1, 1, ROUTER_LANES), lw)]
    out_specs = [tok(d),
                 pl.BlockSpec((tile // SUBLANES, ROW_CHUNKS, SUBLANES, LANES), lambda bi, i: (bi * nt + i, 0, 0, 0)),
                 tok(ROUTER_LANES)]
    out_shape = [jax.ShapeDtypeStruct((b, seq_len, d), F32),
                 jax.ShapeDtypeStruct((b * seq_len // SUBLANES, ROW_CHUNKS, SUBLANES, LANES), F32),
                 jax.ShapeDtypeStruct((b, seq_len, ROUTER_LANES), F32)]
    return pl.pallas_call(
        functools.partial(_post_kernel, alpha=alpha),
        grid=(b, nt), in_specs=in_specs, out_specs=out_specs, out_shape=out_shape,
        compiler_params=_params(("parallel", "parallel")), name="post",
    )(y_pool, y_attn, y_conv, x, mod, w["w_out"], w["ln1_g"], w["ln1_b"], w["router_w"], w["router_b"])


def _route_kernel(sel_ref, ri_ref, rw_ref, seg_ref, *, tile):
    lane = lax.broadcasted_iota(jnp.int32, (1, ROUTER_LANES), 1)
    lane_f = lane.astype(F32)
    blk = ROUTE_BLOCK
    r_io = lax.broadcasted_iota(jnp.int32, (blk, blk), 0)
    c_io = lax.broadcasted_iota(jnp.int32, (blk, blk), 1)
    before = jnp.where(r_io > c_io, 1.0, 0.0).astype(BF16)

    def picks(j):
        s = sel_ref[j * blk:(j + 1) * blk, :]
        return s, lane_f == s[:, 0:1], lane_f == s[:, 1:2]

    carry = jnp.zeros((1, ROUTER_LANES), F32)
    prior = []
    for j in range(tile // blk):
        _, oh1, oh2 = picks(j)
        member = jnp.where(oh1 | oh2, 1.0, 0.0)
        prior.append(_dot(before, member.astype(BF16)) + carry)
        carry = carry + jnp.sum(member, axis=0, keepdims=True)

    cnt = jnp.broadcast_to(carry, (SUBLANES, ROUTER_LANES)).astype(jnp.int32)
    padded = jnp.where(lane < N_EXPERTS, jnp.right_shift(cnt + (SUBLANES - 1), 3) * SUBLANES, 0)
    incl = padded
    step = 1
    while step < N_EXPERTS:
        incl = incl + jnp.where(lane >= step, pltpu.roll(incl, step, axis=1), 0)
        step *= 2
    start = incl - padded
    seg_ref[0] = jnp.where(lane < N_EXPERTS, cnt, pltpu.roll(start, N_EXPERTS, axis=1))
    start_f = start[0:1, :].astype(F32)

    for j in range(tile // blk):
        s, oh1, oh2 = picks(j)
        cols = []
        for oh in (oh1, oh2):
            p = jnp.sum(jnp.where(oh, start_f + prior[j], 0.0), axis=-1, keepdims=True).astype(jnp.int32)
            addr = jnp.right_shift(p, 3) * (SUBLANES * ROW_CHUNKS) + jnp.bitwise_and(p, SUBLANES - 1)
            cols.append(addr.astype(F32))
        z = jnp.where(lane == 0, cols[0], jnp.where(lane == 1, cols[1], s))
        zt = z.T
        ri_ref[0, :, j * blk:(j + 1) * blk] = zt[0:2, :].astype(jnp.int32)
        rw_ref[0, :, j * blk:(j + 1) * blk] = zt[2:4, :]


def _route_call(sel, tile):
    n_tok = sel.shape[0]
    n_tiles = n_tok // tile
    return pl.pallas_call(
        functools.partial(_route_kernel, tile=tile),
        grid=(n_tiles,),
        in_specs=[pl.BlockSpec((tile, ROUTER_LANES), lambda t: (t, 0))],
        out_specs=[pl.BlockSpec((1, TOP_K, tile), lambda t: (t, 0, 0)),
                   pl.BlockSpec((1, TOP_K, tile), lambda t: (t, 0, 0)),
                   pl.BlockSpec((1, SUBLANES, ROUTER_LANES), lambda t: (t, 0, 0))],
        out_shape=[jax.ShapeDtypeStruct((n_tiles, TOP_K, tile), jnp.int32),
                   jax.ShapeDtypeStruct((n_tiles, TOP_K, tile), F32),
                   jax.ShapeDtypeStruct((n_tiles, SUBLANES, ROUTER_LANES), jnp.int32)],
        compiler_params=_params(("parallel",)), name="route",
    )(sel)


def _row_at(start):
    return pl.ds(start, ROW_CHUNKS, stride=SUBLANES)


def _moe_kernel(ri_ref, rw_ref, seg_ref, u2c_ref, wg_ref, wu_ref, wd_ref, o_ref, sorted_ref, *, tile):
    step = pl.program_id(1)
    groups = tile // SUBLANES
    group_rows = SUBLANES * ROW_CHUNKS

    @pl.when((pl.program_id(0) == 0) & (step == 0))
    def _():
        sorted_ref[...] = jnp.zeros_like(sorted_ref)

    @pl.when(step == 0)
    def _():
        def disp_body(a, carry):
            for s in range(SUBLANES):
                row = u2c_ref[_row_at(a * group_rows + s), :]
                for k in range(TOP_K):
                    sorted_ref[_row_at(ri_ref[k * tile + a * SUBLANES + s]), :] = row
            return carry
        lax.fori_loop(0, groups, disp_body, 0)

    blk_groups = EXPERT_BLOCK // SUBLANES
    row_id = lax.broadcasted_iota(jnp.int32, (SUBLANES, 1), 0)

    def tile_at(start, g, c):
        return pl.ds(pl.multiple_of(start + (g * ROW_CHUNKS + c) * SUBLANES, SUBLANES), SUBLANES)

    experts = [step * EXPERTS_PER_STEP + j for j in range(EXPERTS_PER_STEP)]
    cnts = [seg_ref[0, 0, e] for e in experts]
    bases = [seg_ref[0, 0, N_EXPERTS + e] for e in experts]
    trip_rows = EXPERT_BLOCK * BLOCKS_PER_TRIP

    def blk_body(bi, carry):
        chains = [(j, (bi * BLOCKS_PER_TRIP + u) * EXPERT_BLOCK)
                  for j in range(EXPERTS_PER_STEP) for u in range(BLOCKS_PER_TRIP)]
        starts = [jnp.where(first < cnts[j], (bases[j] + first) * ROW_CHUNKS, 0) for j, first in chains]
        xs = [jnp.concatenate(
            [jnp.concatenate([sorted_ref[tile_at(start, g, c), :] for g in range(blk_groups)], axis=0)
             for c in range(ROW_CHUNKS)], axis=1).astype(BF16) for start in starts]
        hs = [_silu(_dot(x, wg_ref[0, j])) * _dot(x, wu_ref[0, j]) for x, (j, _) in zip(xs, chains)]
        ys = [_dot(h.astype(BF16), wd_ref[0, j]) for h, (j, _) in zip(hs, chains)]
        for (j, first), start, y in zip(chains, starts, ys):
            n_valid = cnts[j] - first
            for g in range(blk_groups):
                keep = (row_id + g * SUBLANES) < n_valid
                for c in range(ROW_CHUNKS):
                    sl = tile_at(start, g, c)
                    piece = y[g * SUBLANES:(g + 1) * SUBLANES, c * LANES:(c + 1) * LANES]
                    sorted_ref[sl, :] = jnp.where(keep, piece, sorted_ref[sl, :])
        return carry
    trips = functools.reduce(jnp.maximum, [(cnt + trip_rows - 1) // trip_rows for cnt in cnts])
    lax.fori_loop(0, trips, blk_body, 0)

    @pl.when(step == pl.num_programs(1) - 1)
    def _():
        def comb_body(a, carry):
            for s in range(SUBLANES):
                n = a * SUBLANES + s
                acc = None
                for k in range(TOP_K):
                    term = sorted_ref[_row_at(ri_ref[k * tile + n]), :] * rw_ref[k * tile + n]
                    acc = term if acc is None else acc + term
                o_ref[_row_at(a * group_rows + s), :] = acc
            return carry
        lax.fori_loop(0, groups, comb_body, 0)


def _moe_call(u2c, ri, rw, seg, layer, w, tile):
    n_tok = u2c.shape[0] // ROW_CHUNKS
    n_tiles = n_tok // tile
    d = D_MODEL
    sorted_rows = tile * TOP_K + N_EXPERTS * (SUBLANES - 1) + EXPERT_BLOCK * BLOCKS_PER_TRIP
    sorted_rows = -(-sorted_rows // SUBLANES) * SUBLANES
    eps = EXPERTS_PER_STEP
    smem = lambda shape, imap: pl.BlockSpec(shape, imap, memory_space=pltpu.SMEM)
    in_specs = [smem((TOP_K * tile,), lambda t, e: (t,)),
                smem((TOP_K * tile,), lambda t, e: (t,)),
                smem((1, SUBLANES, ROUTER_LANES), lambda t, e: (t, 0, 0)),
                pl.BlockSpec((tile * ROW_CHUNKS, LANES), lambda t, e: (t, 0), pipeline_mode=pl.Buffered(1)),
                pl.BlockSpec((1, eps, d, EXPERT_FF), lambda t, e: (layer, e, 0, 0)),
                pl.BlockSpec((1, eps, d, EXPERT_FF), lambda t, e: (layer, e, 0, 0)),
                pl.BlockSpec((1, eps, EXPERT_FF, d), lambda t, e: (layer, e, 0, 0))]
    return pl.pallas_call(
        functools.partial(_moe_kernel, tile=tile),
        grid=(n_tiles, N_EXPERTS // eps), in_specs=in_specs,
        out_specs=pl.BlockSpec((tile * ROW_CHUNKS, LANES), lambda t, e: (t, 0)),
        out_shape=jax.ShapeDtypeStruct((n_tok * ROW_CHUNKS, LANES), F32),
        scratch_shapes=[pltpu.VMEM((sorted_rows * ROW_CHUNKS, LANES), F32)],
        compiler_params=_params(("arbitrary", "arbitrary")), name="moe",
    )(ri.reshape(-1), rw.reshape(-1), seg, u2c, w["w_gate"], w["w_up"], w["w_down"])


def _ln2_kernel(m_ref, x1_ref, mod_ref, g_ref, b_ref, o_ref, *, alpha):
    tile = x1_ref.shape[0]
    moe = jnp.concatenate([m_ref[:, c, :, :].reshape(tile, LANES) for c in range(ROW_CHUNKS)], axis=1)
    gate2 = mod_ref[0, 5:6, :]
    o_ref[...] = _layer_norm(alpha * x1_ref[...] + gate2 * moe, g_ref[0], b_ref[0])


def _ln2_call(moe_c, x1, mod, layer, w, alpha, tile):
    b, seq_len, d = x1.shape
    n_tok = b * seq_len
    per_seq = seq_len // tile
    mod_b = (lambda t: (t // per_seq, 0, 0)) if mod.shape[0] == b else (lambda t: (0, 0, 0))
    lw = lambda t: (layer, 0, 0)
    out = pl.pallas_call(
        functools.partial(_ln2_kernel, alpha=alpha),
        grid=(n_tok // tile,),
        in_specs=[pl.BlockSpec((tile // SUBLANES, ROW_CHUNKS, SUBLANES, LANES), lambda t: (t, 0, 0, 0)),
                  pl.BlockSpec((tile, d), lambda t: (t, 0)),
                  pl.BlockSpec((1, 6, d), mod_b),
                  pl.BlockSpec((1, 1, d), lw), pl.BlockSpec((1, 1, d), lw)],
        out_specs=pl.BlockSpec((tile, d), lambda t: (t, 0)),
        out_shape=jax.ShapeDtypeStruct((n_tok, d), F32),
        compiler_params=_params(("parallel",)), name="ln2",
    )(moe_c.reshape(n_tok // SUBLANES, ROW_CHUNKS, SUBLANES, LANES), x1.reshape(n_tok, d), mod,
      w["ln2_g"], w["ln2_b"])
    return out.reshape(b, seq_len, d)


def _rope_tables(seq_len):
    rows = seq_len // GRID_W
    row_id = jnp.repeat(jnp.arange(rows), GRID_W).astype(F32)
    col_id = jnp.tile(jnp.arange(GRID_W), rows).astype(F32)
    inv = ROPE_THETA ** (-jnp.arange(0, AXIS_DIM, 2, dtype=F32) / AXIS_DIM)
    ang = jnp.stack([row_id[:, None] * inv, col_id[:, None] * inv], axis=1)
    cos = jnp.cos(ang)[:, :, None, :]
    sin = jnp.sin(ang)[:, :, None, :]
    shape = (seq_len, 2, 2, AXIS_DIM // 2)
    cos = jnp.broadcast_to(cos, shape).reshape(seq_len, HEAD_DIM)
    sin = (jnp.broadcast_to(sin, shape) * jnp.array([-1.0, 1.0], F32)[None, None, :, None]).reshape(seq_len, HEAD_DIM)
    return jnp.tile(cos, (1, LANES // HEAD_DIM)), jnp.tile(sin, (1, LANES // HEAD_DIM))


def _trunk_layer(x, mod, layer, w, lam_init, alpha, rope, cache, new_kv, tiles):
    outs = _front_call(x, mod, layer, w, rope, new_kv, tiles["front"])
    if new_kv is not None:
        q, k, v, kf, vf, y_pool, y_conv = outs
    else:
        q, k, v, y_pool, y_conv = outs
        kf = vf = None
    y_attn = _attn_call(q, k, v, cache, layer, w, lam_init, tiles["attn"], tiles["attn_heads"])
    x1, u2c, sel = _post_call(y_pool, y_attn, y_conv, x, mod, layer, w, alpha, tiles["post"])
    ri, rw, seg = _route_call(sel.reshape(-1, ROUTER_LANES), tiles["moe"])
    moe_c = _moe_call(u2c.reshape(-1, LANES), ri, rw, seg, layer, w, tiles["moe"])
    x2 = _ln2_call(moe_c, x1, mod, layer, w, alpha, tiles["ln2"])
    return x2, kf, vf


def kernel(x_prompt, x_sample, cache_k, cache_v, c, c_ctx, w_mod, b_mod, w_in, w_out, pool_w, pool_scale, diff_lambda, subln_g, conv_dw, conv_ln_g, conv_ln_b, conv_pw, ln1_g, ln1_b, ln2_g, ln2_b, router_grp, router_grp_b, router_exp, router_exp_b, w_gate, w_up, w_down):
    depth = w_in.shape[0]
    batch, seq, d = x_prompt.shape
    dec_batch, dec_seq, _ = x_sample.shape
    past = cache_k.shape[2]
    alpha = (2 * depth) ** 0.25

    mod_rows = 16
    cvec = jnp.concatenate([c, c_ctx[None], jnp.zeros((mod_rows - dec_batch - 1, d), F32)], axis=0)
    mod_all = _mod_call(cvec, w_mod, b_mod)

    eye = jnp.eye(len(POOL_WINDOWS), dtype=F32)
    pool_bd = (pool_w[:, :, :, None, :] * eye[None, :, None, :, None]).reshape(depth, POOL_WIDTH, POOL_WIDTH)
    router_w = jnp.concatenate(
        [router_exp, router_grp, jnp.zeros((depth, d, ROUTER_LANES - N_EXPERTS - N_GROUPS), F32)], axis=-1)
    router_hi = router_w.astype(BF16)
    router_lo = (router_w - router_hi.astype(F32)).astype(BF16)
    router_b = jnp.concatenate(
        [router_exp_b, router_grp_b, jnp.zeros((depth, ROUTER_LANES - N_EXPERTS - N_GROUPS), F32)], axis=-1)
    r3 = lambda z: z.reshape(depth, 1, z.shape[-1])
    w = {
        "w_in": w_in.astype(BF16), "w_out": w_out.astype(BF16), "pool_bd": pool_bd.astype(BF16),
        "pool_scale": r3(pool_scale), "diff_lambda": diff_lambda, "subln_g": r3(subln_g),
        "conv_dw": conv_dw, "conv_ln_g": r3(conv_ln_g), "conv_ln_b": r3(conv_ln_b),
        "conv_pw": conv_pw.astype(BF16), "ln1_g": r3(ln1_g), "ln1_b": r3(ln1_b), "ln2_g": r3(ln2_g),
        "ln2_b": r3(ln2_b), "router_w": jnp.concatenate([router_hi, router_lo], axis=-1), "router_b": r3(router_b),
        "w_gate": w_gate.astype(BF16), "w_up": w_up.astype(BF16), "w_down": w_down.astype(BF16),
    }
    rope = _rope_tables(dec_seq)
    cache = (cache_k.reshape(dec_batch, depth, past, QK_WIDTH), cache_v.reshape(dec_batch, depth, past, ATTN_WIDTH))
    ctx_tiles = {"front": seq, "attn": seq, "attn_heads": ATTN_HEADS, "post": seq, "moe": 2048, "ln2": seq}
    lat_tiles = {"front": 512, "attn": 2048, "attn_heads": 1, "post": 512, "moe": 2048, "ln2": 512}

    xp, xs = x_prompt, x_sample
    new_kv = None
    for l in range(depth):
        lam_init = 0.8 - 0.6 * math.exp(-0.3 * l)
        mod_ctx = mod_all[l, dec_batch:dec_batch + 1].reshape(1, 6, d)
        mod_lat = mod_all[l, :dec_batch].reshape(dec_batch, 6, d)
        xp, kf, vf = _trunk_layer(xp, mod_ctx, l, w, lam_init, alpha, None, None, (depth, new_kv), ctx_tiles)
        new_kv = (kf, vf)
        xs, _, _ = _trunk_layer(xs, mod_lat, l, w, lam_init, alpha, rope, cache, None, lat_tiles)
    return (xp, xs, new_kv[0].reshape(batch, depth, seq, ATTN_HEADS, 2, HEAD_DIM),
            new_kv[1].reshape(batch, depth, seq, ATTN_HEADS, V_DIM))
```

```python
import functools
import math

import jax
import jax.numpy as jnp
from jax import lax
from jax.experimental import pallas as pl
from jax.experimental.pallas import tpu as pltpu

F32 = jnp.float32
BF16 = jnp.bfloat16

D_MODEL = 1024
GRID_W = 64
POOL_WIDTH = 256
POOL_WINDOWS = (2, 4, 8, 16)
POOL_GC = 64
ATTN_HEADS = 4
HEAD_DIM = 64
V_DIM = 128
ATTN_WIDTH = 512
QK_WIDTH = 512
CONV_WIDTH = 256
CONV_K = 31
IN_WIDTH = 2304
AXIS_DIM = 32
ROPE_THETA = 10000.0
N_GROUPS = 4
EXPERTS_PER_GROUP = 8
N_EXPERTS = 32
EXPERT_FF = 256
LN_EPS = 1e-5

O_A = 0
O_Q = POOL_WIDTH
O_K = O_Q + QK_WIDTH
O_V = O_K + QK_WIDTH
O_C = O_V + ATTN_WIDTH

TOP_K = 2
HALO = 16
LANES = 128
SUBLANES = 8
ROW_CHUNKS = D_MODEL // LANES
ROUTER_LANES = 128
ROUTE_BLOCK = 512
EXPERT_BLOCK = 128
BLOCKS_PER_TRIP = 1
EXPERTS_PER_STEP = 4
ATTN_ROWS = 128
LOG2E = 1.4426950408889634
NEG_BIG = -1e30
VMEM_LIMIT = 56 * 1024 * 1024


def _silu(x):
    return x * jax.nn.sigmoid(x)


def _layer_norm(x, g, b):
    mu = jnp.mean(x, axis=-1, keepdims=True)
    xc = x - mu
    var = jnp.mean(xc * xc, axis=-1, keepdims=True)
    return xc * lax.rsqrt(var + LN_EPS) * g + b


def _dot(a, b):
    return jnp.dot(a, b, preferred_element_type=F32)


def _params(sem):
    return pltpu.CompilerParams(dimension_semantics=sem, vmem_limit_bytes=VMEM_LIMIT)


def _mod_kernel(c_ref, w_ref, b_ref, o_ref):
    s = _silu(c_ref[...])
    o_ref[0] = _dot(s.astype(BF16), w_ref[0].astype(BF16)) + b_ref[0]


def _mod_call(cvec, w_mod, b_mod):
    depth, d, n6 = w_mod.shape
    rows = cvec.shape[0]
    nb = 1536
    return pl.pallas_call(
        _mod_kernel,
        grid=(depth, n6 // nb),
        in_specs=[
            pl.BlockSpec((rows, d), lambda l, j: (0, 0)),
            pl.BlockSpec((1, d, nb), lambda l, j: (l, 0, j)),
            pl.BlockSpec((1, 1, nb), lambda l, j: (l, 0, j)),
        ],
        out_specs=pl.BlockSpec((1, rows, nb), lambda l, j: (l, 0, j)),
        out_shape=jax.ShapeDtypeStruct((depth, rows, n6), F32),
        compiler_params=_params(("parallel", "parallel")),
        name="mod",
    )(cvec, w_mod, b_mod.reshape(depth, 1, n6))


def _front_kernel(*refs, seq_len, tile, has_rope, want_f32_kv, has_prev_kv, kv_slot):
    it = iter(refs)
    x_ref, xp_ref, xn_ref, mod_ref, win_ref = next(it), next(it), next(it), next(it), next(it)
    poolw_ref, pscale_ref, dw_ref, cg_ref, cb_ref, pw_ref = (next(it) for _ in range(6))
    if has_rope:
        cos_ref, sin_ref = next(it), next(it)
    if has_prev_kv:
        next(it), next(it)
    q_ref, k_ref, v_ref = next(it), next(it), next(it)
    if want_f32_kv:
        kf_ref, vf_ref = next(it), next(it)
    yp_ref, yc_ref = next(it), next(it)

    i = pl.program_id(1)
    n_ext = tile + 2 * HALO
    shift = mod_ref[0, 0:1, :]
    scale = mod_ref[0, 1:2, :]
    xe = jnp.concatenate([xp_ref[0], x_ref[0], xn_ref[0]], axis=0)
    u = (xe * (1.0 + scale) + shift).astype(BF16)
    proj = _dot(u, win_ref[0])

    row = lax.broadcasted_iota(jnp.int32, (n_ext, 1), 0)
    gpos = i * tile - HALO + row
    valid = (gpos >= 0) & (gpos < seq_len)

    a_ext = jnp.where(valid, proj[:, O_A:O_A + POOL_WIDTH], 0.0)

    def sh(z, s):
        return pltpu.roll(z, s % n_ext, axis=0)

    p2 = a_ext + sh(a_ext, 1)
    p4 = sh(p2, 1) + sh(p2, -1)
    p8 = sh(p4, 2) + sh(p4, -2)
    p16 = sh(p8, 4) + sh(p8, -4)
    t = gpos[HALO:HALO + tile]
    lane = lax.broadcasted_iota(jnp.int32, (1, POOL_WIDTH), 1)
    mean = None
    for g, (w, p) in enumerate(zip(POOL_WINDOWS, (p2, p4, p8, p16))):
        cnt = jnp.minimum(t + w // 2, seq_len) - jnp.maximum(t - w // 2, 0)
        m = p[HALO:HALO + tile] * (1.0 / cnt.astype(F32))
        mean = m if mean is None else jnp.where(lane >= g * POOL_GC, m, mean)
    pooled = mean - a_ext[HALO:HALO + tile]
    y_pool = _dot(pooled.astype(BF16), poolw_ref[0]) * pscale_ref[0]
    yp_ref[0] = y_pool.astype(BF16)

    ca = proj[:, O_C:O_C + CONV_WIDTH]
    cb = proj[:, O_C + CONV_WIDTH:O_C + 2 * CONV_WIDTH]
    h = jnp.where(valid, ca * jax.nn.sigmoid(cb), 0.0)
    acc = None
    for r in range(SUBLANES):
        hr = h if r == 0 else pltpu.roll(h, n_ext - r, axis=0)
        for j in range(CONV_K):
            off = HALO - CONV_K // 2 + j
            if off % SUBLANES != r:
                continue
            term = hr[off - r:off - r + tile] * dw_ref[0, j:j + 1, :]
            acc = term if acc is None else acc + term
    hn = _silu(_layer_norm(acc, cg_ref[0], cb_ref[0]))
    yc_ref[0] = _dot(hn.astype(BF16), pw_ref[0]).astype(BF16)

    qf = proj[HALO:HALO + tile, O_Q:O_Q + QK_WIDTH]
    kf = proj[HALO:HALO + tile, O_K:O_K + QK_WIDTH]
    vf = proj[HALO:HALO + tile, O_V:O_V + ATTN_WIDTH]
    if want_f32_kv:
        for slot in range(kf_ref.shape[1]):
            mine = has_prev_kv or slot == kv_slot
            kf_ref[0, slot] = kf if mine else jnp.zeros_like(kf)
            vf_ref[0, slot] = vf if mine else jnp.zeros_like(vf)
    if has_rope:
        cos = cos_ref[...]
        sin = sin_ref[...]
        l128 = lax.broadcasted_iota(jnp.int32, (1, LANES), 1)
        first_half = (l128 & (AXIS_DIM - 1)) < (AXIS_DIM // 2)

        def rope(z):
            outs = []
            for c in range(QK_WIDTH // LANES):
                zc = z[:, c * LANES:(c + 1) * LANES]
                partner = jnp.where(first_half,
                                    pltpu.roll(zc, LANES - AXIS_DIM // 2, axis=1),
                                    pltpu.roll(zc, AXIS_DIM // 2, axis=1))
                outs.append(zc * cos + partner * sin)
            return jnp.concatenate(outs, axis=1)

        qf = rope(qf)
        kf = rope(kf)
    q_ref[0] = (qf * (LOG2E / math.sqrt(HEAD_DIM))).astype(BF16)
    k_ref[0] = kf.astype(BF16)
    v_ref[0] = vf.astype(BF16)


def _front_call(x, mod, layer, w, rope, new_kv, tile):
    want_f32_kv = new_kv is not None
    depth, prev_kv = new_kv if want_f32_kv else (None, None)
    b, seq_len, d = x.shape
    nt = seq_len // tile
    hb = tile // HALO
    n_hblk = seq_len // HALO
    mod_b = (lambda bi, i: (bi, 0, 0)) if mod.shape[0] == b else (lambda bi, i: (0, 0, 0))
    lw = lambda bi, i: (layer, 0, 0)
    in_specs = [
        pl.BlockSpec((1, tile, d), lambda bi, i: (bi, i, 0)),
        pl.BlockSpec((1, HALO, d), lambda bi, i: (bi, jnp.maximum(i * hb - 1, 0), 0)),
        pl.BlockSpec((1, HALO, d), lambda bi, i: (bi, jnp.minimum((i + 1) * hb, n_hblk - 1), 0)),
        pl.BlockSpec((1, 6, d), mod_b),
        pl.BlockSpec((1, d, IN_WIDTH), lw),
        pl.BlockSpec((1, POOL_WIDTH, POOL_WIDTH), lw),
        pl.BlockSpec((1, 1, POOL_WIDTH), lw),
        pl.BlockSpec((1, CONV_K, CONV_WIDTH), lw),
        pl.BlockSpec((1, 1, CONV_WIDTH), lw),
        pl.BlockSpec((1, 1, CONV_WIDTH), lw),
        pl.BlockSpec((1, CONV_WIDTH, CONV_WIDTH), lw),
    ]
    args = [x, x, x, mod, w["w_in"], w["pool_bd"], w["pool_scale"], w["conv_dw"], w["conv_ln_g"],
            w["conv_ln_b"], w["conv_pw"]]
    if rope is not None:
        in_specs += [pl.BlockSpec((tile, LANES), lambda bi, i: (i, 0))] * 2
        args += list(rope)
    tok = lambda width: pl.BlockSpec((1, tile, width), lambda bi, i: (bi, i, 0))
    out_specs = [tok(QK_WIDTH), tok(QK_WIDTH), tok(ATTN_WIDTH)]
    out_shape = [jax.ShapeDtypeStruct((b, seq_len, QK_WIDTH), BF16)] * 3
    aliases = {}
    if want_f32_kv:
        if prev_kv is not None:
            aliases = {len(args): len(out_specs), len(args) + 1: len(out_specs) + 1}
            in_specs += [pl.BlockSpec(memory_space=pl.ANY)] * 2
            args += list(prev_kv)
        if prev_kv is None:
            layer_tok = lambda width: pl.BlockSpec((1, depth, tile, width), lambda bi, i: (bi, 0, i, 0))
        else:
            layer_tok = lambda width: pl.BlockSpec((1, 1, tile, width), lambda bi, i: (bi, layer, i, 0))
        out_specs += [layer_tok(QK_WIDTH), layer_tok(ATTN_WIDTH)]
        out_shape += [jax.ShapeDtypeStruct((b, depth, seq_len, QK_WIDTH), F32)] * 2
    out_specs += [tok(POOL_WIDTH), tok(CONV_WIDTH)]
    out_shape += [jax.ShapeDtypeStruct((b, seq_len, POOL_WIDTH), BF16)] * 2
    kern = functools.partial(_front_kernel, seq_len=seq_len, tile=tile, has_rope=rope is not None,
                             want_f32_kv=want_f32_kv, has_prev_kv=prev_kv is not None, kv_slot=layer)
    return pl.pallas_call(
        kern, grid=(b, nt), in_specs=in_specs, out_specs=out_specs, out_shape=out_shape,
        input_output_aliases=aliases,
        compiler_params=_params(("parallel", "parallel")), name="front",
    )(*args)


def _attn_kernel(*refs, lam_init, has_cache):
    if has_cache:
        lam_ref, g_ref, q_ref, k_ref, v_ref, ck_ref, cv_ref, o_ref = refs
    else:
        lam_ref, g_ref, q_ref, k_ref, v_ref, o_ref = refs
    lp = lam_ref[0]
    lam = (jnp.exp(jnp.sum(lp[0:1] * lp[1:2], axis=-1, keepdims=True))
           - jnp.exp(jnp.sum(lp[2:3] * lp[3:4], axis=-1, keepdims=True)) + lam_init)

    lane = lax.broadcasted_iota(jnp.int32, (1, LANES), 1)
    nt = (((1,), (1,)), ((), ()))
    heads = q_ref.shape[2] // LANES
    kv = []
    for hh in range(heads):
        cols = slice(hh * LANES, (hh + 1) * LANES)
        keys = [k_ref[0, :, cols]]
        vals = [v_ref[0, :, cols]]
        if has_cache:
            keys.append(ck_ref[0, 0, :, cols].astype(BF16))
            vals.append(cv_ref[0, 0, :, cols].astype(BF16))
        kv.append((keys, vals))

    def scores(q, keys):
        zero = jnp.zeros_like(q)
        return [[lax.dot_general(qm, kk, nt, preferred_element_type=F32) for kk in keys]
                for qm in (jnp.where(lane < HEAD_DIM, q, zero), jnp.where(lane >= HEAD_DIM, q, zero))]

    def weights(score_maps):
        parts = []
        for s in score_maps:
            m = functools.reduce(jnp.maximum, [jnp.max(z, axis=-1, keepdims=True) for z in s])
            e = [jnp.exp2(z - m) for z in s]
            den = functools.reduce(jnp.add, [jnp.sum(z, axis=-1, keepdims=True) for z in e])
            parts.append((e, den))
        (e1, den1), (e2, den2) = parts
        c = lam * den1 / den2
        return [(z1 - z2 * c).astype(BF16) for z1, z2 in zip(e1, e2)], den1

    def values(a_parts, den1, vals):
        o = functools.reduce(jnp.add, [_dot(a, vv) for a, vv in zip(a_parts, vals)])
        o = o * (1.0 / den1)
        o = o * lax.rsqrt(jnp.mean(o * o, axis=-1, keepdims=True) + LN_EPS) * g_ref[0] * (1.0 - lam_init)
        return o.astype(BF16)

    tq = q_ref.shape[1]
    n_sub = max(1, tq // ATTN_ROWS)
    step = tq // n_sub
    items = [(hh, j) for hh in range(heads) for j in range(n_sub)]

    def item_scores(item):
        hh, j = item
        return scores(q_ref[0, j * step:(j + 1) * step, hh * LANES:(hh + 1) * LANES], kv[hh][0])

    def item_store(item, a_parts, den1):
        hh, j = item
        o_ref[0, j * step:(j + 1) * step, hh * LANES:(hh + 1) * LANES] = values(a_parts, den1, kv[hh][1])

    pending = item_scores(items[0])
    softmaxed = None
    for idx, item in enumerate(items):
        current = pending
        if idx + 1 < len(items):
            pending = item_scores(items[idx + 1])
        done = weights(current)
        if softmaxed is not None:
            item_store(items[idx - 1], *softmaxed)
        softmaxed = done
    item_store(items[-1], *softmaxed)


def _attn_call(q, k, v, cache, layer, w, lam_init, tq, heads):
    b, seq_len, _ = q.shape
    width = heads * LANES
    lw = lambda bi, h, i: (layer, 0, 0)
    in_specs = [
        pl.BlockSpec((1, 4, HEAD_DIM), lw),
        pl.BlockSpec((1, 1, V_DIM), lw),
        pl.BlockSpec((1, tq, width), lambda bi, h, i: (bi, i, h)),
        pl.BlockSpec((1, seq_len, width), lambda bi, h, i: (bi, 0, h)),
        pl.BlockSpec((1, seq_len, width), lambda bi, h, i: (bi, 0, h)),
    ]
    args = [w["diff_lambda"], w["subln_g"], q, k, v]
    if cache is not None:
        past = cache[0].shape[2]
        cspec = pl.BlockSpec((1, 1, past, width), lambda bi, h, i: (bi, layer, 0, h))
        in_specs += [cspec, cspec]
        args += list(cache)
    kern = functools.partial(_attn_kernel, lam_init=lam_init, has_cache=cache is not None)
    return pl.pallas_call(
        kern, grid=(b, ATTN_HEADS // heads, seq_len // tq), in_specs=in_specs,
        out_specs=pl.BlockSpec((1, tq, width), lambda bi, h, i: (bi, i, h)),
        out_shape=jax.ShapeDtypeStruct((b, seq_len, ATTN_WIDTH), BF16),
        compiler_params=_params(("parallel", "parallel", "parallel")), name="attn",
    )(*args)


def _post_kernel(yp_ref, ya_ref, yc_ref, x_ref, mod_ref, wout_ref, g_ref, b_ref, wr_ref, br_ref,
                 x1_ref, u2c_ref, sel_ref, *, alpha):
    o1 = POOL_WIDTH
    o2 = o1 + ATTN_WIDTH
    y = (_dot(yp_ref[0], wout_ref[0, 0:o1, :]) + _dot(ya_ref[0], wout_ref[0, o1:o2, :])
         + _dot(yc_ref[0], wout_ref[0, o2:, :]))
    gate1 = mod_ref[0, 2:3, :]
    shift2 = mod_ref[0, 3:4, :]
    scale2 = mod_ref[0, 4:5, :]
    x1 = _layer_norm(alpha * x_ref[0] + gate1 * y, g_ref[0], b_ref[0])
    x1_ref[0] = x1
    u2 = x1 * (1.0 + scale2) + shift2
    tile = u2.shape[0]
    for c in range(ROW_CHUNKS):
        u2c_ref[:, c, :, :] = u2[:, c * LANES:(c + 1) * LANES].reshape(tile // SUBLANES, SUBLANES, LANES)

    u_hi = u2.astype(BF16)
    u_lo = (u2 - u_hi.astype(F32)).astype(BF16)
    both = _dot(u_hi, wr_ref[0]) + _dot(u_lo, wr_ref[0])
    logits = both[:, :ROUTER_LANES] + both[:, ROUTER_LANES:] + br_ref[0]
    lane = lax.broadcasted_iota(jnp.int32, (1, ROUTER_LANES), 1)
    lane_f = lane.astype(F32)
    is_g = (lane >= N_EXPERTS) & (lane < N_EXPERTS + N_GROUPS)
    rmax = lambda z: jnp.max(z, axis=-1, keepdims=True)
    rmin = lambda z: jnp.min(z, axis=-1, keepdims=True)
    gl = jnp.where(is_g, logits, NEG_BIG)
    ge = jnp.where(is_g, jnp.exp(gl - rmax(gl)), 0.0)
    gp = ge / jnp.sum(ge, axis=-1, keepdims=True)
    g_w = rmax(gp)
    g_idx = rmin(jnp.where(is_g & (gp == g_w), lane_f - N_EXPERTS, 1e9))
    in_grp = (lane < N_EXPERTS) & (jnp.right_shift(lane, 3).astype(F32) == g_idx)
    el = jnp.where(in_grp, logits, NEG_BIG)
    v1 = rmax(el)
    i1 = rmin(jnp.where(in_grp & (el == v1), lane_f, 1e9))
    rest = in_grp & (lane_f != i1)
    el2 = jnp.where(rest, logits, NEG_BIG)
    v2 = rmax(el2)
    i2 = rmin(jnp.where(rest & (el2 == v2), lane_f, 1e9))
    tt = jnp.exp(v2 - v1)
    w1 = 1.0 / (1.0 + tt)
    w2 = tt / (1.0 + tt)
    sel = jnp.zeros((tile, ROUTER_LANES), F32)
    for j, col in enumerate((i1, i2, g_w * w1, g_w * w2)):
        sel = jnp.where(lane == j, col, sel)
    sel_ref[0] = sel


def _post_call(y_pool, y_attn, y_conv, x, mod, layer, w, alpha, tile):
    b, seq_len, d = x.shape
    nt = seq_len // tile
    mod_b = (lambda bi, i: (bi, 0, 0)) if mod.shape[0] == b else (lambda bi, i: (0, 0, 0))
    lw = lambda bi, i: (layer, 0, 0)
    tok = lambda width: pl.BlockSpec((1, tile, width), lambda bi, i: (bi, i, 0))
    in_specs = [tok(POOL_WIDTH), tok(ATTN_WIDTH), tok(CONV_WIDTH), tok(d),
                pl.BlockSpec((1, 6, d), mod_b),
                pl.BlockSpec((1, d, d), lw),
                pl.BlockSpec((1, 1, d), lw), pl.BlockSpec((1, 1, d), lw),
                pl.BlockSpec((1, d, 2 * ROUTER_LANES), lw), pl.BlockSpec((1, 1, ROUTER_LANES), lw)]
    out_specs = [tok(d),
                 pl.BlockSpec((tile // SUBLANES, ROW_CHUNKS, SUBLANES, LANES), lambda bi, i: (bi * nt + i, 0, 0, 0)),
                 tok(ROUTER_LANES)]
    out_shape = [jax.ShapeDtypeStruct((b, seq_len, d), F32),
                 jax.ShapeDtypeStruct((b * seq_len // SUBLANES, ROW_CHUNKS, SUBLANES, LANES), F32),
                 jax.ShapeDtypeStruct((b, seq_len, ROUTER_LANES), F32)]
    return pl.pallas_call(
        functools.partial(_post_kernel, alpha=alpha),
        grid=(b, nt), in_specs=in_specs, out_specs=out_specs, out_shape=out_shape,
        compiler_params=_params(("parallel", "parallel")), name="post",
    )(y_pool, y_attn, y_conv, x, mod, w["w_out"], w["ln1_g"], w["ln1_b"], w["router_w"], w["router_b"])


def _route_kernel(sel_ref, ri_ref, rw_ref, seg_ref, *, tile):
    lane = lax.broadcasted_iota(jnp.int32, (1, ROUTER_LANES), 1)
    lane_f = lane.astype(F32)
    blk = ROUTE_BLOCK
    r_io = lax.broadcasted_iota(jnp.int32, (blk, blk), 0)
    c_io = lax.broadcasted_iota(jnp.int32, (blk, blk), 1)
    before = jnp.where(r_io > c_io, 1.0, 0.0).astype(BF16)

    def picks(j):
        s = sel_ref[j * blk:(j + 1) * blk, :]
        return s, lane_f == s[:, 0:1], lane_f == s[:, 1:2]

    carry = jnp.zeros((1, ROUTER_LANES), F32)
    prior = []
    for j in range(tile // blk):
        _, oh1, oh2 = picks(j)
        member = jnp.where(oh1 | oh2, 1.0, 0.0)
        prior.append(_dot(before, member.astype(BF16)) + carry)
        carry = carry + jnp.sum(member, axis=0, keepdims=True)

    cnt = jnp.broadcast_to(carry, (SUBLANES, ROUTER_LANES)).astype(jnp.int32)
    padded = jnp.where(lane < N_EXPERTS, jnp.right_shift(cnt + (SUBLANES - 1), 3) * SUBLANES, 0)
    incl = padded
    step = 1
    while step < N_EXPERTS:
        incl = incl + jnp.where(lane >= step, pltpu.roll(incl, step, axis=1), 0)
        step *= 2
    start = incl - padded
    seg_ref[0] = jnp.where(lane < N_EXPERTS, cnt, pltpu.roll(start, N_EXPERTS, axis=1))
    start_f = start[0:1, :].astype(F32)

    for j in range(tile // blk):
        s, oh1, oh2 = picks(j)
        cols = []
        for oh in (oh1, oh2):
            p = jnp.sum(jnp.where(oh, start_f + prior[j], 0.0), axis=-1, keepdims=True).astype(jnp.int32)
            addr = jnp.right_shift(p, 3) * (SUBLANES * ROW_CHUNKS) + jnp.bitwise_and(p, SUBLANES - 1)
            cols.append(addr.astype(F32))
        z = jnp.where(lane == 0, cols[0], jnp.where(lane == 1, cols[1], s))
        zt = z.T
        ri_ref[0, :, j * blk:(j + 1) * blk] = zt[0:2, :].astype(jnp.int32)
        rw_ref[0, :, j * blk:(j + 1) * blk] = zt[2:4, :]


def _route_call(sel, tile):
    n_tok = sel.shape[0]
    n_tiles = n_tok // tile
    return pl.pallas_call(
        functools.partial(_route_kernel, tile=tile),
        grid=(n_tiles,),
        in_specs=[pl.BlockSpec((tile, ROUTER_LANES), lambda t: (t, 0))],
        out_specs=[pl.BlockSpec((1, TOP_K, tile), lambda t: (t, 0, 0)),
                   pl.BlockSpec((1, TOP_K, tile), lambda t: (t, 0, 0)),
                   pl.BlockSpec((1, SUBLANES, ROUTER_LANES), lambda t: (t, 0, 0))],
        out_shape=[jax.ShapeDtypeStruct((n_tiles, TOP_K, tile), jnp.int32),
                   jax.ShapeDtypeStruct((n_tiles, TOP_K, tile), F32),
                   jax.ShapeDtypeStruct((n_tiles, SUBLANES, ROUTER_LANES), jnp.int32)],
        compiler_params=_params(("parallel",)), name="route",
    )(sel)


def _row_at(start):
    return pl.ds(start, ROW_CHUNKS, stride=SUBLANES)


def _moe_kernel(ri_ref, rw_ref, seg_ref, u2c_ref, wg_ref, wu_ref, wd_ref, o_ref, sorted_ref, *, tile):
    step = pl.program_id(1)
    groups = tile // SUBLANES
    group_rows = SUBLANES * ROW_CHUNKS

    @pl.when((pl.program_id(0) == 0) & (step == 0))
    def _():
        sorted_ref[...] = jnp.zeros_like(sorted_ref)

    @pl.when(step == 0)
    def _():
        def disp_body(a, carry):
            for s in range(SUBLANES):
                row = u2c_ref[_row_at(a * group_rows + s), :]
                for k in range(TOP_K):
                    sorted_ref[_row_at(ri_ref[k * tile + a * SUBLANES + s]), :] = row
            return carry
        lax.fori_loop(0, groups, disp_body, 0)

    blk_groups = EXPERT_BLOCK // SUBLANES
    row_id = lax.broadcasted_iota(jnp.int32, (SUBLANES, 1), 0)

    def tile_at(start, g, c):
        return pl.ds(pl.multiple_of(start + (g * ROW_CHUNKS + c) * SUBLANES, SUBLANES), SUBLANES)

    experts = [step * EXPERTS_PER_STEP + j for j in range(EXPERTS_PER_STEP)]
    cnts = [seg_ref[0, 0, e] for e in experts]
    bases = [seg_ref[0, 0, N_EXPERTS + e] for e in experts]
    trip_rows = EXPERT_BLOCK * BLOCKS_PER_TRIP

    def blk_body(bi, carry):
        chains = [(j, (bi * BLOCKS_PER_TRIP + u) * EXPERT_BLOCK)
                  for j in range(EXPERTS_PER_STEP) for u in range(BLOCKS_PER_TRIP)]
        starts = [jnp.where(first < cnts[j], (bases[j] + first) * ROW_CHUNKS, 0) for j, first in chains]
        xs = [jnp.concatenate(
            [jnp.concatenate([sorted_ref[tile_at(start, g, c), :] for g in range(blk_groups)], axis=0)
             for c in range(ROW_CHUNKS)], axis=1).astype(BF16) for start in starts]
        hs = [_silu(_dot(x, wg_ref[0, j])) * _dot(x, wu_ref[0, j]) for x, (j, _) in zip(xs, chains)]
        ys = [_dot(h.astype(BF16), wd_ref[0, j]) for h, (j, _) in zip(hs, chains)]
        for (j, first), start, y in zip(chains, starts, ys):
            n_valid = cnts[j] - first
            for g in range(blk_groups):
                keep = (row_id + g * SUBLANES) < n_valid
                for c in range(ROW_CHUNKS):
                    sl = tile_at(start, g, c)
                    piece = y[g * SUBLANES:(g + 1) * SUBLANES, c * LANES:(c + 1) * LANES]
                    sorted_ref[sl, :] = jnp.where(keep, piece, sorted_ref[sl, :])
        return carry
    trips = functools.reduce(jnp.maximum, [(cnt + trip_rows - 1) // trip_rows for cnt in cnts])
    lax.fori_loop(0, trips, blk_body, 0)

    @pl.when(step == pl.num_programs(1) - 1)
    def _():
        def comb_body(a, carry):
            for s in range(SUBLANES):
                n = a * SUBLANES + s
                acc = None
                for k in range(TOP_K):
                    term = sorted_ref[_row_at(ri_ref[k * tile + n]), :] * rw_ref[k * tile + n]
                    acc = term if acc is None else acc + term
                o_ref[_row_at(a * group_rows + s), :] = acc
            return carry
        lax.fori_loop(0, groups, comb_body, 0)


def _moe_call(u2c, ri, rw, seg, layer, w, tile):
    n_tok = u2c.shape[0] // ROW_CHUNKS
    n_tiles = n_tok // tile
    d = D_MODEL
    sorted_rows = tile * TOP_K + N_EXPERTS * (SUBLANES - 1) + EXPERT_BLOCK * BLOCKS_PER_TRIP
    sorted_rows = -(-sorted_rows // SUBLANES) * SUBLANES
    eps = EXPERTS_PER_STEP
    smem = lambda shape, imap: pl.BlockSpec(shape, imap, memory_space=pltpu.SMEM)
    in_specs = [smem((TOP_K * tile,), lambda t, e: (t,)),
                smem((TOP_K * tile,), lambda t, e: (t,)),
                smem((1, SUBLANES, ROUTER_LANES), lambda t, e: (t, 0, 0)),
                pl.BlockSpec((tile * ROW_CHUNKS, LANES), lambda t, e: (t, 0), pipeline_mode=pl.Buffered(1)),
                pl.BlockSpec((1, eps, d, EXPERT_FF), lambda t, e: (layer, e, 0, 0)),
                pl.BlockSpec((1, eps, d, EXPERT_FF), lambda t, e: (layer, e, 0, 0)),
                pl.BlockSpec((1, eps, EXPERT_FF, d), lambda t, e: (layer, e, 0, 0))]
    return pl.pallas_call(
        functools.partial(_moe_kernel, tile=tile),
        grid=(n_tiles, N_EXPERTS // eps), in_specs=in_specs,
        out_specs=pl.BlockSpec((tile * ROW_CHUNKS, LANES), lambda t, e: (t, 0)),
        out_shape=jax.ShapeDtypeStruct((n_tok * ROW_CHUNKS, LANES), F32),
        scratch_shapes=[pltpu.VMEM((sorted_rows * ROW_CHUNKS, LANES), F32)],
        compiler_params=_params(("arbitrary", "arbitrary")), name="moe",
    )(ri.reshape(-1), rw.reshape(-1), seg, u2c, w["w_gate"], w["w_up"], w["w_down"])


def _ln2_kernel(m_ref, x1_ref, mod_ref, g_ref, b_ref, o_ref, *, alpha):
    tile = x1_ref.shape[0]
    moe = jnp.concatenate([m_ref[:, c, :, :].reshape(tile, LANES) for c in range(ROW_CHUNKS)], axis=1)
    gate2 = mod_ref[0, 5:6, :]
    o_ref[...] = _layer_norm(alpha * x1_ref[...] + gate2 * moe, g_ref[0], b_ref[0])


def _ln2_call(moe_c, x1, mod, layer, w, alpha, tile):
    b, seq_len, d = x1.shape
    n_tok = b * seq_len
    per_seq = seq_len // tile
    mod_b = (lambda t: (t // per_seq, 0, 0)) if mod.shape[0] == b else (lambda t: (0, 0, 0))
    lw = lambda t: (layer, 0, 0)
    out = pl.pallas_call(
        functools.partial(_ln2_kernel, alpha=alpha),
        grid=(n_tok // tile,),
        in_specs=[pl.BlockSpec((tile // SUBLANES, ROW_CHUNKS, SUBLANES, LANES), lambda t: (t, 0, 0, 0)),
                  pl.BlockSpec((tile, d), lambda t: (t, 0)),
                  pl.BlockSpec((1, 6, d), mod_b),
                  pl.BlockSpec((1, 1, d), lw), pl.BlockSpec((1, 1, d), lw)],
        out_specs=pl.BlockSpec((tile, d), lambda t: (t, 0)),
        out_shape=jax.ShapeDtypeStruct((n_tok, d), F32),
        compiler_params=_params(("parallel",)), name="ln2",
    )(moe_c.reshape(n_tok // SUBLANES, ROW_CHUNKS, SUBLANES, LANES), x1.reshape(n_tok, d), mod,
      w["ln2_g"], w["ln2_b"])
    return out.reshape(b, seq_len, d)


def _rope_tables(seq_len):
    rows = seq_len // GRID_W
    row_id = jnp.repeat(jnp.arange(rows), GRID_W).astype(F32)
    col_id = jnp.tile(jnp.arange(GRID_W), rows).astype(F32)
    inv = ROPE_THETA ** (-jnp.arange(0, AXIS_DIM, 2, dtype=F32) / AXIS_DIM)
    ang = jnp.stack([row_id[:, None] * inv, col_id[:, None] * inv], axis=1)
    cos = jnp.cos(ang)[:, :, None, :]
    sin = jnp.sin(ang)[:, :, None, :]
    shape = (seq_len, 2, 2, AXIS_DIM // 2)
    cos = jnp.broadcast_to(cos, shape).reshape(seq_len, HEAD_DIM)
    sin = (jnp.broadcast_to(sin, shape) * jnp.array([-1.0, 1.0], F32)[None, None, :, None]).reshape(seq_len, HEAD_DIM)
    return jnp.tile(cos, (1, LANES // HEAD_DIM)), jnp.tile(sin, (1, LANES // HEAD_DIM))


def _trunk_layer(x, mod, layer, w, lam_init, alpha, rope, cache, new_kv, tiles):
    outs = _front_call(x, mod, layer, w, rope, new_kv, tiles["front"])
    if new_kv is not None:
        q, k, v, kf, vf, y_pool, y_conv = outs
    else:
        q, k, v, y_pool, y_conv = outs
        kf = vf = None
    y_attn = _attn_call(q, k, v, cache, layer, w, lam_init, tiles["attn"], tiles["attn_heads"])
    x1, u2c, sel = _post_call(y_pool, y_attn, y_conv, x, mod, layer, w, alpha, tiles["post"])
    ri, rw, seg = _route_call(sel.reshape(-1, ROUTER_LANES), tiles["moe"])
    moe_c = _moe_call(u2c.reshape(-1, LANES), ri, rw, seg, layer, w, tiles["moe"])
    x2 = _ln2_call(moe_c, x1, mod, layer, w, alpha, tiles["ln2"])
    return x2, kf, vf


def kernel(x_prompt, x_sample, cache_k, cache_v, c, c_ctx, w_mod, b_mod, w_in, w_out, pool_w, pool_scale, diff_lambda, subln_g, conv_dw, conv_ln_g, conv_ln_b, conv_pw, ln1_g, ln1_b, ln2_g, ln2_b, router_grp, router_grp_b, router_exp, router_exp_b, w_gate, w_up, w_down):
    depth = w_in.shape[0]
    batch, seq, d = x_prompt.shape
    dec_batch, dec_seq, _ = x_sample.shape
    past = cache_k.shape[2]
    alpha = (2 * depth) ** 0.25

    mod_rows = 16
    cvec = jnp.concatenate([c, c_ctx[None], jnp.zeros((mod_rows - dec_batch - 1, d), F32)], axis=0)
    mod_all = _mod_call(cvec, w_mod, b_mod)

    eye = jnp.eye(len(POOL_WINDOWS), dtype=F32)
    pool_bd = (pool_w[:, :, :, None, :] * eye[None, :, None, :, None]).reshape(depth, POOL_WIDTH, POOL_WIDTH)
    router_w = jnp.concatenate(
        [router_exp, router_grp, jnp.zeros((depth, d, ROUTER_LANES - N_EXPERTS - N_GROUPS), F32)], axis=-1)
    router_hi = router_w.astype(BF16)
    router_lo = (router_w - router_hi.astype(F32)).astype(BF16)
    router_b = jnp.concatenate(
        [router_exp_b, router_grp_b, jnp.zeros((depth, ROUTER_LANES - N_EXPERTS - N_GROUPS), F32)], axis=-1)
    r3 = lambda z: z.reshape(depth, 1, z.shape[-1])
    w = {
        "w_in": w_in.astype(BF16), "w_out": w_out.astype(BF16), "pool_bd": pool_bd.astype(BF16),
        "pool_scale": r3(pool_scale), "diff_lambda": diff_lambda, "subln_g": r3(subln_g),
        "conv_dw": conv_dw, "conv_ln_g": r3(conv_ln_g), "conv_ln_b": r3(conv_ln_b),
        "conv_pw": conv_pw.astype(BF16), "ln1_g": r3(ln1_g), "ln1_b": r3(ln1_b), "ln2_g": r3(ln2_g),
        "ln2_b": r3(ln2_b), "router_w": jnp.concatenate([router_hi, router_lo], axis=-1), "router_b": r3(router_b),
        "w_gate": w_gate.astype(BF16), "w_up": w_up.astype(BF16), "w_down": w_down.astype(BF16),
    }
    rope = _rope_tables(dec_seq)
    cache = (cache_k.reshape(dec_batch, depth, past, QK_WIDTH), cache_v.reshape(dec_batch, depth, past, ATTN_WIDTH))
    ctx_tiles = {"front": seq, "attn": seq, "attn_heads": ATTN_HEADS, "post": seq, "moe": 2048, "ln2": seq}
    lat_tiles = {"front": 512, "attn": 2048, "attn_heads": 1, "post": 512, "moe": 2048, "ln2": 512}

    xp, xs = x_prompt, x_sample
    new_kv = None
    for l in range(depth):
        lam_init = 0.8 - 0.6 * math.exp(-0.3 * l)
        mod_ctx = mod_all[l, dec_batch:dec_batch + 1].reshape(1, 6, d)
        mod_lat = mod_all[l, :dec_batch].reshape(dec_batch, 6, d)
        xp, kf, vf = _trunk_layer(xp, mod_ctx, l, w, lam_init, alpha, None, None, (depth, new_kv), ctx_tiles)
        new_kv = (kf, vf)
        xs, _, _ = _trunk_layer(xs, mod_lat, l, w, lam_init, alpha, rope, cache, None, lat_tiles)
    return (xp, xs, new_kv[0].reshape(batch, depth, seq, ATTN_HEADS, 2, HEAD_DIM),
            new_kv[1].reshape(batch, depth, seq, ATTN_HEADS, V_DIM))
```

```python
import functools
import math

import jax
import jax.numpy as jnp
from jax import lax
from jax.experimental import pallas as pl
from jax.experimental.pallas import tpu as pltpu

F32 = jnp.float32
BF16 = jnp.bfloat16

D_MODEL = 1024
GRID_W = 64
POOL_WIDTH = 256
POOL_WINDOWS = (2, 4, 8, 16)
POOL_GC = 64
ATTN_HEADS = 4
HEAD_DIM = 64
V_DIM = 128
ATTN_WIDTH = 512
QK_WIDTH = 512
CONV_WIDTH = 256
CONV_K = 31
IN_WIDTH = 2304
AXIS_DIM = 32
ROPE_THETA = 10000.0
N_GROUPS = 4
EXPERTS_PER_GROUP = 8
N_EXPERTS = 32
EXPERT_FF = 256
LN_EPS = 1e-5

O_A = 0
O_Q = POOL_WIDTH
O_K = O_Q + QK_WIDTH
O_V = O_K + QK_WIDTH
O_C = O_V + ATTN_WIDTH

TOP_K = 2
HALO = 16
LANES = 128
SUBLANES = 8
ROW_CHUNKS = D_MODEL // LANES
ROUTER_LANES = 128
ROUTE_BLOCK = 512
EXPERT_BLOCK = 128
BLOCKS_PER_TRIP = 1
EXPERTS_PER_STEP = 4
ATTN_ROWS = 128
LOG2E = 1.4426950408889634
NEG_BIG = -1e30
VMEM_LIMIT = 56 * 1024 * 1024


def _silu(x):
    return x * jax.nn.sigmoid(x)


def _layer_norm(x, g, b):
    mu = jnp.mean(x, axis=-1, keepdims=True)
    xc = x - mu
    var = jnp.mean(xc * xc, axis=-1, keepdims=True)
    return xc * lax.rsqrt(var + LN_EPS) * g + b


def _dot(a, b):
    return jnp.dot(a, b, preferred_element_type=F32)


def _params(sem):
    return pltpu.CompilerParams(dimension_semantics=sem, vmem_limit_bytes=VMEM_LIMIT)


def _mod_kernel(c_ref, w_ref, b_ref, o_ref):
    s = _silu(c_ref[...])
    o_ref[0] = _dot(s.astype(BF16), w_ref[0].astype(BF16)) + b_ref[0]


def _mod_call(cvec, w_mod, b_mod):
    depth, d, n6 = w_mod.shape
    rows = cvec.shape[0]
    nb = 1536
    return pl.pallas_call(
        _mod_kernel,
        grid=(depth, n6 // nb),
        in_specs=[
            pl.BlockSpec((rows, d), lambda l, j: (0, 0)),
            pl.BlockSpec((1, d, nb), lambda l, j: (l, 0, j)),
            pl.BlockSpec((1, 1, nb), lambda l, j: (l, 0, j)),
        ],
        out_specs=pl.BlockSpec((1, rows, nb), lambda l, j: (l, 0, j)),
        out_shape=jax.ShapeDtypeStruct((depth, rows, n6), F32),
        compiler_params=_params(("parallel", "parallel")),
        name="mod",
    )(cvec, w_mod, b_mod.reshape(depth, 1, n6))


def _front_kernel(*refs, seq_len, tile, has_rope, want_f32_kv, has_prev_kv, kv_slot):
    it = iter(refs)
    x_ref, xp_ref, xn_ref, mod_ref, win_ref = next(it), next(it), next(it), next(it), next(it)
    poolw_ref, pscale_ref, dw_ref, cg_ref, cb_ref, pw_ref = (next(it) for _ in range(6))
    if has_rope:
        cos_ref, sin_ref = next(it), next(it)
    if has_prev_kv:
        next(it), next(it)
    q_ref, k_ref, v_ref = next(it), next(it), next(it)
    if want_f32_kv:
        kf_ref, vf_ref = next(it), next(it)
    yp_ref, yc_ref = next(it), next(it)

    i = pl.program_id(1)
    n_ext = tile + 2 * HALO
    shift = mod_ref[0, 0:1, :]
    scale = mod_ref[0, 1:2, :]
    xe = jnp.concatenate([xp_ref[0], x_ref[0], xn_ref[0]], axis=0)
    u = (xe * (1.0 + scale) + shift).astype(BF16)
    proj = _dot(u, win_ref[0])

    row = lax.broadcasted_iota(jnp.int32, (n_ext, 1), 0)
    gpos = i * tile - HALO + row
    valid = (gpos >= 0) & (gpos < seq_len)

    a_ext = jnp.where(valid, proj[:, O_A:O_A + POOL_WIDTH], 0.0)

    def sh(z, s):
        return pltpu.roll(z, s % n_ext, axis=0)

    p2 = a_ext + sh(a_ext, 1)
    p4 = sh(p2, 1) + sh(p2, -1)
    p8 = sh(p4, 2) + sh(p4, -2)
    p16 = sh(p8, 4) + sh(p8, -4)
    t = gpos[HALO:HALO + tile]
    lane = lax.broadcasted_iota(jnp.int32, (1, POOL_WIDTH), 1)
    mean = None
    for g, (w, p) in enumerate(zip(POOL_WINDOWS, (p2, p4, p8, p16))):
        cnt = jnp.minimum(t + w // 2, seq_len) - jnp.maximum(t - w // 2, 0)
        m = p[HALO:HALO + tile] * (1.0 / cnt.astype(F32))
        mean = m if mean is None else jnp.where(lane >= g * POOL_GC, m, mean)
    pooled = mean - a_ext[HALO:HALO + tile]
    y_pool = _dot(pooled.astype(BF16), poolw_ref[0]) * pscale_ref[0]
    yp_ref[0] = y_pool.astype(BF16)

    ca = proj[:, O_C:O_C + CONV_WIDTH]
    cb = proj[:, O_C + CONV_WIDTH:O_C + 2 * CONV_WIDTH]
    h = jnp.where(valid, ca * jax.nn.sigmoid(cb), 0.0)
    acc = None
    for r in range(SUBLANES):
        hr = h if r == 0 else pltpu.roll(h, n_ext - r, axis=0)
        for j in range(CONV_K):
            off = HALO - CONV_K // 2 + j
            if off % SUBLANES != r:
                continue
            term = hr[off - r:off - r + tile] * dw_ref[0, j:j + 1, :]
            acc = term if acc is None else acc + term
    hn = _silu(_layer_norm(acc, cg_ref[0], cb_ref[0]))
    yc_ref[0] = _dot(hn.astype(BF16), pw_ref[0]).astype(BF16)

    qf = proj[HALO:HALO + tile, O_Q:O_Q + QK_WIDTH]
    kf = proj[HALO:HALO + tile, O_K:O_K + QK_WIDTH]
    vf = proj[HALO:HALO + tile, O_V:O_V + ATTN_WIDTH]
    if want_f32_kv:
        for slot in range(kf_ref.shape[1]):
            mine = has_prev_kv or slot == kv_slot
            kf_ref[0, slot] = kf if mine else jnp.zeros_like(kf)
            vf_ref[0, slot] = vf if mine else jnp.zeros_like(vf)
    if has_rope:
        cos = cos_ref[...]
        sin = sin_ref[...]
        l128 = lax.broadcasted_iota(jnp.int32, (1, LANES), 1)
        first_half = (l128 & (AXIS_DIM - 1)) < (AXIS_DIM // 2)

        def rope(z):
            outs = []
            for c in range(QK_WIDTH // LANES):
                zc = z[:, c * LANES:(c + 1) * LANES]
                partner = jnp.where(first_half,
                                    pltpu.roll(zc, LANES - AXIS_DIM // 2, axis=1),
                                    pltpu.roll(zc, AXIS_DIM // 2, axis=1))
                outs.append(zc * cos + partner * sin)
            return jnp.concatenate(outs, axis=1)

        qf = rope(qf)
        kf = rope(kf)
    q_ref[0] = (qf * (LOG2E / math.sqrt(HEAD_DIM))).astype(BF16)
    k_ref[0] = kf.astype(BF16)
    v_ref[0] = vf.astype(BF16)


def _front_call(x, mod, layer, w, rope, new_kv, tile):
    want_f32_kv = new_kv is not None
    depth, prev_kv = new_kv if want_f32_kv else (None, None)
    b, seq_len, d = x.shape
    nt = seq_len // tile
    hb = tile // HALO
    n_hblk = seq_len // HALO
    mod_b = (lambda bi, i: (bi, 0, 0)) if mod.shape[0] == b else (lambda bi, i: (0, 0, 0))
    lw = lambda bi, i: (layer, 0, 0)
    in_specs = [
        pl.BlockSpec((1, tile, d), lambda bi, i: (bi, i, 0)),
        pl.BlockSpec((1, HALO, d), lambda bi, i: (bi, jnp.maximum(i * hb - 1, 0), 0)),
        pl.BlockSpec((1, HALO, d), lambda bi, i: (bi, jnp.minimum((i + 1) * hb, n_hblk - 1), 0)),
        pl.BlockSpec((1, 6, d), mod_b),
        pl.BlockSpec((1, d, IN_WIDTH), lw),
        pl.BlockSpec((1, POOL_WIDTH, POOL_WIDTH), lw),
        pl.BlockSpec((1, 1, POOL_WIDTH), lw),
        pl.BlockSpec((1, CONV_K, CONV_WIDTH), lw),
        pl.BlockSpec((1, 1, CONV_WIDTH), lw),
        pl.BlockSpec((1, 1, CONV_WIDTH), lw),
        pl.BlockSpec((1, CONV_WIDTH, CONV_WIDTH), lw),
    ]
    args = [x, x, x, mod, w["w_in"], w["pool_bd"], w["pool_scale"], w["conv_dw"], w["conv_ln_g"],
            w["conv_ln_b"], w["conv_pw"]]
    if rope is not None:
        in_specs += [pl.BlockSpec((tile, LANES), lambda bi, i: (i, 0))] * 2
        args += list(rope)
    tok = lambda width: pl.BlockSpec((1, tile, width), lambda bi, i: (bi, i, 0))
    out_specs = [tok(QK_WIDTH), tok(QK_WIDTH), tok(ATTN_WIDTH)]
    out_shape = [jax.ShapeDtypeStruct((b, seq_len, QK_WIDTH), BF16)] * 3
    aliases = {}
    if want_f32_kv:
        if prev_kv is not None:
            aliases = {len(args): len(out_specs), len(args) + 1: len(out_specs) + 1}
            in_specs += [pl.BlockSpec(memory_space=pl.ANY)] * 2
            args += list(prev_kv)
        if prev_kv is None:
            layer_tok = lambda width: pl.BlockSpec((1, depth, tile, width), lambda bi, i: (bi, 0, i, 0))
        else:
            layer_tok = lambda width: pl.BlockSpec((1, 1, tile, width), lambda bi, i: (bi, layer, i, 0))
        out_specs += [layer_tok(QK_WIDTH), layer_tok(ATTN_WIDTH)]
        out_shape += [jax.ShapeDtypeStruct((b, depth, seq_len, QK_WIDTH), F32)] * 2
    out_specs += [tok(POOL_WIDTH), tok(CONV_WIDTH)]
    out_shape += [jax.ShapeDtypeStruct((b, seq_len, POOL_WIDTH), BF16)] * 2
    kern = functools.partial(_front_kernel, seq_len=seq_len, tile=tile, has_rope=rope is not None,
                             want_f32_kv=want_f32_kv, has_prev_kv=prev_kv is not None, kv_slot=layer)
    return pl.pallas_call(
        kern, grid=(b, nt), in_specs=in_specs, out_specs=out_specs, out_shape=out_shape,
        input_output_aliases=aliases,
        compiler_params=_params(("parallel", "parallel")), name="front",
    )(*args)


def _attn_kernel(*refs, lam_init, has_cache):
    if has_cache:
        lam_ref, g_ref, q_ref, k_ref, v_ref, ck_ref, cv_ref, o_ref = refs
    else:
        lam_ref, g_ref, q_ref, k_ref, v_ref, o_ref = refs
    lp = lam_ref[0]
    lam = (jnp.exp(jnp.sum(lp[0:1] * lp[1:2], axis=-1, keepdims=True))
           - jnp.exp(jnp.sum(lp[2:3] * lp[3:4], axis=-1, keepdims=True)) + lam_init)

    lane = lax.broadcasted_iota(jnp.int32, (1, LANES), 1)
    nt = (((1,), (1,)), ((), ()))
    heads = q_ref.shape[2] // LANES
    kv = []
    for hh in range(heads):
        cols = slice(hh * LANES, (hh + 1) * LANES)
        keys = [k_ref[0, :, cols]]
        vals = [v_ref[0, :, cols]]
        if has_cache:
            keys.append(ck_ref[0, 0, :, cols].astype(BF16))
            vals.append(cv_ref[0, 0, :, cols].astype(BF16))
        kv.append((keys, vals))

    def scores(q, keys):
        zero = jnp.zeros_like(q)
        return [[lax.dot_general(qm, kk, nt, preferred_element_type=F32) for kk in keys]
                for qm in (jnp.where(lane < HEAD_DIM, q, zero), jnp.where(lane >= HEAD_DIM, q, zero))]

    def weights(score_maps):
        parts = []
        for s in score_maps:
            m = functools.reduce(jnp.maximum, [jnp.max(z, axis=-1, keepdims=True) for z in s])
            e = [jnp.exp2(z - m) for z in s]
            den = functools.reduce(jnp.add, [jnp.sum(z, axis=-1, keepdims=True) for z in e])
            parts.append((e, den))
        (e1, den1), (e2, den2) = parts
        c = lam * den1 / den2
        return [(z1 - z2 * c).astype(BF16) for z1, z2 in zip(e1, e2)], den1

    def values(a_parts, den1, vals):
        o = functools.reduce(jnp.add, [_dot(a, vv) for a, vv in zip(a_parts, vals)])
        o = o * (1.0 / den1)
        o = o * lax.rsqrt(jnp.mean(o * o, axis=-1, keepdims=True) + LN_EPS) * g_ref[0] * (1.0 - lam_init)
        return o.astype(BF16)

    tq = q_ref.shape[1]
    n_sub = max(1, tq // ATTN_ROWS)
    step = tq // n_sub
    items = [(hh, j) for hh in range(heads) for j in range(n_sub)]

    def item_scores(item):
        hh, j = item
        return scores(q_ref[0, j * step:(j + 1) * step, hh * LANES:(hh + 1) * LANES], kv[hh][0])

    def item_store(item, a_parts, den1):
        hh, j = item
        o_ref[0, j * step:(j + 1) * step, hh * LANES:(hh + 1) * LANES] = values(a_parts, den1, kv[hh][1])

    pending = item_scores(items[0])
    softmaxed = None
    for idx, item in enumerate(items):
        current = pending
        if idx + 1 < len(items):
            pending = item_scores(items[idx + 1])
        done = weights(current)
        if softmaxed is not None:
            item_store(items[idx - 1], *softmaxed)
        softmaxed = done
    item_store(items[-1], *softmaxed)


def _attn_call(q, k, v, cache, layer, w, lam_init, tq, heads):
    b, seq_len, _ = q.shape
    width = heads * LANES
    lw = lambda bi, h, i: (layer, 0, 0)
    in_specs = [
        pl.BlockSpec((1, 4, HEAD_DIM), lw),
        pl.BlockSpec((1, 1, V_DIM), lw),
        pl.BlockSpec((1, tq, width), lambda bi, h, i: (bi, i, h)),
        pl.BlockSpec((1, seq_len, width), lambda bi, h, i: (bi, 0, h)),
        pl.BlockSpec((1, seq_len, width), lambda bi, h, i: (bi, 0, h)),
    ]
    args = [w["diff_lambda"], w["subln_g"], q, k, v]
    if cache is not None:
        past = cache[0].shape[2]
        cspec = pl.BlockSpec((1, 1, past, width), lambda bi, h, i: (bi, layer, 0, h))
        in_specs += [cspec, cspec]
        args += list(cache)
    kern = functools.partial(_attn_kernel, lam_init=lam_init, has_cache=cache is not None)
    return pl.pallas_call(
        kern, grid=(b, ATTN_HEADS // heads, seq_len // tq), in_specs=in_specs,
        out_specs=pl.BlockSpec((1, tq, width), lambda bi, h, i: (bi, i, h)),
        out_shape=jax.ShapeDtypeStruct((b, seq_len, ATTN_WIDTH), BF16),
        compiler_params=_params(("parallel", "parallel", "parallel")), name="attn",
    )(*args)


def _post_kernel(yp_ref, ya_ref, yc_ref, x_ref, mod_ref, wout_ref, g_ref, b_ref, wr_ref, br_ref,
                 x1_ref, u2c_ref, sel_ref, *, alpha):
    o1 = POOL_WIDTH
    o2 = o1 + ATTN_WIDTH
    y = (_dot(yp_ref[0], wout_ref[0, 0:o1, :]) + _dot(ya_ref[0], wout_ref[0, o1:o2, :])
         + _dot(yc_ref[0], wout_ref[0, o2:, :]))
    gate1 = mod_ref[0, 2:3, :]
    shift2 = mod_ref[0, 3:4, :]
    scale2 = mod_ref[0, 4:5, :]
    x1 = _layer_norm(alpha * x_ref[0] + gate1 * y, g_ref[0], b_ref[0])
    x1_ref[0] = x1
    u2 = x1 * (1.0 + scale2) + shift2
    tile = u2.shape[0]
    for c in range(ROW_CHUNKS):
        u2c_ref[:, c, :, :] = u2[:, c * LANES:(c + 1) * LANES].reshape(tile // SUBLANES, SUBLANES, LANES)

    u_hi = u2.astype(BF16)
    u_lo = (u2 - u_hi.astype(F32)).astype(BF16)
    both = _dot(u_hi, wr_ref[0]) + _dot(u_lo, wr_ref[0])
    logits = both[:, :ROUTER_LANES] + both[:, ROUTER_LANES:] + br_ref[0]
    lane = lax.broadcasted_iota(jnp.int32, (1, ROUTER_LANES), 1)
    lane_f = lane.astype(F32)
    is_g = (lane >= N_EXPERTS) & (lane < N_EXPERTS + N_GROUPS)
    rmax = lambda z: jnp.max(z, axis=-1, keepdims=True)
    rmin = lambda z: jnp.min(z, axis=-1, keepdims=True)
    gl = jnp.where(is_g, logits, NEG_BIG)
    ge = jnp.where(is_g, jnp.exp(gl - rmax(gl)), 0.0)
    gp = ge / jnp.sum(ge, axis=-1, keepdims=True)
    g_w = rmax(gp)
    g_idx = rmin(jnp.where(is_g & (gp == g_w), lane_f - N_EXPERTS, 1e9))
    in_grp = (lane < N_EXPERTS) & (jnp.right_shift(lane, 3).astype(F32) == g_idx)
    el = jnp.where(in_grp, logits, NEG_BIG)
    v1 = rmax(el)
    i1 = rmin(jnp.where(in_grp & (el == v1), lane_f, 1e9))
    rest = in_grp & (lane_f != i1)
    el2 = jnp.where(rest, logits, NEG_BIG)
    v2 = rmax(el2)
    i2 = rmin(jnp.where(rest & (el2 == v2), lane_f, 1e9))
    tt = jnp.exp(v2 - v1)
    w1 = 1.0 / (1.0 + tt)
    w2 = tt / (1.0 + tt)
    sel = jnp.zeros((tile, ROUTER_LANES), F32)
    for j, col in enumerate((i1, i2, g_w * w1, g_w * w2)):
        sel = jnp.where(lane == j, col, sel)
    sel_ref[0] = sel


def _post_call(y_pool, y_attn, y_conv, x, mod, layer, w, alpha, tile):
    b, seq_len, d = x.shape
    nt = seq_len // tile
    mod_b = (lambda bi, i: (bi, 0, 0)) if mod.shape[0] == b else (lambda bi, i: (0, 0, 0))
    lw = lambda bi, i: (layer, 0, 0)
    tok = lambda width: pl.BlockSpec((1, tile, width), lambda bi, i: (bi, i, 0))
    in_specs = [tok(POOL_WIDTH), tok(ATTN_WIDTH), tok(CONV_WIDTH), tok(d),
                pl.BlockSpec((1, 6, d), mod_b),
                pl.BlockSpec((1, d, d), lw),
                pl.BlockSpec((1, 1, d), lw), pl.BlockSpec((1, 1, d), lw),
                pl.BlockSpec((1, d, 2 * ROUTER_LANES), lw), pl.BlockSpec((1, 1, ROUTER_LANES), lw)]
    out_specs = [tok(d),
                 pl.BlockSpec((tile // SUBLANES, ROW_CHUNKS, SUBLANES, LANES), lambda bi, i: (bi * nt + i, 0, 0, 0)),
                 tok(ROUTER_LANES)]
    out_shape = [jax.ShapeDtypeStruct((b, seq_len, d), F32),
                 jax.ShapeDtypeStruct((b * seq_len // SUBLANES, ROW_CHUNKS, SUBLANES, LANES), F32),
                 jax.ShapeDtypeStruct((b, seq_len, ROUTER_LANES), F32)]
    return pl.pallas_call(
        functools.partial(_post_kernel, alpha=alpha),
        grid=(b, nt), in_specs=in_specs, out_specs=out_specs, out_shape=out_shape,
        compiler_params=_params(("parallel", "parallel")), name="post",
    )(y_pool, y_attn, y_conv, x, mod, w["w_out"], w["ln1_g"], w["ln1_b"], w["router_w"], w["router_b"])


def _route_kernel(sel_ref, ri_ref, rw_ref, seg_ref, *, tile):
    lane = lax.broadcasted_iota(jnp.int32, (1, ROUTER_LANES), 1)
    lane_f = lane.astype(F32)
    blk = ROUTE_BLOCK
    r_io = lax.broadcasted_iota(jnp.int32, (blk, blk), 0)
    c_io = lax.broadcasted_iota(jnp.int32, (blk, blk), 1)
    before = jnp.where(r_io > c_io, 1.0, 0.0).astype(BF16)

    def picks(j):
        s = sel_ref[j * blk:(j + 1) * blk, :]
        return s, lane_f == s[:, 0:1], lane_f == s[:, 1:2]

    carry = jnp.zeros((1, ROUTER_LANES), F32)
    prior = []
    for j in range(tile // blk):
        _, oh1, oh2 = picks(j)
        member = jnp.where(oh1 | oh2, 1.0, 0.0)
        prior.append(_dot(before, member.astype(BF16)) + carry)
        carry = carry + jnp.sum(member, axis=0, keepdims=True)

    cnt = jnp.broadcast_to(carry, (SUBLANES, ROUTER_LANES)).astype(jnp.int32)
    padded = jnp.where(lane < N_EXPERTS, jnp.right_shift(cnt + (SUBLANES - 1), 3) * SUBLANES, 0)
    incl = padded
    step = 1
    while step < N_EXPERTS:
        incl = incl + jnp.where(lane >= step, pltpu.roll(incl, step, axis=1), 0)
        step *= 2
    start = incl - padded
    seg_ref[0] = jnp.where(lane < N_EXPERTS, cnt, pltpu.roll(start, N_EXPERTS, axis=1))
    start_f = start[0:1, :].astype(F32)

    for j in range(tile // blk):
        s, oh1, oh2 = picks(j)
        cols = []
        for oh in (oh1, oh2):
            p = jnp.sum(jnp.where(oh, start_f + prior[j], 0.0), axis=-1, keepdims=True).astype(jnp.int32)
            addr = jnp.right_shift(p, 3) * (SUBLANES * ROW_CHUNKS) + jnp.bitwise_and(p, SUBLANES - 1)
            cols.append(addr.astype(F32))
        z = jnp.where(lane == 0, cols[0], jnp.where(lane == 1, cols[1], s))
        zt = z.T
        ri_ref[0, :, j * blk:(j + 1) * blk] = zt[0:2, :].astype(jnp.int32)
        rw_ref[0, :, j * blk:(j + 1) * blk] = zt[2:4, :]


def _route_call(sel, tile):
    n_tok = sel.shape[0]
    n_tiles = n_tok // tile
    return pl.pallas_call(
        functools.partial(_route_kernel, tile=tile),
        grid=(n_tiles,),
        in_specs=[pl.BlockSpec((tile, ROUTER_LANES), lambda t: (t, 0))],
        out_specs=[pl.BlockSpec((1, TOP_K, tile), lambda t: (t, 0, 0)),
                   pl.BlockSpec((1, TOP_K, tile), lambda t: (t, 0, 0)),
                   pl.BlockSpec((1, SUBLANES, ROUTER_LANES), lambda t: (t, 0, 0))],
        out_shape=[jax.ShapeDtypeStruct((n_tiles, TOP_K, tile), jnp.int32),
                   jax.ShapeDtypeStruct((n_tiles, TOP_K, tile), F32),
                   jax.ShapeDtypeStruct((n_tiles, SUBLANES, ROUTER_LANES), jnp.int32)],
        compiler_params=_params(("parallel",)), name="route",
    )(sel)


def _row_at(start):
    return pl.ds(start, ROW_CHUNKS, stride=SUBLANES)


def _moe_kernel(ri_ref, rw_ref, seg_ref, u2c_ref, wg_ref, wu_ref, wd_ref, o_ref, sorted_ref, *, tile):
    step = pl.program_id(1)
    groups = tile // SUBLANES
    group_rows = SUBLANES * ROW_CHUNKS

    @pl.when((pl.program_id(0) == 0) & (step == 0))
    def _():
        sorted_ref[...] = jnp.zeros_like(sorted_ref)

    @pl.when(step == 0)
    def _():
        def disp_body(a, carry):
            for s in range(SUBLANES):
                row = u2c_ref[_row_at(a * group_rows + s), :]
                for k in range(TOP_K):
                    sorted_ref[_row_at(ri_ref[k * tile + a * SUBLANES + s]), :] = row
            return carry
        lax.fori_loop(0, groups, disp_body, 0)

    blk_groups = EXPERT_BLOCK // SUBLANES
    row_id = lax.broadcasted_iota(jnp.int32, (SUBLANES, 1), 0)

    def tile_at(start, g, c):
        return pl.ds(pl.multiple_of(start + (g * ROW_CHUNKS + c) * SUBLANES, SUBLANES), SUBLANES)

    experts = [step * EXPERTS_PER_STEP + j for j in range(EXPERTS_PER_STEP)]
    cnts = [seg_ref[0, 0, e] for e in experts]
    bases = [seg_ref[0, 0, N_EXPERTS + e] for e in experts]
    trip_rows = EXPERT_BLOCK * BLOCKS_PER_TRIP

    def blk_body(bi, carry):
        chains = [(j, (bi * BLOCKS_PER_TRIP + u) * EXPERT_BLOCK)
                  for j in range(EXPERTS_PER_STEP) for u in range(BLOCKS_PER_TRIP)]
        starts = [jnp.where(first < cnts[j], (bases[j] + first) * ROW_CHUNKS, 0) for j, first in chains]
        xs = [jnp.concatenate(
            [jnp.concatenate([sorted_ref[tile_at(start, g, c), :] for g in range(blk_groups)], axis=0)
             for c in range(ROW_CHUNKS)], axis=1).astype(BF16) for start in starts]
        hs = [_silu(_dot(x, wg_ref[0, j])) * _dot(x, wu_ref[0, j]) for x, (j, _) in zip(xs, chains)]
        ys = [_dot(h.astype(BF16), wd_ref[0, j]) for h, (j, _) in zip(hs, chains)]
        for (j, first), start, y in zip(chains, starts, ys):
            n_valid = cnts[j] - first
            for g in range(blk_groups):
                keep = (row_id + g * SUBLANES) < n_valid
                for c in range(ROW_CHUNKS):
                    sl = tile_at(start, g, c)
                    piece = y[g * SUBLANES:(g + 1) * SUBLANES, c * LANES:(c + 1) * LANES]
                    sorted_ref[sl, :] = jnp.where(keep, piece, sorted_ref[sl, :])
        return carry
    trips = functools.reduce(jnp.maximum, [(cnt + trip_rows - 1) // trip_rows for cnt in cnts])
    lax.fori_loop(0, trips, blk_body, 0)

    @pl.when(step == pl.num_programs(1) - 1)
    def _():
        def comb_body(a, carry):
            for s in range(SUBLANES):
                n = a * SUBLANES + s
                acc = None
                for k in range(TOP_K):
                    term = sorted_ref[_row_at(ri_ref[k * tile + n]), :] * rw_ref[k * tile + n]
                    acc = term if acc is None else acc + term
                o_ref[_row_at(a * group_rows + s), :] = acc
            return carry
        lax.fori_loop(0, groups, comb_body, 0)


def _moe_call(u2c, ri, rw, seg, layer, w, tile):
    n_tok = u2c.shape[0] // ROW_CHUNKS
    n_tiles = n_tok // tile
    d = D_MODEL
    sorted_rows = tile * TOP_K + N_EXPERTS * (SUBLANES - 1) + EXPERT_BLOCK * BLOCKS_PER_TRIP
    sorted_rows = -(-sorted_rows // SUBLANES) * SUBLANES
    eps = EXPERTS_PER_STEP
    smem = lambda shape, imap: pl.BlockSpec(shape, imap, memory_space=pltpu.SMEM)
    in_specs = [smem((TOP_K * tile,), lambda t, e: (t,)),
                smem((TOP_K * tile,), lambda t, e: (t,)),
                smem((1, SUBLANES, ROUTER_LANES), lambda t, e: (t, 0, 0)),
                pl.BlockSpec((tile * ROW_CHUNKS, LANES), lambda t, e: (t, 0), pipeline_mode=pl.Buffered(1)),
                pl.BlockSpec((1, eps, d, EXPERT_FF), lambda t, e: (layer, e, 0, 0)),
                pl.BlockSpec((1, eps, d, EXPERT_FF), lambda t, e: (layer, e, 0, 0)),
                pl.BlockSpec((1, eps, EXPERT_FF, d), lambda t, e: (layer, e, 0, 0))]
    return pl.pallas_call(
        functools.partial(_moe_kernel, tile=tile),
        grid=(n_tiles, N_EXPERTS // eps), in_specs=in_specs,
        out_specs=pl.BlockSpec((tile * ROW_CHUNKS, LANES), lambda t, e: (t, 0)),
        out_shape=jax.ShapeDtypeStruct((n_tok * ROW_CHUNKS, LANES), F32),
        scratch_shapes=[pltpu.VMEM((sorted_rows * ROW_CHUNKS, LANES), F32)],
        compiler_params=_params(("arbitrary", "arbitrary")), name="moe",
    )(ri.reshape(-1), rw.reshape(-1), seg, u2c, w["w_gate"], w["w_up"], w["w_down"])


def _ln2_kernel(m_ref, x1_ref, mod_ref, g_ref, b_ref, o_ref, *, alpha):
    tile = x1_ref.shape[0]
    moe = jnp.concatenate([m_ref[:, c, :, :].reshape(tile, LANES) for c in range(ROW_CHUNKS)], axis=1)
    gate2 = mod_ref[0, 5:6, :]
    o_ref[...] = _layer_norm(alpha * x1_ref[...] + gate2 * moe, g_ref[0], b_ref[0])


def _ln2_call(moe_c, x1, mod, layer, w, alpha, tile):
    b, seq_len, d = x1.shape
    n_tok = b * seq_len
    per_seq = seq_len // tile
    mod_b = (lambda t: (t // per_seq, 0, 0)) if mod.shape[0] == b else (lambda t: (0, 0, 0))
    lw = lambda t: (layer, 0, 0)
    out = pl.pallas_call(
        functools.partial(_ln2_kernel, alpha=alpha),
        grid=(n_tok // tile,),
        in_specs=[pl.BlockSpec((tile // SUBLANES, ROW_CHUNKS, SUBLANES, LANES), lambda t: (t, 0, 0, 0)),
                  pl.BlockSpec((tile, d), lambda t: (t, 0)),
                  pl.BlockSpec((1, 6, d), mod_b),
                  pl.BlockSpec((1, 1, d), lw), pl.BlockSpec((1, 1, d), lw)],
        out_specs=pl.BlockSpec((tile, d), lambda t: (t, 0)),
        out_shape=jax.ShapeDtypeStruct((n_tok, d), F32),
        compiler_params=_params(("parallel",)), name="ln2",
    )(moe_c.reshape(n_tok // SUBLANES, ROW_CHUNKS, SUBLANES, LANES), x1.reshape(n_tok, d), mod,
      w["ln2_g"], w["ln2_b"])
    return out.reshape(b, seq_len, d)


def _rope_tables(seq_len):
    rows = seq_len // GRID_W
    row_id = jnp.repeat(jnp.arange(rows), GRID_W).astype(F32)
    col_id = jnp.tile(jnp.arange(GRID_W), rows).astype(F32)
    inv = ROPE_THETA ** (-jnp.arange(0, AXIS_DIM, 2, dtype=F32) / AXIS_DIM)
    ang = jnp.stack([row_id[:, None] * inv, col_id[:, None] * inv], axis=1)
    cos = jnp.cos(ang)[:, :, None, :]
    sin = jnp.sin(ang)[:, :, None, :]
    shape = (seq_len, 2, 2, AXIS_DIM // 2)
    cos = jnp.broadcast_to(cos, shape).reshape(seq_len, HEAD_DIM)
    sin = (jnp.broadcast_to(sin, shape) * jnp.array([-1.0, 1.0], F32)[None, None, :, None]).reshape(seq_len, HEAD_DIM)
    return jnp.tile(cos, (1, LANES // HEAD_DIM)), jnp.tile(sin, (1, LANES // HEAD_DIM))


def _trunk_layer(x, mod, layer, w, lam_init, alpha, rope, cache, new_kv, tiles):
    outs = _front_call(x, mod, layer, w, rope, new_kv, tiles["front"])
    if new_kv is not None:
        q, k, v, kf, vf, y_pool, y_conv = outs
    else:
        q, k, v, y_pool, y_conv = outs
        kf = vf = None
    y_attn = _attn_call(q, k, v, cache, layer, w, lam_init, tiles["attn"], tiles["attn_heads"])
    x1, u2c, sel = _post_call(y_pool, y_attn, y_conv, x, mod, layer, w, alpha, tiles["post"])
    ri, rw, seg = _route_call(sel.reshape(-1, ROUTER_LANES), tiles["moe"])
    moe_c = _moe_call(u2c.reshape(-1, LANES), ri, rw, seg, layer, w, tiles["moe"])
    x2 = _ln2_call(moe_c, x1, mod, layer, w, alpha, tiles["ln2"])
    return x2, kf, vf


def kernel(x_prompt, x_sample, cache_k, cache_v, c, c_ctx, w_mod, b_mod, w_in, w_out, pool_w, pool_scale, diff_lambda, subln_g, conv_dw, conv_ln_g, conv_ln_b, conv_pw, ln1_g, ln1_b, ln2_g, ln2_b, router_grp, router_grp_b, router_exp, router_exp_b, w_gate, w_up, w_down):
    depth = w_in.shape[0]
    batch, seq, d = x_prompt.shape
    dec_batch, dec_seq, _ = x_sample.shape
    past = cache_k.shape[2]
    alpha = (2 * depth) ** 0.25

    mod_rows = 16
    cvec = jnp.concatenate([c, c_ctx[None], jnp.zeros((mod_rows - dec_batch - 1, d), F32)], axis=0)
    mod_all = _mod_call(cvec, w_mod, b_mod)

    eye = jnp.eye(len(POOL_WINDOWS), dtype=F32)
    pool_bd = (pool_w[:, :, :, None, :] * eye[None, :, None, :, None]).reshape(depth, POOL_WIDTH, POOL_WIDTH)
    router_w = jnp.concatenate(
        [router_exp, router_grp, jnp.zeros((depth, d, ROUTER_LANES - N_EXPERTS - N_GROUPS), F32)], axis=-1)
    router_hi = router_w.astype(BF16)
    router_lo = (router_w - router_hi.astype(F32)).astype(BF16)
    router_b = jnp.concatenate(
        [router_exp_b, router_grp_b, jnp.zeros((depth, ROUTER_LANES - N_EXPERTS - N_GROUPS), F32)], axis=-1)
    r3 = lambda z: z.reshape(depth, 1, z.shape[-1])
    w = {
        "w_in": w_in.astype(BF16), "w_out": w_out.astype(BF16), "pool_bd": pool_bd.astype(BF16),
        "pool_scale": r3(pool_scale), "diff_lambda": diff_lambda, "subln_g": r3(subln_g),
        "conv_dw": conv_dw, "conv_ln_g": r3(conv_ln_g), "conv_ln_b": r3(conv_ln_b),
        "conv_pw": conv_pw.astype(BF16), "ln1_g": r3(ln1_g), "ln1_b": r3(ln1_b), "ln2_g": r3(ln2_g),
        "ln2_b": r3(ln2_b), "router_w": jnp.concatenate([router_hi, router_lo], axis=-1), "router_b": r3(router_b),
        "w_gate": w_gate.astype(BF16), "w_up": w_up.astype(BF16), "w_down": w_down.astype(BF16),
    }
    rope = _rope_tables(dec_seq)
    cache = (cache_k.reshape(dec_batch, depth, past, QK_WIDTH), cache_v.reshape(dec_batch, depth, past, ATTN_WIDTH))
    ctx_tiles = {"front": seq, "attn": seq, "attn_heads": ATTN_HEADS, "post": seq, "moe": 2048, "ln2": seq}
    lat_tiles = {"front": 1024, "attn": 2048, "attn_heads": 1, "post": 1024, "moe": 2048, "ln2": 1024}

    xp, xs = x_prompt, x_sample
    new_kv = None
    for l in range(depth):
        lam_init = 0.8 - 0.6 * math.exp(-0.3 * l)
        mod_ctx = mod_all[l, dec_batch:dec_batch + 1].reshape(1, 6, d)
        mod_lat = mod_all[l, :dec_batch].reshape(dec_batch, 6, d)
        xp, kf, vf = _trunk_layer(xp, mod_ctx, l, w, lam_init, alpha, None, None, (depth, new_kv), ctx_tiles)
        new_kv = (kf, vf)
        xs, _, _ = _trunk_layer(xs, mod_lat, l, w, lam_init, alpha, rope, cache, None, lat_tiles)
    return (xp, xs, new_kv[0].reshape(batch, depth, seq, ATTN_HEADS, 2, HEAD_DIM),
            new_kv[1].reshape(batch, depth, seq, ATTN_HEADS, V_DIM))
```

```python
import functools
import math

import jax
import jax.numpy as jnp
from jax import lax
from jax.experimental import pallas as pl
from jax.experimental.pallas import tpu as pltpu

F32 = jnp.float32
BF16 = jnp.bfloat16

D_MODEL = 1024
GRID_W = 64
POOL_WIDTH = 256
POOL_WINDOWS = (2, 4, 8, 16)
POOL_GC = 64
ATTN_HEADS = 4
HEAD_DIM = 64
V_DIM = 128
ATTN_WIDTH = 512
QK_WIDTH = 512
CONV_WIDTH = 256
CONV_K = 31
IN_WIDTH = 2304
AXIS_DIM = 32
ROPE_THETA = 10000.0
N_GROUPS = 4
EXPERTS_PER_GROUP = 8
N_EXPERTS = 32
EXPERT_FF = 256
LN_EPS = 1e-5

O_A = 0
O_Q = POOL_WIDTH
O_K = O_Q + QK_WIDTH
O_V = O_K + QK_WIDTH
O_C = O_V + ATTN_WIDTH

TOP_K = 2
HALO = 16
LANES = 128
SUBLANES = 8
ROW_CHUNKS = D_MODEL // LANES
ROUTER_LANES = 128
ROUTE_BLOCK = 512
EXPERT_BLOCK = 128
BLOCKS_PER_TRIP = 2
EXPERTS_PER_STEP = 4
ATTN_ROWS = 128
LOG2E = 1.4426950408889634
NEG_BIG = -1e30
VMEM_LIMIT = 56 * 1024 * 1024


def _silu(x):
    return x * jax.nn.sigmoid(x)


def _layer_norm(x, g, b):
    mu = jnp.mean(x, axis=-1, keepdims=True)
    xc = x - mu
    var = jnp.mean(xc * xc, axis=-1, keepdims=True)
    return xc * lax.rsqrt(var + LN_EPS) * g + b


def _dot(a, b):
    return jnp.dot(a, b, preferred_element_type=F32)


def _params(sem):
    return pltpu.CompilerParams(dimension_semantics=sem, vmem_limit_bytes=VMEM_LIMIT)


def _mod_kernel(c_ref, w_ref, b_ref, o_ref):
    s = _silu(c_ref[...])
    o_ref[0] = _dot(s.astype(BF16), w_ref[0].astype(BF16)) + b_ref[0]


def _mod_call(cvec, w_mod, b_mod):
    depth, d, n6 = w_mod.shape
    rows = cvec.shape[0]
    nb = 1536
    return pl.pallas_call(
        _mod_kernel,
        grid=(depth, n6 // nb),
        in_specs=[
            pl.BlockSpec((rows, d), lambda l, j: (0, 0)),
            pl.BlockSpec((1, d, nb), lambda l, j: (l, 0, j)),
            pl.BlockSpec((1, 1, nb), lambda l, j: (l, 0, j)),
        ],
        out_specs=pl.BlockSpec((1, rows, nb), lambda l, j: (l, 0, j)),
        out_shape=jax.ShapeDtypeStruct((depth, rows, n6), F32),
        compiler_params=_params(("parallel", "parallel")),
        name="mod",
    )(cvec, w_mod, b_mod.reshape(depth, 1, n6))


def _front_kernel(*refs, seq_len, tile, has_rope, want_f32_kv, has_prev_kv, kv_slot):
    it = iter(refs)
    x_ref, xp_ref, xn_ref, mod_ref, win_ref = next(it), next(it), next(it), next(it), next(it)
    poolw_ref, pscale_ref, dw_ref, cg_ref, cb_ref, pw_ref = (next(it) for _ in range(6))
    if has_rope:
        cos_ref, sin_ref = next(it), next(it)
    if has_prev_kv:
        next(it), next(it)
    q_ref, k_ref, v_ref = next(it), next(it), next(it)
    if want_f32_kv:
        kf_ref, vf_ref = next(it), next(it)
    yp_ref, yc_ref = next(it), next(it)

    i = pl.program_id(1)
    n_ext = tile + 2 * HALO
    shift = mod_ref[0, 0:1, :]
    scale = mod_ref[0, 1:2, :]
    xe = jnp.concatenate([xp_ref[0], x_ref[0], xn_ref[0]], axis=0)
    u = (xe * (1.0 + scale) + shift).astype(BF16)
    proj = _dot(u, win_ref[0])

    row = lax.broadcasted_iota(jnp.int32, (n_ext, 1), 0)
    gpos = i * tile - HALO + row
    valid = (gpos >= 0) & (gpos < seq_len)

    a_ext = jnp.where(valid, proj[:, O_A:O_A + POOL_WIDTH], 0.0)

    def sh(z, s):
        return pltpu.roll(z, s % n_ext, axis=0)

    p2 = a_ext + sh(a_ext, 1)
    p4 = sh(p2, 1) + sh(p2, -1)
    p8 = sh(p4, 2) + sh(p4, -2)
    p16 = sh(p8, 4) + sh(p8, -4)
    t = gpos[HALO:HALO + tile]
    lane = lax.broadcasted_iota(jnp.int32, (1, POOL_WIDTH), 1)
    mean = None
    for g, (w, p) in enumerate(zip(POOL_WINDOWS, (p2, p4, p8, p16))):
        cnt = jnp.minimum(t + w // 2, seq_len) - jnp.maximum(t - w // 2, 0)
        m = p[HALO:HALO + tile] * (1.0 / cnt.astype(F32))
        mean = m if mean is None else jnp.where(lane >= g * POOL_GC, m, mean)
    pooled = mean - a_ext[HALO:HALO + tile]
    y_pool = _dot(pooled.astype(BF16), poolw_ref[0]) * pscale_ref[0]
    yp_ref[0] = y_pool.astype(BF16)

    ca = proj[:, O_C:O_C + CONV_WIDTH]
    cb = proj[:, O_C + CONV_WIDTH:O_C + 2 * CONV_WIDTH]
    h = jnp.where(valid, ca * jax.nn.sigmoid(cb), 0.0)
    acc = None
    for r in range(SUBLANES):
        hr = h if r == 0 else pltpu.roll(h, n_ext - r, axis=0)
        for j in range(CONV_K):
            off = HALO - CONV_K // 2 + j
            if off % SUBLANES != r:
                continue
            term = hr[off - r:off - r + tile] * dw_ref[0, j:j + 1, :]
            acc = term if acc is None else acc + term
    hn = _silu(_layer_norm(acc, cg_ref[0], cb_ref[0]))
    yc_ref[0] = _dot(hn.astype(BF16), pw_ref[0]).astype(BF16)

    qf = proj[HALO:HALO + tile, O_Q:O_Q + QK_WIDTH]
    kf = proj[HALO:HALO + tile, O_K:O_K + QK_WIDTH]
    vf = proj[HALO:HALO + tile, O_V:O_V + ATTN_WIDTH]
    if want_f32_kv:
        for slot in range(kf_ref.shape[1]):
            mine = has_prev_kv or slot == kv_slot
            kf_ref[0, slot] = kf if mine else jnp.zeros_like(kf)
            vf_ref[0, slot] = vf if mine else jnp.zeros_like(vf)
    if has_rope:
        cos = cos_ref[...]
        sin = sin_ref[...]
        l128 = lax.broadcasted_iota(jnp.int32, (1, LANES), 1)
        first_half = (l128 & (AXIS_DIM - 1)) < (AXIS_DIM // 2)

        def rope(z):
            outs = []
            for c in range(QK_WIDTH // LANES):
                zc = z[:, c * LANES:(c + 1) * LANES]
                partner = jnp.where(first_half,
                                    pltpu.roll(zc, LANES - AXIS_DIM // 2, axis=1),
                                    pltpu.roll(zc, AXIS_DIM // 2, axis=1))
                outs.append(zc * cos + partner * sin)
            return jnp.concatenate(outs, axis=1)

        qf = rope(qf)
        kf = rope(kf)
    q_ref[0] = (qf * (LOG2E / math.sqrt(HEAD_DIM))).astype(BF16)
    k_ref[0] = kf.astype(BF16)
    v_ref[0] = vf.astype(BF16)


def _front_call(x, mod, layer, w, rope, new_kv, tile):
    want_f32_kv = new_kv is not None
    depth, prev_kv = new_kv if want_f32_kv else (None, None)
    b, seq_len, d = x.shape
    nt = seq_len // tile
    hb = tile // HALO
    n_hblk = seq_len // HALO
    mod_b = (lambda bi, i: (bi, 0, 0)) if mod.shape[0] == b else (lambda bi, i: (0, 0, 0))
    lw = lambda bi, i: (layer, 0, 0)
    in_specs = [
        pl.BlockSpec((1, tile, d), lambda bi, i: (bi, i, 0)),
        pl.BlockSpec((1, HALO, d), lambda bi, i: (bi, jnp.maximum(i * hb - 1, 0), 0)),
        pl.BlockSpec((1, HALO, d), lambda bi, i: (bi, jnp.minimum((i + 1) * hb, n_hblk - 1), 0)),
        pl.BlockSpec((1, 6, d), mod_b),
        pl.BlockSpec((1, d, IN_WIDTH), lw),
        pl.BlockSpec((1, POOL_WIDTH, POOL_WIDTH), lw),
        pl.BlockSpec((1, 1, POOL_WIDTH), lw),
        pl.BlockSpec((1, CONV_K, CONV_WIDTH), lw),
        pl.BlockSpec((1, 1, CONV_WIDTH), lw),
        pl.BlockSpec((1, 1, CONV_WIDTH), lw),
        pl.BlockSpec((1, CONV_WIDTH, CONV_WIDTH), lw),
    ]
    args = [x, x, x, mod, w["w_in"], w["pool_bd"], w["pool_scale"], w["conv_dw"], w["conv_ln_g"],
            w["conv_ln_b"], w["conv_pw"]]
    if rope is not None:
        in_specs += [pl.BlockSpec((tile, LANES), lambda bi, i: (i, 0))] * 2
        args += list(rope)
    tok = lambda width: pl.BlockSpec((1, tile, width), lambda bi, i: (bi, i, 0))
    out_specs = [tok(QK_WIDTH), tok(QK_WIDTH), tok(ATTN_WIDTH)]
    out_shape = [jax.ShapeDtypeStruct((b, seq_len, QK_WIDTH), BF16)] * 3
    aliases = {}
    if want_f32_kv:
        if prev_kv is not None:
            aliases = {len(args): len(out_specs), len(args) + 1: len(out_specs) + 1}
            in_specs += [pl.BlockSpec(memory_space=pl.ANY)] * 2
            args += list(prev_kv)
        if prev_kv is None:
            layer_tok = lambda width: pl.BlockSpec((1, depth, tile, width), lambda bi, i: (bi, 0, i, 0))
        else:
            layer_tok = lambda width: pl.BlockSpec((1, 1, tile, width), lambda bi, i: (bi, layer, i, 0))
        out_specs += [layer_tok(QK_WIDTH), layer_tok(ATTN_WIDTH)]
        out_shape += [jax.ShapeDtypeStruct((b, depth, seq_len, QK_WIDTH), F32)] * 2
    out_specs += [tok(POOL_WIDTH), tok(CONV_WIDTH)]
    out_shape += [jax.ShapeDtypeStruct((b, seq_len, POOL_WIDTH), BF16)] * 2
    kern = functools.partial(_front_kernel, seq_len=seq_len, tile=tile, has_rope=rope is not None,
                             want_f32_kv=want_f32_kv, has_prev_kv=prev_kv is not None, kv_slot=layer)
    return pl.pallas_call(
        kern, grid=(b, nt), in_specs=in_specs, out_specs=out_specs, out_shape=out_shape,
        input_output_aliases=aliases,
        compiler_params=_params(("parallel", "parallel")), name="front",
    )(*args)


def _attn_kernel(*refs, lam_init, has_cache):
    if has_cache:
        lam_ref, g_ref, q_ref, k_ref, v_ref, ck_ref, cv_ref, o_ref = refs
    else:
        lam_ref, g_ref, q_ref, k_ref, v_ref, o_ref = refs
    lp = lam_ref[0]
    lam = (jnp.exp(jnp.sum(lp[0:1] * lp[1:2], axis=-1, keepdims=True))
           - jnp.exp(jnp.sum(lp[2:3] * lp[3:4], axis=-1, keepdims=True)) + lam_init)

    lane = lax.broadcasted_iota(jnp.int32, (1, LANES), 1)
    nt = (((1,), (1,)), ((), ()))
    heads = q_ref.shape[2] // LANES
    kv = []
    for hh in range(heads):
        cols = slice(hh * LANES, (hh + 1) * LANES)
        keys = [k_ref[0, :, cols]]
        vals = [v_ref[0, :, cols]]
        if has_cache:
            keys.append(ck_ref[0, 0, :, cols].astype(BF16))
            vals.append(cv_ref[0, 0, :, cols].astype(BF16))
        kv.append((keys, vals))

    def scores(q, keys):
        zero = jnp.zeros_like(q)
        return [[lax.dot_general(qm, kk, nt, preferred_element_type=F32) for kk in keys]
                for qm in (jnp.where(lane < HEAD_DIM, q, zero), jnp.where(lane >= HEAD_DIM, q, zero))]

    def weights(score_maps):
        parts = []
        for s in score_maps:
            m = functools.reduce(jnp.maximum, [jnp.max(z, axis=-1, keepdims=True) for z in s])
            e = [jnp.exp2(z - m) for z in s]
            den = functools.reduce(jnp.add, [jnp.sum(z, axis=-1, keepdims=True) for z in e])
            parts.append((e, den))
        (e1, den1), (e2, den2) = parts
        c = lam * den1 / den2
        return [(z1 - z2 * c).astype(BF16) for z1, z2 in zip(e1, e2)], den1

    def values(a_parts, den1, vals):
        o = functools.reduce(jnp.add, [_dot(a, vv) for a, vv in zip(a_parts, vals)])
        o = o * (1.0 / den1)
        o = o * lax.rsqrt(jnp.mean(o * o, axis=-1, keepdims=True) + LN_EPS) * g_ref[0] * (1.0 - lam_init)
        return o.astype(BF16)

    tq = q_ref.shape[1]
    n_sub = max(1, tq // ATTN_ROWS)
    step = tq // n_sub
    items = [(hh, j) for hh in range(heads) for j in range(n_sub)]

    def item_scores(item):
        hh, j = item
        return scores(q_ref[0, j * step:(j + 1) * step, hh * LANES:(hh + 1) * LANES], kv[hh][0])

    def item_store(item, a_parts, den1):
        hh, j = item
        o_ref[0, j * step:(j + 1) * step, hh * LANES:(hh + 1) * LANES] = values(a_parts, den1, kv[hh][1])

    pending = item_scores(items[0])
    softmaxed = None
    for idx, item in enumerate(items):
        current = pending
        if idx + 1 < len(items):
            pending = item_scores(items[idx + 1])
        done = weights(current)
        if softmaxed is not None:
            item_store(items[idx - 1], *softmaxed)
        softmaxed = done
    item_store(items[-1], *softmaxed)


def _attn_call(q, k, v, cache, layer, w, lam_init, tq, heads):
    b, seq_len, _ = q.shape
    width = heads * LANES
    lw = lambda bi, h, i: (layer, 0, 0)
    in_specs = [
        pl.BlockSpec((1, 4, HEAD_DIM), lw),
        pl.BlockSpec((1, 1, V_DIM), lw),
        pl.BlockSpec((1, tq, width), lambda bi, h, i: (bi, i, h)),
        pl.BlockSpec((1, seq_len, width), lambda bi, h, i: (bi, 0, h)),
        pl.BlockSpec((1, seq_len, width), lambda bi, h, i: (bi, 0, h)),
    ]
    args = [w["diff_lambda"], w["subln_g"], q, k, v]
    if cache is not None:
        past = cache[0].shape[2]
        cspec = pl.BlockSpec((1, 1, past, width), lambda bi, h, i: (bi, layer, 0, h))
        in_specs += [cspec, cspec]
        args += list(cache)
    kern = functools.partial(_attn_kernel, lam_init=lam_init, has_cache=cache is not None)
    return pl.pallas_call(
        kern, grid=(b, ATTN_HEADS // heads, seq_len // tq), in_specs=in_specs,
        out_specs=pl.BlockSpec((1, tq, width), lambda bi, h, i: (bi, i, h)),
        out_shape=jax.ShapeDtypeStruct((b, seq_len, ATTN_WIDTH), BF16),
        compiler_params=_params(("parallel", "parallel", "parallel")), name="attn",
    )(*args)


def _post_kernel(yp_ref, ya_ref, yc_ref, x_ref, mod_ref, wout_ref, g_ref, b_ref, wr_ref, br_ref,
                 x1_ref, u2c_ref, sel_ref, *, alpha):
    o1 = POOL_WIDTH
    o2 = o1 + ATTN_WIDTH
    y = (_dot(yp_ref[0], wout_ref[0, 0:o1, :]) + _dot(ya_ref[0], wout_ref[0, o1:o2, :])
         + _dot(yc_ref[0], wout_ref[0, o2:, :]))
    gate1 = mod_ref[0, 2:3, :]
    shift2 = mod_ref[0, 3:4, :]
    scale2 = mod_ref[0, 4:5, :]
    x1 = _layer_norm(alpha * x_ref[0] + gate1 * y, g_ref[0], b_ref[0])
    x1_ref[0] = x1
    u2 = x1 * (1.0 + scale2) + shift2
    tile = u2.shape[0]
    for c in range(ROW_CHUNKS):
        u2c_ref[:, c, :, :] = u2[:, c * LANES:(c + 1) * LANES].reshape(tile // SUBLANES, SUBLANES, LANES)

    u_hi = u2.astype(BF16)
    u_lo = (u2 - u_hi.astype(F32)).astype(BF16)
    both = _dot(u_hi, wr_ref[0]) + _dot(u_lo, wr_ref[0])
    logits = both[:, :ROUTER_LANES] + both[:, ROUTER_LANES:] + br_ref[0]
    lane = lax.broadcasted_iota(jnp.int32, (1, ROUTER_LANES), 1)
    lane_f = lane.astype(F32)
    is_g = (lane >= N_EXPERTS) & (lane < N_EXPERTS + N_GROUPS)
    rmax = lambda z: jnp.max(z, axis=-1, keepdims=True)
    rmin = lambda z: jnp.min(z, axis=-1, keepdims=True)
    gl = jnp.where(is_g, logits, NEG_BIG)
    ge = jnp.where(is_g, jnp.exp(gl - rmax(gl)), 0.0)
    gp = ge / jnp.sum(ge, axis=-1, keepdims=True)
    g_w = rmax(gp)
    g_idx = rmin(jnp.where(is_g & (gp == g_w), lane_f - N_EXPERTS, 1e9))
    in_grp = (lane < N_EXPERTS) & (jnp.right_shift(lane, 3).astype(F32) == g_idx)
    el = jnp.where(in_grp, logits, NEG_BIG)
    v1 = rmax(el)
    i1 = rmin(jnp.where(in_grp & (el == v1), lane_f, 1e9))
    rest = in_grp & (lane_f != i1)
    el2 = jnp.where(rest, logits, NEG_BIG)
    v2 = rmax(el2)
    i2 = rmin(jnp.where(rest & (el2 == v2), lane_f, 1e9))
    tt = jnp.exp(v2 - v1)
    w1 = 1.0 / (1.0 + tt)
    w2 = tt / (1.0 + tt)
    sel = jnp.zeros((tile, ROUTER_LANES), F32)
    for j, col in enumerate((i1, i2, g_w * w1, g_w * w2)):
        sel = jnp.where(lane == j, col, sel)
    sel_ref[0] = sel


def _post_call(y_pool, y_attn, y_conv, x, mod, layer, w, alpha, tile):
    b, seq_len, d = x.shape
    nt = seq_len // tile
    mod_b = (lambda bi, i: (bi, 0, 0)) if mod.shape[0] == b else (lambda bi, i: (0, 0, 0))
    lw = lambda bi, i: (layer, 0, 0)
    tok = lambda width: pl.BlockSpec((1, tile, width), lambda bi, i: (bi, i, 0))
    in_specs = [tok(POOL_WIDTH), tok(ATTN_WIDTH), tok(CONV_WIDTH), tok(d),
                pl.BlockSpec((1, 6, d), mod_b),
                pl.BlockSpec((1, d, d), lw),
                pl.BlockSpec((1, 1, d), lw), pl.BlockSpec((1, 1, d), lw),
                pl.BlockSpec((1, d, 2 * ROUTER_LANES), lw), pl.BlockSpec((1, 1, ROUTER_LANES), lw)]
    out_specs = [tok(d),
                 pl.BlockSpec((tile // SUBLANES, ROW_CHUNKS, SUBLANES, LANES), lambda bi, i: (bi * nt + i, 0, 0, 0)),
                 tok(ROUTER_LANES)]
    out_shape = [jax.ShapeDtypeStruct((b, seq_len, d), F32),
                 jax.ShapeDtypeStruct((b * seq_len // SUBLANES, ROW_CHUNKS, SUBLANES, LANES), F32),
                 jax.ShapeDtypeStruct((b, seq_len, ROUTER_LANES), F32)]
    return pl.pallas_call(
        functools.partial(_post_kernel, alpha=alpha),
        grid=(b, nt), in_specs=in_specs, out_specs=out_specs, out_shape=out_shape,
        compiler_params=_params(("parallel", "parallel")), name="post",
    )(y_pool, y_attn, y_conv, x, mod, w["w_out"], w["ln1_g"], w["ln1_b"], w["router_w"], w["router_b"])


def _route_kernel(sel_ref, ri_ref, rw_ref, seg_ref, *, tile):
    lane = lax.broadcasted_iota(jnp.int32, (1, ROUTER_LANES), 1)
    lane_f = lane.astype(F32)
    blk = ROUTE_BLOCK
    r_io = lax.broadcasted_iota(jnp.int32, (blk, blk), 0)
    c_io = lax.broadcasted_iota(jnp.int32, (blk, blk), 1)
    before = jnp.where(r_io > c_io, 1.0, 0.0).astype(BF16)

    def picks(j):
        s = sel_ref[j * blk:(j + 1) * blk, :]
        return s, lane_f == s[:, 0:1], lane_f == s[:, 1:2]

    carry = jnp.zeros((1, ROUTER_LANES), F32)
    prior = []
    for j in range(tile // blk):
        _, oh1, oh2 = picks(j)
        member = jnp.where(oh1 | oh2, 1.0, 0.0)
        prior.append(_dot(before, member.astype(BF16)) + carry)
        carry = carry + jnp.sum(member, axis=0, keepdims=True)

    cnt = jnp.broadcast_to(carry, (SUBLANES, ROUTER_LANES)).astype(jnp.int32)
    padded = jnp.where(lane < N_EXPERTS, jnp.right_shift(cnt + (SUBLANES - 1), 3) * SUBLANES, 0)
    incl = padded
    step = 1
    while step < N_EXPERTS:
        incl = incl + jnp.where(lane >= step, pltpu.roll(incl, step, axis=1), 0)
        step *= 2
    start = incl - padded
    seg_ref[0] = jnp.where(lane < N_EXPERTS, cnt, pltpu.roll(start, N_EXPERTS, axis=1))
    start_f = start[0:1, :].astype(F32)

    for j in range(tile // blk):
        s, oh1, oh2 = picks(j)
        cols = []
        for oh in (oh1, oh2):
            p = jnp.sum(jnp.where(oh, start_f + prior[j], 0.0), axis=-1, keepdims=True).astype(jnp.int32)
            addr = jnp.right_shift(p, 3) * (SUBLANES * ROW_CHUNKS) + jnp.bitwise_and(p, SUBLANES - 1)
            cols.append(addr.astype(F32))
        z = jnp.where(lane == 0, cols[0], jnp.where(lane == 1, cols[1], s))
        zt = z.T
        ri_ref[0, :, j * blk:(j + 1) * blk] = zt[0:2, :].astype(jnp.int32)
        rw_ref[0, :, j * blk:(j + 1) * blk] = zt[2:4, :]


def _route_call(sel, tile):
    n_tok = sel.shape[0]
    n_tiles = n_tok // tile
    return pl.pallas_call(
        functools.partial(_route_kernel, tile=tile),
        grid=(n_tiles,),
        in_specs=[pl.BlockSpec((tile, ROUTER_LANES), lambda t: (t, 0))],
        out_specs=[pl.BlockSpec((1, TOP_K, tile), lambda t: (t, 0, 0)),
                   pl.BlockSpec((1, TOP_K, tile), lambda t: (t, 0, 0)),
                   pl.BlockSpec((1, SUBLANES, ROUTER_LANES), lambda t: (t, 0, 0))],
        out_shape=[jax.ShapeDtypeStruct((n_tiles, TOP_K, tile), jnp.int32),
                   jax.ShapeDtypeStruct((n_tiles, TOP_K, tile), F32),
                   jax.ShapeDtypeStruct((n_tiles, SUBLANES, ROUTER_LANES), jnp.int32)],
        compiler_params=_params(("parallel",)), name="route",
    )(sel)


def _row_at(start):
    return pl.ds(start, ROW_CHUNKS, stride=SUBLANES)


def _moe_kernel(ri_ref, rw_ref, seg_ref, u2c_ref, wg_ref, wu_ref, wd_ref, o_ref, sorted_ref, *, tile):
    step = pl.program_id(1)
    groups = tile // SUBLANES
    group_rows = SUBLANES * ROW_CHUNKS

    @pl.when((pl.program_id(0) == 0) & (step == 0))
    def _():
        sorted_ref[...] = jnp.zeros_like(sorted_ref)

    @pl.when(step == 0)
    def _():
        def disp_body(a, carry):
            for s in range(SUBLANES):
                row = u2c_ref[_row_at(a * group_rows + s), :]
                for k in range(TOP_K):
                    sorted_ref[_row_at(ri_ref[k * tile + a * SUBLANES + s]), :] = row
            return carry
        lax.fori_loop(0, groups, disp_body, 0)

    blk_groups = EXPERT_BLOCK // SUBLANES
    row_id = lax.broadcasted_iota(jnp.int32, (SUBLANES, 1), 0)

    def tile_at(start, g, c):
        return pl.ds(pl.multiple_of(start + (g * ROW_CHUNKS + c) * SUBLANES, SUBLANES), SUBLANES)

    experts = [step * EXPERTS_PER_STEP + j for j in range(EXPERTS_PER_STEP)]
    cnts = [seg_ref[0, 0, e] for e in experts]
    bases = [seg_ref[0, 0, N_EXPERTS + e] for e in experts]
    trip_rows = EXPERT_BLOCK * BLOCKS_PER_TRIP

    def blk_body(bi, carry):
        chains = [(j, (bi * BLOCKS_PER_TRIP + u) * EXPERT_BLOCK)
                  for j in range(EXPERTS_PER_STEP) for u in range(BLOCKS_PER_TRIP)]
        starts = [jnp.where(first < cnts[j], (bases[j] + first) * ROW_CHUNKS, 0) for j, first in chains]
        xs = [jnp.concatenate(
            [jnp.concatenate([sorted_ref[tile_at(start, g, c), :] for g in range(blk_groups)], axis=0)
             for c in range(ROW_CHUNKS)], axis=1).astype(BF16) for start in starts]
        hs = [_silu(_dot(x, wg_ref[0, j])) * _dot(x, wu_ref[0, j]) for x, (j, _) in zip(xs, chains)]
        ys = [_dot(h.astype(BF16), wd_ref[0, j]) for h, (j, _) in zip(hs, chains)]
        for (j, first), start, y in zip(chains, starts, ys):
            n_valid = cnts[j] - first
            for g in range(blk_groups):
                keep = (row_id + g * SUBLANES) < n_valid
                for c in range(ROW_CHUNKS):
                    sl = tile_at(start, g, c)
                    piece = y[g * SUBLANES:(g + 1) * SUBLANES, c * LANES:(c + 1) * LANES]
                    sorted_ref[sl, :] = jnp.where(keep, piece, sorted_ref[sl, :])
        return carry
    trips = functools.reduce(jnp.maximum, [(cnt + trip_rows - 1) // trip_rows for cnt in cnts])
    lax.fori_loop(0, trips, blk_body, 0)

    @pl.when(step == pl.num_programs(1) - 1)
    def _():
        def comb_body(a, carry):
            for s in range(SUBLANES):
                n = a * SUBLANES + s
                acc = None
                for k in range(TOP_K):
                    term = sorted_ref[_row_at(ri_ref[k * tile + n]), :] * rw_ref[k * tile + n]
                    acc = term if acc is None else acc + term
                o_ref[_row_at(a * group_rows + s), :] = acc
            return carry
        lax.fori_loop(0, groups, comb_body, 0)


def _moe_call(u2c, ri, rw, seg, layer, w, tile):
    n_tok = u2c.shape[0] // ROW_CHUNKS
    n_tiles = n_tok // tile
    d = D_MODEL
    sorted_rows = tile * TOP_K + N_EXPERTS * (SUBLANES - 1) + EXPERT_BLOCK * BLOCKS_PER_TRIP
    sorted_rows = -(-sorted_rows // SUBLANES) * SUBLANES
    eps = EXPERTS_PER_STEP
    smem = lambda shape, imap: pl.BlockSpec(shape, imap, memory_space=pltpu.SMEM)
    in_specs = [smem((TOP_K * tile,), lambda t, e: (t,)),
                smem((TOP_K * tile,), lambda t, e: (t,)),
                smem((1, SUBLANES, ROUTER_LANES), lambda t, e: (t, 0, 0)),
                pl.BlockSpec((tile * ROW_CHUNKS, LANES), lambda t, e: (t, 0), pipeline_mode=pl.Buffered(1)),
                pl.BlockSpec((1, eps, d, EXPERT_FF), lambda t, e: (layer, e, 0, 0)),
                pl.BlockSpec((1, eps, d, EXPERT_FF), lambda t, e: (layer, e, 0, 0)),
                pl.BlockSpec((1, eps, EXPERT_FF, d), lambda t, e: (layer, e, 0, 0))]
    return pl.pallas_call(
        functools.partial(_moe_kernel, tile=tile),
        grid=(n_tiles, N_EXPERTS // eps), in_specs=in_specs,
        out_specs=pl.BlockSpec((tile * ROW_CHUNKS, LANES), lambda t, e: (t, 0)),
        out_shape=jax.ShapeDtypeStruct((n_tok * ROW_CHUNKS, LANES), F32),
        scratch_shapes=[pltpu.VMEM((sorted_rows * ROW_CHUNKS, LANES), F32)],
        compiler_params=_params(("arbitrary", "arbitrary")), name="moe",
    )(ri.reshape(-1), rw.reshape(-1), seg, u2c, w["w_gate"], w["w_up"], w["w_down"])


def _ln2_kernel(m_ref, x1_ref, mod_ref, g_ref, b_ref, o_ref, *, alpha):
    tile = x1_ref.shape[0]
    moe = jnp.concatenate([m_ref[:, c, :, :].reshape(tile, LANES) for c in range(ROW_CHUNKS)], axis=1)
    gate2 = mod_ref[0, 5:6, :]
    o_ref[...] = _layer_norm(alpha * x1_ref[...] + gate2 * moe, g_ref[0], b_ref[0])


def _ln2_call(moe_c, x1, mod, layer, w, alpha, tile):
    b, seq_len, d = x1.shape
    n_tok = b * seq_len
    per_seq = seq_len // tile
    mod_b = (lambda t: (t // per_seq, 0, 0)) if mod.shape[0] == b else (lambda t: (0, 0, 0))
    lw = lambda t: (layer, 0, 0)
    out = pl.pallas_call(
        functools.partial(_ln2_kernel, alpha=alpha),
        grid=(n_tok // tile,),
        in_specs=[pl.BlockSpec((tile // SUBLANES, ROW_CHUNKS, SUBLANES, LANES), lambda t: (t, 0, 0, 0)),
                  pl.BlockSpec((tile, d), lambda t: (t, 0)),
                  pl.BlockSpec((1, 6, d), mod_b),
                  pl.BlockSpec((1, 1, d), lw), pl.BlockSpec((1, 1, d), lw)],
        out_specs=pl.BlockSpec((tile, d), lambda t: (t, 0)),
        out_shape=jax.ShapeDtypeStruct((n_tok, d), F32),
        compiler_params=_params(("parallel",)), name="ln2",
    )(moe_c.reshape(n_tok // SUBLANES, ROW_CHUNKS, SUBLANES, LANES), x1.reshape(n_tok, d), mod,
      w["ln2_g"], w["ln2_b"])
    return out.reshape(b, seq_len, d)


def _rope_tables(seq_len):
    rows = seq_len // GRID_W
    row_id = jnp.repeat(jnp.arange(rows), GRID_W).astype(F32)
    col_id = jnp.tile(jnp.arange(GRID_W), rows).astype(F32)
    inv = ROPE_THETA ** (-jnp.arange(0, AXIS_DIM, 2, dtype=F32) / AXIS_DIM)
    ang = jnp.stack([row_id[:, None] * inv, col_id[:, None] * inv], axis=1)
    cos = jnp.cos(ang)[:, :, None, :]
    sin = jnp.sin(ang)[:, :, None, :]
    shape = (seq_len, 2, 2, AXIS_DIM // 2)
    cos = jnp.broadcast_to(cos, shape).reshape(seq_len, HEAD_DIM)
    sin = (jnp.broadcast_to(sin, shape) * jnp.array([-1.0, 1.0], F32)[None, None, :, None]).reshape(seq_len, HEAD_DIM)
    return jnp.tile(cos, (1, LANES // HEAD_DIM)), jnp.tile(sin, (1, LANES // HEAD_DIM))


def _trunk_layer(x, mod, layer, w, lam_init, alpha, rope, cache, new_kv, tiles):
    outs = _front_call(x, mod, layer, w, rope, new_kv, tiles["front"])
    if new_kv is not None:
        q, k, v, kf, vf, y_pool, y_conv = outs
    else:
        q, k, v, y_pool, y_conv = outs
        kf = vf = None
    y_attn = _attn_call(q, k, v, cache, layer, w, lam_init, tiles["attn"], tiles["attn_heads"])
    x1, u2c, sel = _post_call(y_pool, y_attn, y_conv, x, mod, layer, w, alpha, tiles["post"])
    ri, rw, seg = _route_call(sel.reshape(-1, ROUTER_LANES), tiles["moe"])
    moe_c = _moe_call(u2c.reshape(-1, LANES), ri, rw, seg, layer, w, tiles["moe"])
    x2 = _ln2_call(moe_c, x1, mod, layer, w, alpha, tiles["ln2"])
    return x2, kf, vf


def kernel(x_prompt, x_sample, cache_k, cache_v, c, c_ctx, w_mod, b_mod, w_in, w_out, pool_w, pool_scale, diff_lambda, subln_g, conv_dw, conv_ln_g, conv_ln_b, conv_pw, ln1_g, ln1_b, ln2_g, ln2_b, router_grp, router_grp_b, router_exp, router_exp_b, w_gate, w_up, w_down):
    depth = w_in.shape[0]
    batch, seq, d = x_prompt.shape
    dec_batch, dec_seq, _ = x_sample.shape
    past = cache_k.shape[2]
    alpha = (2 * depth) ** 0.25

    mod_rows = 16
    cvec = jnp.concatenate([c, c_ctx[None], jnp.zeros((mod_rows - dec_batch - 1, d), F32)], axis=0)
    mod_all = _mod_call(cvec, w_mod, b_mod)

    eye = jnp.eye(len(POOL_WINDOWS), dtype=F32)
    pool_bd = (pool_w[:, :, :, None, :] * eye[None, :, None, :, None]).reshape(depth, POOL_WIDTH, POOL_WIDTH)
    router_w = jnp.concatenate(
        [router_exp, router_grp, jnp.zeros((depth, d, ROUTER_LANES - N_EXPERTS - N_GROUPS), F32)], axis=-1)
    router_hi = router_w.astype(BF16)
    router_lo = (router_w - router_hi.astype(F32)).astype(BF16)
    router_b = jnp.concatenate(
        [router_exp_b, router_grp_b, jnp.zeros((depth, ROUTER_LANES - N_EXPERTS - N_GROUPS), F32)], axis=-1)
    r3 = lambda z: z.reshape(depth, 1, z.shape[-1])
    w = {
        "w_in": w_in.astype(BF16), "w_out": w_out.astype(BF16), "pool_bd": pool_bd.astype(BF16),
        "pool_scale": r3(pool_scale), "diff_lambda": diff_lambda, "subln_g": r3(subln_g),
        "conv_dw": conv_dw, "conv_ln_g": r3(conv_ln_g), "conv_ln_b": r3(conv_ln_b),
        "conv_pw": conv_pw.astype(BF16), "ln1_g": r3(ln1_g), "ln1_b": r3(ln1_b), "ln2_g": r3(ln2_g),
        "ln2_b": r3(ln2_b), "router_w": jnp.concatenate([router_hi, router_lo], axis=-1), "router_b": r3(router_b),
        "w_gate": w_gate.astype(BF16), "w_up": w_up.astype(BF16), "w_down": w_down.astype(BF16),
    }
    rope = _rope_tables(dec_seq)
    cache = (cache_k.reshape(dec_batch, depth, past, QK_WIDTH), cache_v.reshape(dec_batch, depth, past, ATTN_WIDTH))
    ctx_tiles = {"front": seq, "attn": seq, "attn_heads": ATTN_HEADS, "post": seq, "moe": 2048, "ln2": seq}
    lat_tiles = {"front": 1024, "attn": 2048, "attn_heads": 1, "post": 1024, "moe": 2048, "ln2": 1024}

    xp, xs = x_prompt, x_sample
    new_kv = None
    for l in range(depth):
        lam_init = 0.8 - 0.6 * math.exp(-0.3 * l)
        mod_ctx = mod_all[l, dec_batch:dec_batch + 1].reshape(1, 6, d)
        mod_lat = mod_all[l, :dec_batch].reshape(dec_batch, 6, d)
        xp, kf, vf = _trunk_layer(xp, mod_ctx, l, w, lam_init, alpha, None, None, (depth, new_kv), ctx_tiles)
        new_kv = (kf, vf)
        xs, _, _ = _trunk_layer(xs, mod_lat, l, w, lam_init, alpha, rope, cache, None, lat_tiles)
    return (xp, xs, new_kv[0].reshape(batch, depth, seq, ATTN_HEADS, 2, HEAD_DIM),
            new_kv[1].reshape(batch, depth, seq, ATTN_HEADS, V_DIM))
```
